```python
import math
import jax
import jax.numpy as jnp
from jax import lax
import numpy as np

D_MODEL = 1024
BATCH = 32
SEQ = 2048
DEPTH = 4

CTX_LEN = 256
GRID_W = 64
EPS = 1e-6
F32 = jnp.float32

BRANCH_WIDTH = 256
N_BRANCH = 4
S5_GROUP = 16
S5_GROUPS = BRANCH_WIDTH // S5_GROUP
S5_STATE = 64
FN_GROUPS = 4
FN_GROUP = BRANCH_WIDTH // FN_GROUPS
GDN_HEADS = 4
GDN_HEAD_DIM = BRANCH_WIDTH // GDN_HEADS
GDN_CHUNK = 64
GDN_CONV = 3
SG_GROUPS = 4
SG_GROUP = BRANCH_WIDTH // SG_GROUPS
SG_CHUNK = 128
D_FF = -(-8 * D_MODEL // (3 * 256)) * 256

IN_SIZES = (BRANCH_WIDTH, BRANCH_WIDTH, BRANCH_WIDTH, 2 * GDN_HEADS, 2 * GDN_HEADS,
            BRANCH_WIDTH, BRANCH_WIDTH, BRANCH_WIDTH, BRANCH_WIDTH, BRANCH_WIDTH,
            N_BRANCH * D_MODEL)
N_STATE_BLOCKS = 5
N_STATE_COLS = 3 * BRANCH_WIDTH + 4 * GDN_HEADS
D_IN = 8 * BRANCH_WIDTH + 4 * GDN_HEADS + N_BRANCH * D_MODEL

kernel_name = 'hybrid_s5_fnet_gdn_sgu_diffusion_trunk'


def _ident(t):
    return t


def _rev(t):
    return None if t is None else t[:, ::-1]


def _split(p, sizes):
    return jnp.split(p, np.cumsum(sizes)[:-1].tolist(), axis=-1)


def _rmsnorm(x, g):
    xf = x.astype(F32)
    y = xf * lax.rsqrt(jnp.mean(xf * xf, axis=-1, keepdims=True) + EPS)
    return (y * g.astype(F32)).astype(x.dtype)


def _l2norm(t):
    return t * lax.rsqrt(jnp.sum(t * t, axis=-1, keepdims=True) + EPS)


def _cplx(re, im):
    return lax.complex(re.astype(F32), im.astype(F32))


def _short_conv(x, w):
    ch, kw = w.shape
    return lax.conv_general_dilated(
        x, w.T[:, None, :].astype(x.dtype), window_strides=(1,),
        padding=[(kw // 2, kw // 2)], dimension_numbers=('NWC', 'WIO', 'NWC'),
        feature_group_count=ch)


def _lin_combine(e1, e2):
    a1, b1 = e1
    a2, b2 = e2
    return a1 * a2, a2 * b1 + b2


def _s5_scan(u, a_bar, b_bar, s0):
    bu = jnp.einsum('gpc,btgc->btgp', b_bar, u.astype(jnp.complex64))
    if s0 is not None:
        bu = bu.at[:, 0].add(a_bar * s0)
    _, s = lax.associative_scan(_lin_combine, (jnp.broadcast_to(a_bar, bu.shape), bu), axis=1)
    return s


def _s5_branch(u_l, u_c, lp, ctx_out):
    B = u_l.shape[0]
    ul = u_l.astype(F32).reshape(B, -1, S5_GROUPS, S5_GROUP)
    uc = u_c.astype(F32).reshape(B, -1, S5_GROUPS, S5_GROUP)
    s_lat, s_ctx = [], []
    for d in range(2):
        f = _rev if d else _ident
        lam = _cplx(lp['s5_lam_re'][d], lp['s5_lam_im'][d])
        a_bar = jnp.exp(lam * jnp.exp(lp['s5_log_dt'][d].astype(F32))[:, None])
        b_bar = ((a_bar - 1.0) / lam)[..., None] * _cplx(lp['s5_b_re'][d], lp['s5_b_im'][d])
        sc = _s5_scan(f(uc), a_bar, b_bar, None)
        sl = _s5_scan(f(ul), a_bar, b_bar, sc[:, -1])
        s_lat.append(f(sl))
        s_ctx.append(sc)
    c_f = _cplx(lp['s5_c_re'][0], lp['s5_c_im'][0])
    c_b = _cplx(lp['s5_c_re'][1], lp['s5_c_im'][1])
    d_skip = lp['s5_d'].astype(F32).reshape(S5_GROUPS, S5_GROUP)

    def readout(u, s_f, s_b):
        y = (jnp.einsum('gcp,btgp->btgc', c_f, s_f) + jnp.einsum('gcp,btgp->btgc', c_b, s_b)).real
        y = jax.nn.gelu((y + d_skip * u).reshape(B, -1, BRANCH_WIDTH))
        return y * jax.nn.sigmoid(y @ lp['s5_glu_w'].astype(F32) + lp['s5_glu_b'].astype(F32))

    y_l = readout(ul, s_lat[0], s_lat[1])
    y_c = readout(uc, s_ctx[0], _rev(s_ctx[1])) if ctx_out else None
    return y_l, y_c


def _fnet(u, w):
    B, L, _ = u.shape
    ug = u.astype(F32).reshape(B, L, FN_GROUPS, FN_GROUP).transpose(0, 2, 1, 3)
    f = jnp.fft.fft2(ug, norm='ortho').real
    return jnp.einsum('bglc,gce->blge', f, w.astype(F32)).reshape(B, L, BRANCH_WIDTH)


def _gdn_chunk(q, k, v, g, beta, s0, with_out):
    B, L, H, K = k.shape
    V = v.shape[-1]
    C = GDN_CHUNK
    N = L // C

    def blk(t):
        return jnp.moveaxis(t.reshape((B, N, C) + t.shape[2:]), 2, 3)

    k, v, beta = blk(k), blk(v), blk(beta)
    g = jnp.cumsum(blk(g), axis=-1)
    incl = jnp.tril(jnp.ones((C, C), dtype=bool))
    strict = jnp.tril(jnp.ones((C, C), dtype=bool), -1)
    rel = jnp.exp(jnp.where(incl, g[..., :, None] - g[..., None, :], -jnp.inf))
    kb = k * beta[..., None]
    a = jnp.where(strict, jnp.einsum('bnhik,bnhjk->bnhij', kb, k) * rel, 0.0)
    rhs = jnp.concatenate([v * beta[..., None], kb * jnp.exp(g)[..., None]], axis=-1)
    sol = lax.linalg.triangular_solve(a + jnp.eye(C, dtype=F32), rhs, left_side=True,
                                      lower=True, unit_diagonal=True)
    w_v, w_k = sol[..., :V], sol[..., V:]
    k_dec = k * jnp.exp(g[..., -1:] - g)[..., None]
    g_tot = jnp.exp(g[..., -1])

    def to_scan(t):
        return jnp.moveaxis(t, 1, 0)

    if with_out:
        q = blk(q)
        qk = jnp.where(incl, jnp.einsum('bnhik,bnhjk->bnhij', q, k) * rel, 0.0)
        q_dec = q * jnp.exp(g)[..., None]

        def step(s, xs):
            wv, wk, kd, gt, qkm, qd = xs
            u = wv - jnp.einsum('bhck,bhkv->bhcv', wk, s)
            o = jnp.einsum('bhck,bhkv->bhcv', qd, s) + jnp.einsum('bhij,bhjv->bhiv', qkm, u)
            return s * gt[..., None, None] + jnp.einsum('bhck,bhcv->bhkv', kd, u), o

        xs = (w_v, w_k, k_dec, g_tot, qk, q_dec)
        s_fin, o = lax.scan(step, s0, tuple(to_scan(t) for t in xs))
        o = jnp.moveaxis(jnp.moveaxis(o, 0, 1), 3, 2).reshape(B, L, H, V)
        return o, s_fin

    def step_state(s, xs):
        wv, wk, kd, gt = xs
        u = wv - jnp.einsum('bhck,bhkv->bhcv', wk, s)
        return s * gt[..., None, None] + jnp.einsum('bhck,bhcv->bhkv', kd, u), None

    xs = (w_v, w_k, k_dec, g_tot)
    s_fin, _ = lax.scan(step_state, s0, tuple(to_scan(t) for t in xs))
    return None, s_fin


def _gdn_prep(q, k, v, a, b, lp):
    conv = lp['gdn_conv']
    B, L, _ = k.shape

    def feats(t, w):
        return jax.nn.silu(_short_conv(t, w).astype(F32)).reshape(B, L, GDN_HEADS, GDN_HEAD_DIM)

    kk = _l2norm(feats(k, conv[0]))
    vv = feats(v, conv[1])
    qq = None if q is None else _l2norm(feats(q, conv[2])) * GDN_HEAD_DIM ** -0.5
    a = a.astype(F32).reshape(B, L, 2, GDN_HEADS)
    b = b.astype(F32).reshape(B, L, 2, GDN_HEADS)
    g = -jnp.exp(lp['gdn_a_log'].astype(F32)) * jax.nn.softplus(a + lp['gdn_dt_bias'].astype(F32))
    return qq, kk, vv, g, jax.nn.sigmoid(b)


def _gdn_out(o, z, gain):
    B, L = o.shape[:2]
    gate = jax.nn.silu(z.astype(F32)).reshape(B, L, GDN_HEADS, GDN_HEAD_DIM)
    return (_rmsnorm(o, gain) * gate).reshape(B, L, BRANCH_WIDTH)


def _gdn_branch(lat, ctx, z, zc, lp, ctx_out):
    ql, kl, vl, gl, bl = _gdn_prep(*lat, lp)
    qc, kc, vc, gc, bc = _gdn_prep(*ctx, lp)
    s0 = jnp.zeros((kl.shape[0], GDN_HEADS, GDN_HEAD_DIM, GDN_HEAD_DIM), F32)
    o_lat, o_ctx = [], []
    for d in range(2):
        f = _rev if d else _ident
        oc, sc = _gdn_chunk(f(qc), f(kc), f(vc), f(gc[:, :, d]), f(bc[:, :, d]), s0, ctx_out)
        ol, _ = _gdn_chunk(f(ql), f(kl), f(vl), f(gl[:, :, d]), f(bl[:, :, d]), sc, True)
        o_lat.append(f(ol))
        o_ctx.append(f(oc))
    y_l = _gdn_out(o_lat[0] + o_lat[1], z, lp['gdn_norm'])
    y_c = _gdn_out(o_ctx[0] + o_ctx[1], zc, lp['gdn_norm']) if ctx_out else None
    return y_l, y_c


def _sgu(u, v, lp, n_chunks):
    B, L, _ = u.shape
    shp = (B, n_chunks, SG_CHUNK, SG_GROUPS, SG_GROUP)
    u = jax.nn.gelu(u.astype(F32)).reshape(shp)
    v = jax.nn.gelu(v.astype(F32)).reshape(shp)
    mu = jnp.mean(v, axis=-1, keepdims=True)
    var = jnp.mean(jnp.square(v - mu), axis=-1, keepdims=True)
    v = ((v - mu) * lax.rsqrt(var + EPS) * lp['sg_ln_g'].astype(F32).reshape(SG_GROUPS, SG_GROUP)
         + lp['sg_ln_b'].astype(F32).reshape(SG_GROUPS, SG_GROUP))
    sv = jnp.einsum('gpq,bnqgc->bnpgc', lp['sg_w'].astype(F32), v) + lp['sg_b'].astype(F32).T[:, :, None]
    return (u * sv).reshape(B, L, BRANCH_WIDTH)


def _merge(ys, gate, w_branch, w_out):
    B, L, _ = gate.shape
    g = jax.nn.sigmoid(gate.astype(F32)).reshape(B, L, N_BRANCH, D_MODEL)
    acc = g[:, :, 0] * (ys[0] @ w_branch[0])
    for i in range(1, N_BRANCH):
        acc = acc + g[:, :, i] * (ys[i] @ w_branch[i])
    return acc @ w_out


def _mixer(h, hc, lp, rows, ctx_out):
    u5, k, v, a, b, q, z, ufn, usg, vsg, gate = _split(h @ lp['w_in'], IN_SIZES)
    if ctx_out:
        u5c, kc, vc, ac, bc, qc, zc, ufnc, usgc, vsgc, gatec = _split(hc @ lp['w_in'], IN_SIZES)
    else:
        u5c, kc, vc, ac, bc = _split(hc @ lp['w_in'][:, :N_STATE_COLS], IN_SIZES[:N_STATE_BLOCKS])
        qc = zc = None
    y5, y5c = _s5_branch(u5, u5c, lp, ctx_out)
    yg, ygc = _gdn_branch((q, k, v, a, b), (qc, kc, vc, ac, bc), z, zc, lp, ctx_out)
    yfn = _fnet(ufn, lp['fn_w'])
    ysg = _sgu(usg, vsg, lp, rows * GRID_W // SG_CHUNK)
    out = _merge((y5, yfn, yg, ysg), gate, lp['w_branch'], lp['w_out']).astype(h.dtype)
    if not ctx_out:
        return out, None
    yfnc = _fnet(ufnc, lp['fn_w'])
    ysgc = _sgu(usgc, vsgc, lp, hc.shape[1] // SG_CHUNK)
    outc = _merge((y5c, yfnc, ygc, ysgc), gatec, lp['w_branch'], lp['w_out']).astype(hc.dtype)
    return out, outc


def _swiglu(h, w1, w2):
    gt, up = jnp.split(h @ w1, 2, axis=-1)
    return (jax.nn.silu(gt) * up) @ w2


def setup_inputs(seed: int = 0) -> dict:
    key = jax.random.key(seed)
    ks = iter(jax.random.split(key, 48))

    def nrm(shape, scale):
        return jax.random.normal(next(ks), shape, F32) * scale

    Lr, D, W = DEPTH, D_MODEL, BRANCH_WIDTH
    G, P, H = S5_GROUPS, S5_STATE, GDN_HEADS
    n = jnp.arange(P, dtype=F32)
    gdn_dt = jnp.exp(jax.random.uniform(next(ks), (Lr, 2, H), F32, math.log(1e-3), math.log(1e-1)))
    return {
        'x': nrm((BATCH, SEQ, D), 1.0),
        'c': nrm((BATCH, D), 1.0),
        'ctx': nrm((BATCH, CTX_LEN, D), 1.0),
        'c_ctx': nrm((D,), 1.0),
        'ada_w': nrm((Lr, D, 6 * D), 0.5 * D ** -0.5),
        'ada_b': nrm((Lr, 6 * D), 0.02),
        'norm1': 1.0 + nrm((Lr, D), 0.02),
        'norm2': 1.0 + nrm((Lr, D), 0.02),
        'w_in': nrm((Lr, D, D_IN), D ** -0.5),
        's5_lam_re': -0.5 + nrm((Lr, 2, G, P), 0.01),
        's5_lam_im': jnp.pi * n + nrm((Lr, 2, G, P), 0.01),
        's5_log_dt': jax.random.uniform(next(ks), (Lr, 2, G), F32, math.log(1e-3), math.log(1e-1)),
        's5_b_re': nrm((Lr, 2, G, P, S5_GROUP), (2 * S5_GROUP) ** -0.5),
        's5_b_im': nrm((Lr, 2, G, P, S5_GROUP), (2 * S5_GROUP) ** -0.5),
        's5_c_re': nrm((Lr, 2, G, S5_GROUP, P), 0.5 ** 0.5),
        's5_c_im': nrm((Lr, 2, G, S5_GROUP, P), 0.5 ** 0.5),
        's5_d': nrm((Lr, W), 0.5),
        's5_glu_w': nrm((Lr, W, W), W ** -0.5),
        's5_glu_b': nrm((Lr, W), 0.02),
        'fn_w': nrm((Lr, FN_GROUPS, FN_GROUP, FN_GROUP), FN_GROUP ** -0.5),
        'gdn_conv': nrm((Lr, 3, W, GDN_CONV), GDN_CONV ** -0.5),
        'gdn_a_log': jnp.log(jax.random.uniform(next(ks), (Lr, 2, H), F32, 1.0, 16.0)),
        'gdn_dt_bias': gdn_dt + jnp.log(-jnp.expm1(-gdn_dt)),
        'gdn_norm': 1.0 + nrm((Lr, GDN_HEAD_DIM), 0.02),
        'sg_ln_g': 1.0 + nrm((Lr, W), 0.02),
        'sg_ln_b': nrm((Lr, W), 0.02),
        'sg_w': nrm((Lr, SG_GROUPS, SG_CHUNK, SG_CHUNK), SG_CHUNK ** -0.5),
        'sg_b': 1.0 + nrm((Lr, SG_GROUPS, SG_CHUNK), 0.02),
        'w_branch': nrm((Lr, N_BRANCH, W, D), W ** -0.5),
        'w_out': nrm((Lr, D, D), D ** -0.5),
        'ffn_w1': nrm((Lr, D, 2 * D_FF), D ** -0.5),
        'ffn_w2': nrm((Lr, D_FF, D), D_FF ** -0.5),
        'norm_f': 1.0 + nrm((D,), 0.02),
    }


def reference(x, c, ctx, c_ctx, ada_w, ada_b, norm1, norm2, w_in, s5_lam_re, s5_lam_im,
              s5_log_dt, s5_b_re, s5_b_im, s5_c_re, s5_c_im, s5_d, s5_glu_w, s5_glu_b, fn_w,
              gdn_conv, gdn_a_log, gdn_dt_bias, gdn_norm, sg_ln_g, sg_ln_b, sg_w, sg_b,
              w_branch, w_out, ffn_w1, ffn_w2, norm_f):
    rows = x.shape[1] // GRID_W
    xc = ctx
    silu_c = jax.nn.silu(c)[:, None, :]
    silu_cc = jax.nn.silu(c_ctx)
    for l in range(DEPTH):
        ctx_out = l < DEPTH - 1
        lp = {
            'w_in': w_in[l], 's5_lam_re': s5_lam_re[l], 's5_lam_im': s5_lam_im[l],
            's5_log_dt': s5_log_dt[l], 's5_b_re': s5_b_re[l], 's5_b_im': s5_b_im[l],
            's5_c_re': s5_c_re[l], 's5_c_im': s5_c_im[l], 's5_d': s5_d[l],
            's5_glu_w': s5_glu_w[l], 's5_glu_b': s5_glu_b[l], 'fn_w': fn_w[l],
            'gdn_conv': gdn_conv[l], 'gdn_a_log': gdn_a_log[l], 'gdn_dt_bias': gdn_dt_bias[l],
            'gdn_norm': gdn_norm[l], 'sg_ln_g': sg_ln_g[l], 'sg_ln_b': sg_ln_b[l],
            'sg_w': sg_w[l], 'sg_b': sg_b[l], 'w_branch': w_branch[l], 'w_out': w_out[l],
        }
        sh1, sc1, g1, sh2, sc2, g2 = jnp.split(silu_c @ ada_w[l] + ada_b[l], 6, axis=-1)
        csh1, csc1, cg1, csh2, csc2, cg2 = jnp.split(silu_cc @ ada_w[l] + ada_b[l], 6, axis=-1)
        h = _rmsnorm(x, norm1[l]) * (1.0 + sc1) + sh1
        hc = _rmsnorm(xc, norm1[l]) * (1.0 + csc1) + csh1
        y, yc = _mixer(h, hc, lp, rows, ctx_out)
        x = x + g1 * y
        x = x + g2 * _swiglu(_rmsnorm(x, norm2[l]) * (1.0 + sc2) + sh2, ffn_w1[l], ffn_w2[l])
        if ctx_out:
            xc = xc + cg1 * yc
            xc = xc + cg2 * _swiglu(_rmsnorm(xc, norm2[l]) * (1.0 + csc2) + csh2, ffn_w1[l], ffn_w2[l])
    return _rmsnorm(x, norm_f)
```

```python
import functools
import math

import numpy as np
import jax
import jax.numpy as jnp
from jax import lax
from jax.experimental import pallas as pl
from jax.experimental.pallas import tpu as pltpu

F32 = jnp.float32
BF16 = jnp.bfloat16
HIGHEST = lax.Precision.HIGHEST

EPS = 1e-6
W = 256
N_BRANCH = 4
S5_GROUP = 16
S5_GROUPS = W // S5_GROUP
S5_STATE = 64
N_S5 = S5_GROUPS * S5_STATE
FN_GROUPS = 4
FN_GROUP = W // FN_GROUPS
GDN_HEADS = 4
GDN_HEAD_DIM = W // GDN_HEADS
GDN_CHUNK = 64
SG_GROUPS = 4
SG_GROUP = W // SG_GROUPS
SG_CHUNK = 128
LANES = 128
T_TILE = 8
VMEM_LIMIT = 56 * 1024 * 1024


def _cparams(*sem):
    return pltpu.CompilerParams(dimension_semantics=sem, vmem_limit_bytes=VMEM_LIMIT)


def _resident(shape):
    nd = len(shape)
    return pl.BlockSpec(shape, lambda *_: (0,) * nd, pipeline_mode=pl.Buffered(1))


def _dot(a, b):
    return jnp.dot(a, b, preferred_element_type=F32)


def _dot_hi(a, b):
    return jnp.dot(a, b, preferred_element_type=F32, precision=HIGHEST)


def _gelu(x):
    return 0.5 * x * (1.0 + jnp.tanh(math.sqrt(2.0 / math.pi) * (x + 0.044715 * (x * x * x))))


def _sigmoid(x):
    return 1.0 / (1.0 + jnp.exp(-x))


def _silu(x):
    return x * _sigmoid(x)


def _modulated_norm(x, gain, scale, shift):
    y = x * lax.rsqrt(jnp.mean(x * x, axis=-1, keepdims=True) + EPS) * gain
    return y * (1.0 + scale) + shift


def _ada_kernel(c_ref, w_ref, b_ref, o_ref):
    c = c_ref[...]
    o_ref[0] = _dot_hi(_silu(c), w_ref[0]) + b_ref[0]


def _ada(cc, ada_w, ada_b):
    depth, d, n = ada_w.shape
    rows = cc.shape[0]
    nt = n // d
    return pl.pallas_call(
        _ada_kernel,
        grid=(depth, nt),
        in_specs=[pl.BlockSpec((rows, d), lambda l, j: (0, 0)),
                  pl.BlockSpec((1, d, d), lambda l, j: (l, 0, j)),
                  pl.BlockSpec((1, 1, d), lambda l, j: (l, 0, j))],
        out_specs=pl.BlockSpec((1, rows, d), lambda l, j: (l, 0, j)),
        out_shape=jax.ShapeDtypeStruct((depth, rows, n), F32),
        compiler_params=_cparams("parallel", "parallel"),
        name="ada",
    )(cc, ada_w, ada_b.reshape(depth, 1, n))


def _kin_kernel(x_ref, sh_ref, sc_ref, g_ref, w5, wkvq, wab, wz, wfn, wsg, wgate,
                o5, okvq, oab, oz, ofn, osg, ogate):
    tt, b, d = x_ref.shape
    h = _modulated_norm(x_ref[...], g_ref[...], sc_ref[...], sh_ref[...])
    hb = h.reshape(tt * b, d).astype(BF16)
    o5[...] = _dot(hb, w5[...])
    okvq[...] = _dot(hb, wkvq[...])
    oab[...] = _dot(hb, wab[...])
    oz[...] = _dot(hb, wz[...])
    ofn[...] = _dot(hb, wfn[...])
    osg[...] = _dot(hb, wsg[...])
    for j in range(N_BRANCH):
        cols = slice(j * d, (j + 1) * d)
        ogate[:, cols] = _sigmoid(_dot(hb, wgate[:, cols])).astype(BF16)


def _mod_spec(b, d, k, n_lat_tiles):
    return pl.BlockSpec((None, b, d), lambda i: (jnp.where(i < n_lat_tiles, 0, 1), 0, k))


def _kin(x3, mod, gain, ws, n_lat_tiles):
    tt = T_TILE
    ttot, b, d = x3.shape
    rows = tt * b
    nr = ttot * b
    widths = [w.shape[1] for w in ws]
    dts = [F32] * 6 + [BF16]
    return pl.pallas_call(
        _kin_kernel,
        grid=(ttot // tt,),
        in_specs=[pl.BlockSpec((tt, b, d), lambda i: (i, 0, 0)),
                  _mod_spec(b, d, 0, n_lat_tiles), _mod_spec(b, d, 1, n_lat_tiles),
                  _resident((1, d))] + [_resident(w.shape) for w in ws],
        out_specs=[pl.BlockSpec((rows, n), lambda i: (i, 0)) for n in widths],
        out_shape=[jax.ShapeDtypeStruct((nr, n), dt) for n, dt in zip(widths, dts)],
        compiler_params=_cparams("parallel"),
        name="kin",
    )(x3, mod, mod, gain, *ws)


def _s5_kernel(uf_ref, ub_ref, bmf, bmb, cmf, cmb, af_ref, ab_ref, yf_ref, yb_ref,
               st_ref, bu_ref, s_ref, *, b):
    i = pl.program_id(0)
    rows = uf_ref.shape[0]
    tt = rows // b
    nblk = N_S5 // LANES

    @pl.when(i == 0)
    def _():
        st_ref[...] = jnp.zeros_like(st_ref)

    dirs = ((uf_ref, bmf, cmf, af_ref, yf_ref, range(tt)),
            (ub_ref, bmb, cmb, ab_ref, yb_ref, range(tt - 1, -1, -1)))
    for dr, (u_ref, bm, cm, a_ref, y_ref, order) in enumerate(dirs):
        bu_ref[...] = _dot(u_ref[...].astype(BF16), bm[...])
        for c in range(nblk):
            re = slice(c * LANES, (c + 1) * LANES)
            im = slice(N_S5 + c * LANES, N_S5 + (c + 1) * LANES)
            a_re = a_ref[:, re]
            a_im = a_ref[:, im]
            s_re = st_ref[dr, :, re]
            s_im = st_ref[dr, :, im]
            for t in order:
                r = slice(t * b, (t + 1) * b)
                n_re = a_re * s_re - a_im * s_im + bu_ref[r, re]
                n_im = a_re * s_im + a_im * s_re + bu_ref[r, im]
                s_re, s_im = n_re, n_im
                s_ref[r, re] = s_re.astype(BF16)
                s_ref[r, im] = s_im.astype(BF16)
            st_ref[dr, :, re] = s_re
            st_ref[dr, :, im] = s_im
        y_ref[...] = _dot(s_ref[...], cm[...])


def _s5(u5, bmats, cmats, avecs, b, n_lat_tiles):
    nr = u5.shape[0]
    rows = T_TILE * b
    n_tiles = nr // rows
    fwd = lambda i: ((i + n_lat_tiles) % n_tiles, 0)
    bwd = lambda i: (n_tiles - 1 - i, 0)
    return pl.pallas_call(
        functools.partial(_s5_kernel, b=b),
        grid=(n_tiles,),
        in_specs=[pl.BlockSpec((rows, W), fwd), pl.BlockSpec((rows, W), bwd),
                  _resident(bmats[0].shape), _resident(bmats[1].shape),
                  _resident(cmats[0].shape), _resident(cmats[1].shape),
                  _resident(avecs[0].shape), _resident(avecs[1].shape)],
        out_specs=[pl.BlockSpec((rows, W), fwd), pl.BlockSpec((rows, W), bwd)],
        out_shape=[jax.ShapeDtypeStruct((nr, W), F32)] * 2,
        scratch_shapes=[pltpu.VMEM((2, b, 2 * N_S5), F32),
                        pltpu.VMEM((rows, 2 * N_S5), F32),
                        pltpu.VMEM((rows, 2 * N_S5), BF16)],
        compiler_params=_cparams("arbitrary"),
        name="s5",
    )(u5, u5, bmats[0], bmats[1], cmats[0], cmats[1], avecs[0], avecs[1])


def _fnet_kernel(a_ref, cos_ref, nsin_ref, y_ref):
    ac = a_ref[:, :W].astype(BF16)
    as_ = a_ref[:, W:].astype(BF16)
    y_ref[...] = _dot(cos_ref[...], ac) + _dot(nsin_ref[...], as_)


def _fnet(afn2, cos_l, nsin_l, cos_c, nsin_c, b, l_lat, l_ctx):
    ttot = l_lat + l_ctx
    lat = pl.pallas_call(
        _fnet_kernel,
        grid=(b,),
        in_specs=[pl.BlockSpec((l_lat, 2 * W), lambda j: (0, j)),
                  _resident(cos_l.shape), _resident(nsin_l.shape)],
        out_specs=pl.BlockSpec((l_lat, W), lambda j: (0, j)),
        out_shape=jax.ShapeDtypeStruct((ttot, b * W), F32),
        compiler_params=_cparams("parallel"),
        name="fnet_lat",
    )(afn2, cos_l, nsin_l)
    blk = l_lat // l_ctx
    return pl.pallas_call(
        lambda a_ref, c_ref, s_ref, prev_ref, y_ref: _fnet_kernel(a_ref, c_ref, s_ref, y_ref),
        grid=(b,),
        in_specs=[pl.BlockSpec((l_ctx, 2 * W), lambda j: (blk, j)),
                  _resident(cos_c.shape), _resident(nsin_c.shape),
                  pl.BlockSpec(memory_space=pl.ANY)],
        out_specs=pl.BlockSpec((l_ctx, W), lambda j: (blk, j)),
        out_shape=jax.ShapeDtypeStruct((ttot, b * W), F32),
        input_output_aliases={3: 0},
        compiler_params=_cparams("parallel"),
        name="fnet_ctx",
    )(afn2, cos_c, nsin_c, lat)


def _sgu_kernel(uv_ref, w_ref, bias_ref, lng_ref, lnb_ref, avg_ref, y_ref, *, nb):
    lane_group = lax.broadcasted_iota(jnp.int32, (SG_CHUNK, W), 1) // SG_GROUP
    for j in range(nb):
        u = _gelu(uv_ref[:, j * 2 * W:j * 2 * W + W])
        v = _gelu(uv_ref[:, j * 2 * W + W:(j + 1) * 2 * W])
        mu = _dot_hi(v, avg_ref[...])
        dv = v - mu
        var = _dot_hi(dv * dv, avg_ref[...])
        vn = (dv * lax.rsqrt(var + EPS) * lng_ref[...] + lnb_ref[...]).astype(BF16)
        sv = _dot(w_ref[0], vn)
        for g in range(1, SG_GROUPS):
            sv = jnp.where(lane_group == g, _dot(w_ref[g], vn), sv)
        y_ref[:, j * W:(j + 1) * W] = u * (sv + bias_ref[...])


def _sgu(sg2, w, bias, lng, lnb, avg, b):
    ttot = sg2.shape[0]
    nb = 4
    return pl.pallas_call(
        functools.partial(_sgu_kernel, nb=nb),
        grid=(ttot // SG_CHUNK, b // nb),
        in_specs=[pl.BlockSpec((SG_CHUNK, nb * 2 * W), lambda n, j: (n, j)),
                  _resident(w.shape), _resident(bias.shape), _resident(lng.shape),
                  _resident(lnb.shape), _resident(avg.shape)],
        out_specs=pl.BlockSpec((SG_CHUNK, nb * W), lambda n, j: (n, j)),
        out_shape=jax.ShapeDtypeStruct((ttot, b * W), F32),
        compiler_params=_cparams("parallel", "parallel"),
        name="sgu",
    )(sg2, w, bias, lng, lnb, avg)


def _gdn_prep_kernel(x_ref, xp_ref, xn_ref, ab_ref, cw_ref, ones_ref, alog_ref, dtb_ref,
                     o_ref, gb_ref, *, n_lat_tiles, n_tiles):
    i = pl.program_id(0)
    tt, b, n = x_ref.shape
    x = x_ref[...]
    has_prev = jnp.logical_and(i != 0, i != n_lat_tiles).astype(F32)
    has_next = jnp.logical_and(i != n_lat_tiles - 1, i != n_tiles - 1).astype(F32)
    xp = jnp.concatenate([xp_ref[...] * has_prev, x[:-1]], axis=0)
    xn = jnp.concatenate([x[1:], xn_ref[...] * has_next], axis=0)
    y = xp * cw_ref[0:1, :] + x * cw_ref[1:2, :] + xn * cw_ref[2:3, :]
    y = _silu(y).reshape(tt * b, n)
    k = y[:, :W]
    v = y[:, W:2 * W]
    q = y[:, 2 * W:]
    k = k * lax.rsqrt(_dot_hi(k * k, ones_ref[...]) + EPS)
    q = q * lax.rsqrt(_dot_hi(q * q, ones_ref[...]) + EPS) * GDN_HEAD_DIM ** -0.5
    o_ref[:, :W] = k
    o_ref[:, W:2 * W] = v
    o_ref[:, 2 * W:] = q
    ab = ab_ref[...].reshape(tt * b, LANES)
    z = ab + dtb_ref[...]
    softplus = jnp.maximum(z, 0.0) + jnp.log(1.0 + jnp.exp(-jnp.abs(z)))
    g = -jnp.exp(alog_ref[...]) * softplus
    lane = lax.broadcasted_iota(jnp.int32, ab.shape, 1)
    gb_ref[...] = jnp.where(lane < 2 * GDN_HEADS, g, _sigmoid(ab))


def _gdn_prep(kvq3, ab3, cw, ones_seg, alog, dtb, n_lat_tiles):
    tt = T_TILE
    ttot, b, n = kvq3.shape
    n_tiles = ttot // tt
    rows = tt * b
    return pl.pallas_call(
        functools.partial(_gdn_prep_kernel, n_lat_tiles=n_lat_tiles, n_tiles=n_tiles),
        grid=(n_tiles,),
        in_specs=[pl.BlockSpec((tt, b, n), lambda i: (i, 0, 0)),
                  pl.BlockSpec((1, b, n), lambda i: (jnp.maximum(i * tt - 1, 0), 0, 0)),
                  pl.BlockSpec((1, b, n), lambda i: (jnp.minimum(i * tt + tt, ttot - 1), 0, 0)),
                  pl.BlockSpec((tt, b, LANES), lambda i: (i, 0, 0)),
                  _resident(cw.shape), _resident(ones_seg.shape),
                  _resident(alog.shape), _resident(dtb.shape)],
        out_specs=[pl.BlockSpec((rows, n), lambda i: (i, 0)),
                   pl.BlockSpec((rows, LANES), lambda i: (i, 0))],
        out_shape=[jax.ShapeDtypeStruct((ttot * b, n), F32),
                   jax.ShapeDtypeStruct((ttot * b, LANES), F32)],
        compiler_params=_cparams("parallel"),
        name="gdn_prep",
    )(kvq3, kvq3, kvq3, ab3, cw, ones_seg, alog, dtb)


def _gdn_direction(kvq_ref, gb_ref, expand_ref, s_ref, o_ref, backward):
    c = GDN_CHUNK
    heads = GDN_HEADS
    row = lax.broadcasted_iota(jnp.int32, (c, W), 0)
    lane = lax.broadcasted_iota(jnp.int32, (c, W), 1)
    col = lane % c
    head = lane // GDN_HEAD_DIM
    r64 = lax.broadcasted_iota(jnp.int32, (c, c), 0)
    c64 = lax.broadcasted_iota(jnp.int32, (c, c), 1)
    if backward:
        incl = row <= col
        strict = row < col
        earlier = row < col
        incl64 = (r64 <= c64).astype(F32)
    else:
        incl = row >= col
        strict = row > col
        earlier = row > col
        incl64 = (r64 >= c64).astype(F32)
    hmask = [(head == h).astype(F32) for h in range(heads)]

    def stack_heads(t):
        return jnp.concatenate([t * hmask[h] for h in range(heads)], axis=0)

    k = kvq_ref[:, :W]
    v = kvq_ref[:, W:2 * W]
    q = kvq_ref[:, 2 * W:]
    ge = _dot_hi(gb_ref[...], expand_ref[...])
    g_l = ge[:, :W]
    beta_l = ge[:, W:]
    cs = _dot_hi(incl64, jnp.concatenate([g_l, g_l * earlier.astype(F32)], axis=1))
    gc = cs[:, :W]
    dif = cs[:, W:]
    g_tot = jnp.sum(g_l, axis=0, keepdims=True)
    kb = k * beta_l
    k_st = stack_heads(k).astype(BF16)
    kq = lax.dot_general(jnp.concatenate([kb, q], axis=0).astype(BF16), k_st,
                         (((1,), (1,)), ((), ())), preferred_element_type=F32)
    rel = jnp.where(incl, jnp.exp(jnp.where(incl, dif, 0.0)), 0.0)
    a = jnp.where(strict, kq[:c] * rel, 0.0)
    qk = kq[c:] * rel
    p = stack_heads(a)
    pr = lax.broadcasted_iota(jnp.int32, (heads * c, W), 0)
    pc = lax.broadcasted_iota(jnp.int32, (heads * c, W), 1)
    same = lambda n: (pr // n) == (pc // n)
    base = 8
    a0 = jnp.where(same(base), p, 0.0)
    t_inv = (pr == pc).astype(F32) - a0
    pw = a0
    for _ in range(int(math.log2(base)) - 1):
        pb = pw.astype(BF16)
        pw = _dot(pb, pb)
        t_inv = t_inv + _dot(t_inv.astype(BF16), pw.astype(BF16))
    n = base
    while n < c:
        z = jnp.where(jnp.logical_and(same(2 * n), jnp.logical_not(same(n))), p, 0.0)
        tb = t_inv.astype(BF16)
        t_inv = t_inv - _dot(_dot(tb, z.astype(BF16)).astype(BF16), tb)
        n *= 2
    e_gc = jnp.exp(gc)
    rhs = jnp.concatenate([stack_heads(v * beta_l), stack_heads(kb * e_gc)], axis=1)
    w_st = _dot(t_inv.astype(BF16), rhs.astype(BF16))
    w_all = w_st[:c]
    for h in range(1, heads):
        w_all = w_all + w_st[h * c:(h + 1) * c]
    w_v = w_all[:, :W]
    w_k = w_all[:, W:]
    s = s_ref[...]
    sb = s.astype(BF16)
    u = w_v - _dot(w_k.astype(BF16), sb)
    o_ref[...] = _dot((q * e_gc).astype(BF16), sb) + _dot(qk.astype(BF16), stack_heads(u).astype(BF16))
    k_dec = k * jnp.exp(g_tot - gc)
    upd = lax.dot_general(k_dec.astype(BF16), u.astype(BF16), (((0,), (0,)), ((), ())),
                          preferred_element_type=F32)
    bd = (lax.broadcasted_iota(jnp.int32, (W, W), 0) // GDN_HEAD_DIM
          == lax.broadcasted_iota(jnp.int32, (W, W), 1) // GDN_HEAD_DIM)
    s_ref[...] = s * jnp.exp(g_tot) + jnp.where(bd, upd, 0.0)


def _gdn_chunk_kernel(kvqf_ref, kvqb_ref, gbf_ref, gbb_ref, ef_ref, eb_ref, of_ref, ob_ref,
                      sf_ref, sb_ref):
    @pl.when(pl.program_id(1) == 0)
    def _():
        sf_ref[...] = jnp.zeros_like(sf_ref)
        sb_ref[...] = jnp.zeros_like(sb_ref)

    _gdn_direction(kvqf_ref, gbf_ref, ef_ref, sf_ref, of_ref, False)
    _gdn_direction(kvqb_ref, gbb_ref, eb_ref, sb_ref, ob_ref, True)


def _gdn_chunk(kvq2, gb2, expand_f, expand_b, b, l_lat, l_ctx):
    c = GDN_CHUNK
    ttot = l_lat + l_ctx
    n_lat = l_lat // c
    n_chunks = ttot // c
    fwd = lambda bi, i: ((i + n_lat) % n_chunks, bi)
    bwd = lambda bi, i: (n_chunks - 1 - i, bi)
    return pl.pallas_call(
        _gdn_chunk_kernel,
        grid=(b, n_chunks),
        in_specs=[pl.BlockSpec((c, 3 * W), fwd), pl.BlockSpec((c, 3 * W), bwd),
                  pl.BlockSpec((c, LANES), fwd), pl.BlockSpec((c, LANES), bwd),
                  _resident(expand_f.shape), _resident(expand_b.shape)],
        out_specs=[pl.BlockSpec((c, W), fwd), pl.BlockSpec((c, W), bwd)],
        out_shape=[jax.ShapeDtypeStruct((ttot, b * W), F32)] * 2,
        scratch_shapes=[pltpu.VMEM((W, W), F32), pltpu.VMEM((W, W), F32)],
        compiler_params=_cparams("parallel", "arbitrary"),
        name="gdn_chunk",
    )(kvq2, kvq2, gb2, gb2, expand_f, expand_b)


def _merge_kernel(x_ref, g1_ref, yf_ref, yb_ref, u5_ref, dskip_ref, gluw_ref, glub_ref,
                  of_ref, ob_ref, z_ref, gain_ref, avg_ref, yfn_ref, ysg_ref, gate_ref,
                  wbr_ref, wout_ref, o_ref):
    tt, b, d = x_ref.shape
    y5 = _gelu(yf_ref[...] + yb_ref[...] + dskip_ref[...] * u5_ref[...])
    y5 = y5 * _sigmoid(_dot(y5.astype(BF16), gluw_ref[...]) + glub_ref[...])
    o = of_ref[...] + ob_ref[...]
    yg = o * lax.rsqrt(_dot_hi(o * o, avg_ref[...]) + EPS) * gain_ref[...] * _silu(z_ref[...])
    ys = (y5, yfn_ref[...], yg, ysg_ref[...])
    acc = None
    for j in range(N_BRANCH):
        t = gate_ref[:, j * d:(j + 1) * d].astype(F32) * _dot(ys[j].astype(BF16), wbr_ref[j])
        acc = t if acc is None else acc + t
    out = _dot(acc.astype(BF16), wout_ref[...])
    o_ref[...] = x_ref[...] + g1_ref[...] * out.reshape(tt, b, d)


def _merge(x3, mod, acts, consts, n_lat_tiles):
    tt = T_TILE
    ttot, b, d = x3.shape
    rows = tt * b
    yf, yb, u5, of, ob, z, yfn, ysg, gate = acts
    dskip, gluw, glub, gain, avg, wbr, wout = consts
    row_spec = lambda n: pl.BlockSpec((rows, n), lambda i: (i, 0))
    return pl.pallas_call(
        _merge_kernel,
        grid=(ttot // tt,),
        in_specs=[pl.BlockSpec((tt, b, d), lambda i: (i, 0, 0)), _mod_spec(b, d, 2, n_lat_tiles),
                  row_spec(W), row_spec(W), row_spec(W),
                  _resident(dskip.shape), _resident(gluw.shape), _resident(glub.shape),
                  row_spec(W), row_spec(W), row_spec(W),
                  _resident(gain.shape), _resident(avg.shape),
                  row_spec(W), row_spec(W), row_spec(N_BRANCH * d),
                  _resident(wbr.shape), _resident(wout.shape)],
        out_specs=pl.BlockSpec((tt, b, d), lambda i: (i, 0, 0)),
        out_shape=jax.ShapeDtypeStruct(x3.shape, F32),
        compiler_params=_cparams("parallel"),
        name="merge",
    )(x3, mod, yf, yb, u5, dskip, gluw, glub, of, ob, z, gain, avg, yfn, ysg, gate, wbr, wout)


def _ffn_kernel(x_ref, sh_ref, sc_ref, g2_ref, gain_ref, w1_ref, w2_ref, nf_ref, o_ref, *, final):
    tt, b, d = x_ref.shape
    dff = w2_ref.shape[0]
    x = x_ref[...]
    h = _modulated_norm(x, gain_ref[...], sc_ref[...], sh_ref[...])
    t = _dot(h.reshape(tt * b, d).astype(BF16), w1_ref[...])
    act = (_silu(t[:, :dff]) * t[:, dff:]).astype(BF16)
    y = x + g2_ref[...] * _dot(act, w2_ref[...]).reshape(tt, b, d)
    if final:
        y = y * lax.rsqrt(jnp.mean(y * y, axis=-1, keepdims=True) + EPS) * nf_ref[...]
    o_ref[...] = y


def _ffn(x3, mod, gain, w1, w2, norm_f, n_lat_tiles, final):
    tt = T_TILE
    ttot, b, d = x3.shape
    return pl.pallas_call(
        functools.partial(_ffn_kernel, final=final),
        grid=(ttot // tt,),
        in_specs=[pl.BlockSpec((tt, b, d), lambda i: (i, 0, 0)),
                  _mod_spec(b, d, 3, n_lat_tiles), _mod_spec(b, d, 4, n_lat_tiles),
                  _mod_spec(b, d, 5, n_lat_tiles),
                  _resident(gain.shape), _resident(w1.shape), _resident(w2.shape),
                  _resident(norm_f.shape)],
        out_specs=pl.BlockSpec((tt, b, d), lambda i: (i, 0, 0)),
        out_shape=jax.ShapeDtypeStruct(x3.shape, F32),
        compiler_params=_cparams("parallel"),
        name="ffn",
    )(x3, mod, mod, mod, gain, w1, w2, norm_f)


def _dft_tables(n):
    idx = np.arange(n, dtype=np.int64)
    ang = 2.0 * np.pi * ((idx[:, None] * idx[None, :]) % n).astype(np.float64) / n
    return np.cos(ang), np.sin(ang)


def _s5_tables(lam_re, lam_im, log_dt, b_re, b_im, c_re, c_im):
    g, p, cg = S5_GROUPS, S5_STATE, S5_GROUP
    lam = lax.complex(lam_re.astype(F32), lam_im.astype(F32))
    a_bar = jnp.exp(lam * jnp.exp(log_dt.astype(F32))[:, None])
    b_bar = ((a_bar - 1.0) / lam)[..., None] * lax.complex(b_re.astype(F32), b_im.astype(F32))
    eye = jnp.eye(g, dtype=F32)
    bm_re = jnp.einsum('gpc,gh->gchp', b_bar.real, eye).reshape(g * cg, g * p)
    bm_im = jnp.einsum('gpc,gh->gchp', b_bar.imag, eye).reshape(g * cg, g * p)
    bmat = jnp.concatenate([bm_re, bm_im], axis=1).astype(BF16)
    cm_re = jnp.einsum('gcp,gh->gphc', c_re.astype(F32), eye).reshape(g * p, g * cg)
    cm_im = jnp.einsum('gcp,gh->gphc', c_im.astype(F32), eye).reshape(g * p, g * cg)
    cmat = jnp.concatenate([cm_re, -cm_im], axis=0).astype(BF16)
    avec = jnp.concatenate([a_bar.real.reshape(1, -1), a_bar.imag.reshape(1, -1)], axis=1)
    return bmat, cmat, avec


def _layer_consts(l, p, l_lat, l_ctx):
    d = p['w_in'].shape[1]
    w_in = p['w_in'][l]
    sizes = (W, W, W, 2 * GDN_HEADS, 2 * GDN_HEADS, W, W, W, W, W, N_BRANCH * d)
    offs = np.concatenate([[0], np.cumsum(sizes)])
    col = lambda j: w_in[:, offs[j]:offs[j + 1]]
    u5, k, v, a, bt, q, z, ufn, usg, vsg, gate = [col(j) for j in range(len(sizes))]
    cc, sc = _dft_tables(FN_GROUP)
    fw = p['fn_w'][l].astype(F32)
    wc = jnp.einsum('cd,gde->gce', jnp.asarray(cc, F32), fw, precision=HIGHEST)
    ws = jnp.einsum('cd,gde->gce', jnp.asarray(sc, F32), fw, precision=HIGHEST)
    ufn_g = ufn.reshape(d, FN_GROUPS, FN_GROUP)
    fold_c = jnp.einsum('kgc,gce->kge', ufn_g, wc, precision=HIGHEST).reshape(d, W)
    fold_s = jnp.einsum('kgc,gce->kge', ufn_g, ws, precision=HIGHEST).reshape(d, W)
    w_ab = jnp.concatenate([a, bt, jnp.zeros((d, LANES - 4 * GDN_HEADS), F32)], axis=1)
    ws_in = [u5, jnp.concatenate([k, v, q], axis=1), w_ab, z,
             jnp.concatenate([fold_c, fold_s], axis=1), jnp.concatenate([usg, vsg], axis=1), gate]
    ws_in = [w.astype(BF16) for w in ws_in]

    s5 = [_s5_tables(p['s5_lam_re'][l, dr], p['s5_lam_im'][l, dr], p['s5_log_dt'][l, dr],
                     p['s5_b_re'][l, dr], p['s5_b_im'][l, dr], p['s5_c_re'][l, dr],
                     p['s5_c_im'][l, dr]) for dr in range(2)]

    conv = p['gdn_conv'][l].astype(F32)
    cw = jnp.transpose(conv, (2, 0, 1)).reshape(conv.shape[2], 3 * W)
    pad = jnp.zeros((1, LANES - 2 * GDN_HEADS), F32)
    alog = jnp.concatenate([p['gdn_a_log'][l].astype(F32).reshape(1, -1), pad], axis=1)
    dtb = jnp.concatenate([p['gdn_dt_bias'][l].astype(F32).reshape(1, -1), pad], axis=1)

    lane_head = np.arange(W) // GDN_HEAD_DIM
    seg = (lane_head[:, None] == lane_head[None, :]).astype(np.float32)
    expand = []
    for dr in range(2):
        e = np.zeros((LANES, 2 * W), np.float32)
        for h in range(GDN_HEADS):
            e[dr * GDN_HEADS + h, :W] = lane_head == h
            e[2 * GDN_HEADS + dr * GDN_HEADS + h, W:] = lane_head == h
        expand.append(jnp.asarray(e))

    sg_bias = jnp.repeat(p['sg_b'][l].astype(F32).T, SG_GROUP, axis=1)
    merge_consts = (p['s5_d'][l].astype(F32).reshape(1, W), p['s5_glu_w'][l].astype(BF16),
                    p['s5_glu_b'][l].astype(F32).reshape(1, W),
                    jnp.tile(p['gdn_norm'][l].astype(F32), GDN_HEADS).reshape(1, W),
                    jnp.asarray(seg / GDN_HEAD_DIM), p['w_branch'][l].astype(BF16),
                    p['w_out'][l].astype(BF16))
    return dict(
        ws_in=ws_in, s5=s5, cw=cw, alog=alog, dtb=dtb, seg_ones=jnp.asarray(seg), expand=expand,
        sg_w=p['sg_w'][l].astype(BF16), sg_bias=sg_bias,
        sg_lng=p['sg_ln_g'][l].astype(F32).reshape(1, W), sg_lnb=p['sg_ln_b'][l].astype(F32).reshape(1, W),
        sg_avg=jnp.asarray(seg / SG_GROUP), merge=merge_consts,
        w1=p['ffn_w1'][l].astype(BF16), w2=p['ffn_w2'][l].astype(BF16),
        norm1=p['norm1'][l].astype(F32).reshape(1, d), norm2=p['norm2'][l].astype(F32).reshape(1, d))


def kernel(x, c, ctx, c_ctx, ada_w, ada_b, norm1, norm2, w_in, s5_lam_re, s5_lam_im, s5_log_dt, s5_b_re, s5_b_im, s5_c_re, s5_c_im, s5_d, s5_glu_w, s5_glu_b, fn_w, gdn_conv, gdn_a_log, gdn_dt_bias, gdn_norm, sg_ln_g, sg_ln_b, sg_w, sg_b, w_branch, w_out, ffn_w1, ffn_w2, norm_f):
    p = dict(w_in=w_in, s5_lam_re=s5_lam_re, s5_lam_im=s5_lam_im, s5_log_dt=s5_log_dt,
             s5_b_re=s5_b_re, s5_b_im=s5_b_im, s5_c_re=s5_c_re, s5_c_im=s5_c_im, s5_d=s5_d,
             s5_glu_w=s5_glu_w, s5_glu_b=s5_glu_b, fn_w=fn_w, gdn_conv=gdn_conv,
             gdn_a_log=gdn_a_log, gdn_dt_bias=gdn_dt_bias, gdn_norm=gdn_norm, sg_ln_g=sg_ln_g,
             sg_ln_b=sg_ln_b, sg_w=sg_w, sg_b=sg_b, w_branch=w_branch, w_out=w_out,
             ffn_w1=ffn_w1, ffn_w2=ffn_w2, norm1=norm1, norm2=norm2)
    b, l_lat, d = x.shape
    l_ctx = ctx.shape[1]
    depth = ada_w.shape[0]
    ttot = l_lat + l_ctx
    assert b % 8 == 0 and l_lat % l_ctx == 0 and l_ctx % SG_CHUNK == 0 and d == N_BRANCH * W
    n_lat_tiles = l_lat // T_TILE

    x3 = jnp.concatenate([jnp.transpose(x, (1, 0, 2)), jnp.transpose(ctx, (1, 0, 2))], axis=0).astype(F32)

    ada_rows = -(-(b + 1) // 8) * 8
    cc = jnp.zeros((ada_rows, d), F32).at[:b].set(c.astype(F32)).at[b].set(c_ctx.astype(F32))
    mods = _ada(cc, ada_w.astype(F32), ada_b.astype(F32))
    mod_all = jnp.stack([mods[:, :b], jnp.broadcast_to(mods[:, b:b + 1], (depth, b, 6 * d))], axis=1)

    cos_l, sin_l = _dft_tables(l_lat)
    cos_c, sin_c = _dft_tables(l_ctx)
    sc_l = 1.0 / math.sqrt(l_lat * FN_GROUP)
    sc_c = 1.0 / math.sqrt(l_ctx * FN_GROUP)
    cos_l, nsin_l = jnp.asarray(cos_l * sc_l, BF16), jnp.asarray(-sin_l * sc_l, BF16)
    cos_c, nsin_c = jnp.asarray(cos_c * sc_c, BF16), jnp.asarray(-sin_c * sc_c, BF16)
    nf = norm_f.astype(F32).reshape(1, d)

    for l in range(depth):
        k = _layer_consts(l, p, l_lat, l_ctx)
        mod = mod_all[l]
        u5, kvq, ab, z, afn, sg, gate = _kin(x3, mod, k['norm1'], k['ws_in'], n_lat_tiles)
        yf, yb = _s5(u5, [t[0] for t in k['s5']], [t[1] for t in k['s5']], [t[2] for t in k['s5']],
                     b, n_lat_tiles)
        yfn = _fnet(afn.reshape(ttot, b * 2 * W), cos_l, nsin_l, cos_c, nsin_c, b, l_lat, l_ctx)
        ysg = _sgu(sg.reshape(ttot, b * 2 * W), k['sg_w'], k['sg_bias'], k['sg_lng'], k['sg_lnb'],
                   k['sg_avg'], b)
        kvq_p, gb = _gdn_prep(kvq.reshape(ttot, b, 3 * W), ab.reshape(ttot, b, LANES), k['cw'],
                              k['seg_ones'], k['alog'], k['dtb'], n_lat_tiles)
        of, ob = _gdn_chunk(kvq_p.reshape(ttot, b * 3 * W), gb.reshape(ttot, b * LANES),
                            k['expand'][0], k['expand'][1], b, l_lat, l_ctx)
        acts = (yf, yb, u5, of.reshape(ttot * b, W), ob.reshape(ttot * b, W), z,
                yfn.reshape(ttot * b, W), ysg.reshape(ttot * b, W), gate)
        x3 = _merge(x3, mod, acts, k['merge'], n_lat_tiles)
        x3 = _ffn(x3, mod, k['norm2'], k['w1'], k['w2'], nf, n_lat_tiles, final=(l == depth - 1))

    return jnp.transpose(x3[:l_lat], (1, 0, 2)).astype(x.dtype)
```

```python
import functools
import math

import numpy as np
import jax
import jax.numpy as jnp
from jax import lax
from jax.experimental import pallas as pl
from jax.experimental.pallas import tpu as pltpu

F32 = jnp.float32
BF16 = jnp.bfloat16
HIGHEST = lax.Precision.HIGHEST

EPS = 1e-6
W = 256
N_BRANCH = 4
S5_GROUP = 16
S5_GROUPS = W // S5_GROUP
S5_STATE = 64
N_S5 = S5_GROUPS * S5_STATE
FN_GROUPS = 4
FN_GROUP = W // FN_GROUPS
GDN_HEADS = 4
GDN_HEAD_DIM = W // GDN_HEADS
GDN_CHUNK = 64
GDN_BASE = 8
SG_GROUPS = 4
SG_GROUP = W // SG_GROUPS
SG_CHUNK = 128
LANES = 128
SUBLANES = 8
ROW_TILE = 256
S5_STEPS = 32
VMEM_LIMIT = 56 * 1024 * 1024


def _cparams(*sem):
    return pltpu.CompilerParams(dimension_semantics=sem, vmem_limit_bytes=VMEM_LIMIT)


def _resident(shape):
    nd = len(shape)
    return pl.BlockSpec(shape, lambda *_: (0,) * nd, pipeline_mode=pl.Buffered(1))


def _dot(a, b):
    return jnp.dot(a, b, preferred_element_type=F32)


def _split(x, parts):
    out = []
    for _ in range(parts - 1):
        hi = x.astype(BF16)
        out.append(hi)
        x = x - hi.astype(F32)
    out.append(x.astype(BF16))
    return out


def _dot_sel(x, sel, parts):
    acc = None
    for piece in _split(x, parts):
        t = _dot(piece, sel)
        acc = t if acc is None else acc + t
    return acc


def _gelu(x):
    return 0.5 * x * (1.0 + jnp.tanh(math.sqrt(2.0 / math.pi) * (x + 0.044715 * (x * x * x))))


def _sigmoid(x):
    return 1.0 / (1.0 + jnp.exp(-x))


def _silu(x):
    return x * _sigmoid(x)


def _modulated_norm(x, gain, scale, shift):
    y = x * lax.rsqrt(jnp.mean(x * x, axis=-1, keepdims=True) + EPS) * gain
    return y * (1.0 + scale) + shift


def _ada_kernel(c_ref, w_ref, b_ref, o_ref):
    c = c_ref[...]
    o_ref[0] = jnp.dot(_silu(c), w_ref[0], preferred_element_type=F32, precision=HIGHEST) + b_ref[0]


def _ada(cc, ada_w, ada_b):
    depth, d, n = ada_w.shape
    rows = cc.shape[0]
    return pl.pallas_call(
        _ada_kernel,
        grid=(depth, n // d),
        in_specs=[pl.BlockSpec((rows, d), lambda l, j: (0, 0)),
                  pl.BlockSpec((1, d, d), lambda l, j: (l, 0, j)),
                  pl.BlockSpec((1, 1, d), lambda l, j: (l, 0, j))],
        out_specs=pl.BlockSpec((1, rows, d), lambda l, j: (l, 0, j)),
        out_shape=jax.ShapeDtypeStruct((depth, rows, n), F32),
        compiler_params=_cparams("parallel", "parallel"),
        name="ada",
    )(cc, ada_w, ada_b.reshape(depth, 1, n))


def _tile_spec(n):
    return pl.BlockSpec((None, ROW_TILE, n), lambda b, j: (b, j, 0))


def _mod_spec(nb, d, k, n_lat_tiles):
    return pl.BlockSpec((None, 1, d), lambda b, j: (jnp.where(j < n_lat_tiles, b, nb), 0, k))


def _kin_kernel(x_ref, sh_ref, sc_ref, g_ref, w5, wkvq, wab, wz, wfn, wsg, wgate,
                o5, okvq, oab, oz, ofn, osg, ogate):
    d = x_ref.shape[1]
    hb = _modulated_norm(x_ref[...], g_ref[...], sc_ref[...], sh_ref[...]).astype(BF16)
    o5[...] = _dot(hb, w5[...])
    okvq[...] = _dot(hb, wkvq[...])
    oab[...] = _dot(hb, wab[...])
    oz[...] = _dot(hb, wz[...])
    ofn[...] = _dot(hb, wfn[...]).astype(BF16)
    osg[...] = _dot(hb, wsg[...])
    for j in range(N_BRANCH):
        cols = slice(j * d, (j + 1) * d)
        ogate[:, cols] = _sigmoid(_dot(hb, wgate[:, cols])).astype(BF16)


def _kin(x3, mod, gain, ws, n_lat_tiles):
    nb, ttot, d = x3.shape
    widths = [w.shape[1] for w in ws]
    dts = [F32, F32, F32, F32, BF16, F32, BF16]
    return pl.pallas_call(
        _kin_kernel,
        grid=(nb, ttot // ROW_TILE),
        in_specs=[_tile_spec(d), _mod_spec(nb, d, 0, n_lat_tiles), _mod_spec(nb, d, 1, n_lat_tiles),
                  _resident((1, d))] + [_resident(w.shape) for w in ws],
        out_specs=[_tile_spec(n) for n in widths],
        out_shape=[jax.ShapeDtypeStruct((nb, ttot, n), dt) for n, dt in zip(widths, dts)],
        compiler_params=_cparams("parallel", "parallel"),
        name="kin",
    )(x3, mod, mod, gain, *ws)


def _s5_kernel(uf_ref, ub_ref, perm_ref, permt_ref, bmf, bmb, cmf, cmb, af_ref, ab_ref,
               yf_ref, yb_ref, st_ref, bu_ref, s_ref):
    i = pl.program_id(0)
    nb, tt, _ = uf_ref.shape
    sub = SUBLANES
    nblk = N_S5 // LANES

    @pl.when(i == 0)
    def _():
        st_ref[...] = jnp.zeros_like(st_ref)

    dirs = ((uf_ref, bmf, cmf, af_ref, yf_ref, range(tt)),
            (ub_ref, bmb, cmb, ab_ref, yb_ref, range(tt - 1, -1, -1)))

    def project(dr):
        u_ref, bm = dirs[dr][0], dirs[dr][1]
        parts = []
        for k in range(tt // sub):
            blk = u_ref[:, k * sub:(k + 1) * sub, :].reshape(nb * sub, W).astype(BF16)
            parts.append(_dot(perm_ref[...], blk).astype(BF16))
        bu_ref[dr] = _dot(jnp.concatenate(parts, axis=0), bm[...])

    def recur(dr):
        a_ref, order = dirs[dr][3], dirs[dr][5]
        for c in range(nblk):
            re = slice(c * LANES, (c + 1) * LANES)
            im = slice(N_S5 + c * LANES, N_S5 + (c + 1) * LANES)
            a_re = a_ref[:, re]
            a_im = a_ref[:, im]
            s_re = st_ref[dr, :, re]
            s_im = st_ref[dr, :, im]
            for t in order:
                r = slice(t * nb, (t + 1) * nb)
                n_re = a_re * s_re - a_im * s_im + bu_ref[dr, r, re]
                n_im = a_re * s_im + a_im * s_re + bu_ref[dr, r, im]
                s_re, s_im = n_re, n_im
                s_ref[dr, r, re] = s_re.astype(BF16)
                s_ref[dr, r, im] = s_im.astype(BF16)
            st_ref[dr, :, re] = s_re
            st_ref[dr, :, im] = s_im

    def read_out(dr):
        cm, y_ref = dirs[dr][2], dirs[dr][4]
        y_tb = _dot(s_ref[dr], cm[...])
        for k in range(tt // sub):
            hi, lo = _split(y_tb[k * sub * nb:(k + 1) * sub * nb], 2)
            y_bt = _dot(permt_ref[...], hi) + _dot(permt_ref[...], lo)
            y_ref[:, k * sub:(k + 1) * sub, :] = y_bt.reshape(nb, sub, W)

    project(0)
    project(1)
    recur(0)
    read_out(0)
    recur(1)
    read_out(1)


def _s5(u5, perm, permt, bmats, cmats, avecs, n_lat_tiles):
    nb, ttot, _ = u5.shape
    tt = S5_STEPS
    n_tiles = ttot // tt
    fwd = lambda i: (0, (i + n_lat_tiles) % n_tiles, 0)
    bwd = lambda i: (0, n_tiles - 1 - i, 0)
    rows = tt * nb
    return pl.pallas_call(
        _s5_kernel,
        grid=(n_tiles,),
        in_specs=[pl.BlockSpec((nb, tt, W), fwd), pl.BlockSpec((nb, tt, W), bwd),
                  _resident(perm.shape), _resident(permt.shape),
                  _resident(bmats[0].shape), _resident(bmats[1].shape),
                  _resident(cmats[0].shape), _resident(cmats[1].shape),
                  _resident(avecs[0].shape), _resident(avecs[1].shape)],
        out_specs=[pl.BlockSpec((nb, tt, W), fwd), pl.BlockSpec((nb, tt, W), bwd)],
        out_shape=[jax.ShapeDtypeStruct((nb, ttot, W), F32)] * 2,
        scratch_shapes=[pltpu.VMEM((2, nb, 2 * N_S5), F32),
                        pltpu.VMEM((2, rows, 2 * N_S5), F32),
                        pltpu.VMEM((2, rows, 2 * N_S5), BF16)],
        compiler_params=_cparams("arbitrary"),
        name="s5",
    )(u5, u5, perm, permt, bmats[0], bmats[1], cmats[0], cmats[1], avecs[0], avecs[1])


def _fnet_kernel(a_ref, cos_ref, nsin_ref, y_ref):
    y_ref[...] = _dot(cos_ref[...], a_ref[:, :W]) + _dot(nsin_ref[...], a_ref[:, W:])


def _fnet_alias_kernel(a_ref, cos_ref, nsin_ref, prev_ref, y_ref):
    _fnet_kernel(a_ref, cos_ref, nsin_ref, y_ref)


def _fnet(afn, cos_l, nsin_l, cos_c, nsin_c, l_lat, l_ctx):
    nb, ttot, _ = afn.shape
    lat = pl.pallas_call(
        _fnet_kernel,
        grid=(nb,),
        in_specs=[pl.BlockSpec((None, l_lat, 2 * W), lambda b: (b, 0, 0)),
                  _resident(cos_l.shape), _resident(nsin_l.shape)],
        out_specs=pl.BlockSpec((None, l_lat, W), lambda b: (b, 0, 0)),
        out_shape=jax.ShapeDtypeStruct((nb, ttot, W), F32),
        compiler_params=_cparams("parallel"),
        name="fnet_lat",
    )(afn, cos_l, nsin_l)
    blk = l_lat // l_ctx
    return pl.pallas_call(
        _fnet_alias_kernel,
        grid=(nb,),
        in_specs=[pl.BlockSpec((None, l_ctx, 2 * W), lambda b: (b, blk, 0)),
                  _resident(cos_c.shape), _resident(nsin_c.shape),
                  pl.BlockSpec(memory_space=pl.ANY)],
        out_specs=pl.BlockSpec((None, l_ctx, W), lambda b: (b, blk, 0)),
        out_shape=jax.ShapeDtypeStruct((nb, ttot, W), F32),
        input_output_aliases={3: 0},
        compiler_params=_cparams("parallel"),
        name="fnet_ctx",
    )(afn, cos_c, nsin_c, lat)


def _sgu_kernel(uv_ref, w_ref, bias_ref, lng_ref, lnb_ref, avg_ref, y_ref):
    lane_group = lax.broadcasted_iota(jnp.int32, (SG_CHUNK, W), 1) // SG_GROUP
    for n in range(uv_ref.shape[0] // SG_CHUNK):
        rows = slice(n * SG_CHUNK, (n + 1) * SG_CHUNK)
        u = _gelu(uv_ref[rows, :W])
        v = _gelu(uv_ref[rows, W:])
        dv = v - _dot_sel(v, avg_ref[...], 2)
        var = _dot_sel(dv * dv, avg_ref[...], 2)
        vn = (dv * lax.rsqrt(var + EPS) * lng_ref[...] + lnb_ref[...]).astype(BF16)
        sv = _dot(w_ref[0], vn)
        for g in range(1, SG_GROUPS):
            sv = jnp.where(lane_group == g, _dot(w_ref[g], vn), sv)
        y_ref[rows, :] = u * (sv + bias_ref[...])


def _sgu(sg, w, bias, lng, lnb, avg, rows):
    nb, ttot, _ = sg.shape
    return pl.pallas_call(
        _sgu_kernel,
        grid=(nb, ttot // rows),
        in_specs=[pl.BlockSpec((None, rows, 2 * W), lambda b, n: (b, n, 0)),
                  _resident(w.shape), _resident(bias.shape), _resident(lng.shape),
                  _resident(lnb.shape), _resident(avg.shape)],
        out_specs=pl.BlockSpec((None, rows, W), lambda b, n: (b, n, 0)),
        out_shape=jax.ShapeDtypeStruct((nb, ttot, W), F32),
        compiler_params=_cparams("parallel", "parallel"),
        name="sgu",
    )(sg, w, bias, lng, lnb, avg)


def _gdn_prep_kernel(x_ref, xp_ref, xn_ref, ab_ref, cw_ref, ones_ref, alog_ref, dtb_ref,
                     o_ref, gb_ref, *, n_lat_tiles, n_tiles):
    j = pl.program_id(1)
    rows, n = x_ref.shape
    x = x_ref[...]
    has_prev = jnp.logical_and(j != 0, j != n_lat_tiles).astype(F32)
    has_next = jnp.logical_and(j != n_lat_tiles - 1, j != n_tiles - 1).astype(F32)
    row = lax.broadcasted_iota(jnp.int32, (rows, n), 0)
    xp = jnp.where(row == 0, xp_ref[SUBLANES - 1:SUBLANES, :] * has_prev, pltpu.roll(x, 1, 0))
    xn = jnp.where(row == rows - 1, xn_ref[0:1, :] * has_next, pltpu.roll(x, rows - 1, 0))
    y = _silu(xp * cw_ref[0:1, :] + x * cw_ref[1:2, :] + xn * cw_ref[2:3, :])
    k = y[:, :W]
    q = y[:, 2 * W:]
    o_ref[:, :W] = k * lax.rsqrt(_dot_sel(k * k, ones_ref[...], 2) + EPS)
    o_ref[:, W:2 * W] = y[:, W:2 * W]
    o_ref[:, 2 * W:] = q * lax.rsqrt(_dot_sel(q * q, ones_ref[...], 2) + EPS) * GDN_HEAD_DIM ** -0.5
    ab = ab_ref[...]
    z = ab + dtb_ref[...]
    softplus = jnp.maximum(z, 0.0) + jnp.log(1.0 + jnp.exp(-jnp.abs(z)))
    g = -jnp.exp(alog_ref[...]) * softplus
    lane = lax.broadcasted_iota(jnp.int32, ab.shape, 1)
    gb_ref[...] = jnp.where(lane < 2 * GDN_HEADS, g, _sigmoid(ab))


def _gdn_prep(kvq, ab, cw, ones_seg, alog, dtb, n_lat_tiles):
    nb, ttot, n = kvq.shape
    n_tiles = ttot // ROW_TILE
    per = ROW_TILE // SUBLANES
    last = ttot // SUBLANES - 1
    return pl.pallas_call(
        functools.partial(_gdn_prep_kernel, n_lat_tiles=n_lat_tiles, n_tiles=n_tiles),
        grid=(nb, n_tiles),
        in_specs=[_tile_spec(n),
                  pl.BlockSpec((None, SUBLANES, n), lambda b, j: (b, jnp.maximum(j * per - 1, 0), 0)),
                  pl.BlockSpec((None, SUBLANES, n), lambda b, j: (b, jnp.minimum((j + 1) * per, last), 0)),
                  _tile_spec(LANES),
                  _resident(cw.shape), _resident(ones_seg.shape),
                  _resident(alog.shape), _resident(dtb.shape)],
        out_specs=[_tile_spec(n), _tile_spec(LANES)],
        out_shape=[jax.ShapeDtypeStruct((nb, ttot, n), F32),
                   jax.ShapeDtypeStruct((nb, ttot, LANES), F32)],
        compiler_params=_cparams("parallel", "parallel"),
        name="gdn_prep",
    )(kvq, kvq, kvq, ab, cw, ones_seg, alog, dtb)


def _gdn_masks(backward):
    c, heads = GDN_CHUNK, GDN_HEADS
    i = np.arange(c)[:, None]
    j = (np.arange(W) % c)[None, :]
    incl = (i <= j) if backward else (i >= j)
    strict = (i < j) if backward else (i > j)
    j64 = np.arange(c)[None, :]
    incl64 = (i <= j64) if backward else (i >= j64)
    pr = np.arange(heads * c)[:, None]
    pc = np.arange(W)[None, :]
    same = lambda n: (pr // n) == (pc // n)
    levels = [same(GDN_BASE)]
    n = GDN_BASE
    while n < c:
        levels.append(same(2 * n) & ~same(n))
        n *= 2
    f = lambda m: jnp.asarray(m, F32)
    return (f(np.stack([incl, strict])), jnp.asarray(incl64, BF16), f(np.stack(levels)))


def _gdn_chains(chains, bd_ref, eye_ref):
    c, heads = GDN_CHUNK, GDN_HEADS
    bd = bd_ref[...]
    each = lambda f, *cols: [f(*args) for args in zip(*cols)]

    def stack_heads(t):
        reps = t.shape[1] // W
        mask = bd if reps == 1 else jnp.concatenate([bd] * reps, axis=1)
        return jnp.concatenate([t] * heads, axis=0) * mask

    kvq, gb, expand, m64, incl64, lvls, s_refs = zip(*chains)
    k = [t[:, :W] for t in kvq]
    v = [t[:, W:2 * W] for t in kvq]
    q = [t[:, 2 * W:] for t in kvq]
    ge = each(lambda g, e: _dot_sel(g, e[...], 3), gb, expand)
    g_l = [t[:, :W] for t in ge]
    beta_l = [t[:, W:] for t in ge]
    incl = [m[0] for m in m64]
    strict = [m[1] for m in m64]
    diag = [i - s_ for i, s_ in zip(incl, strict)]

    def cumulative(g, i64):
        acc = None
        for piece in _split(g, 3):
            t = _dot(i64[...], piece)
            acc = t if acc is None else acc + t
        return acc

    gc = each(cumulative, g_l, incl64)
    dif = each(lambda g, dg: g - jnp.sum(g * dg, axis=0, keepdims=True), gc, diag)
    g_tot = [jnp.sum(t, axis=0, keepdims=True) for t in g_l]
    kb = each(lambda a_, b_: a_ * b_, k, beta_l)
    k_st = [stack_heads(t).astype(BF16) for t in k]
    kq = each(lambda a_, b_, st: lax.dot_general(
        jnp.concatenate([a_, b_], axis=0).astype(BF16), st, (((1,), (1,)), ((), ())),
        preferred_element_type=F32), kb, q, k_st)
    rel = each(lambda d_, i: jnp.exp(d_ * i) * i, dif, incl)
    a = each(lambda t, r, st: t[:c] * r * st, kq, rel, strict)
    qk = each(lambda t, r: t[c:] * r, kq, rel)
    p = [stack_heads(t) for t in a]
    pw = each(lambda t, lv: t * lv[0], p, lvls)
    t_inv = [eye_ref[...] - t for t in pw]
    for _ in range(int(math.log2(GDN_BASE)) - 1):
        pb = [t.astype(BF16) for t in pw]
        pw = [_dot(t, t) for t in pb]
        t_inv = each(lambda t, w_: t + _dot(t.astype(BF16), w_.astype(BF16)), t_inv, pw)
    for lvl in range(1, lvls[0].shape[0]):
        z = each(lambda t, lv: (t * lv[lvl]).astype(BF16), p, lvls)
        tb = [t.astype(BF16) for t in t_inv]
        tz = each(lambda t, z_: _dot(t, z_).astype(BF16), tb, z)
        t_inv = each(lambda t, tz_, tb_: t - _dot(tz_, tb_), t_inv, tz, tb)
    e_gc = [jnp.exp(t) for t in gc]
    rhs = each(lambda v_, b_, kb_, e: stack_heads(jnp.concatenate([v_ * b_, kb_ * e], axis=1)).astype(BF16),
               v, beta_l, kb, e_gc)
    w_st = each(lambda t, r: _dot(t.astype(BF16), r), t_inv, rhs)

    def fold_heads(t):
        acc = t[:c]
        for h in range(1, heads):
            acc = acc + t[h * c:(h + 1) * c]
        return acc

    w_all = [fold_heads(t) for t in w_st]
    s = [r[...] for r in s_refs]
    sb = [t.astype(BF16) for t in s]
    u = each(lambda w_, sb_: w_[:, :W] - _dot(w_[:, W:].astype(BF16), sb_), w_all, sb)
    o = each(lambda q_, e, sb_, qk_, u_: _dot((q_ * e).astype(BF16), sb_)
             + _dot(qk_.astype(BF16), stack_heads(u_).astype(BF16)), q, e_gc, sb, qk, u)
    k_dec = each(lambda k_, gt, g: k_ * jnp.exp(gt - g), k, g_tot, gc)
    upd = each(lambda kd, u_: lax.dot_general(kd.astype(BF16), u_.astype(BF16), (((0,), (0,)), ((), ())),
                                              preferred_element_type=F32), k_dec, u)
    for r, s_, gt, up in zip(s_refs, s, g_tot, upd):
        r[...] = s_ * jnp.exp(gt) + up * bd
    return o


def _gdn_chunk_kernel(kvqf_ref, kvqb_ref, gbf_ref, gbb_ref, ef_ref, eb_ref, m64f_ref, m64b_ref,
                      i64f_ref, i64b_ref, lvlf_ref, lvlb_ref, bd_ref, eye_ref, of_ref, ob_ref, s_ref):
    @pl.when(pl.program_id(1) == 0)
    def _():
        s_ref[...] = jnp.zeros_like(s_ref)

    chains = []
    for bi in range(kvqf_ref.shape[0]):
        chains.append((kvqf_ref[bi], gbf_ref[bi], ef_ref, m64f_ref, i64f_ref, lvlf_ref, s_ref.at[bi, 0]))
        chains.append((kvqb_ref[bi], gbb_ref[bi], eb_ref, m64b_ref, i64b_ref, lvlb_ref, s_ref.at[bi, 1]))
    outs = _gdn_chains(chains, bd_ref, eye_ref)
    for bi in range(kvqf_ref.shape[0]):
        of_ref[bi] = outs[2 * bi]
        ob_ref[bi] = outs[2 * bi + 1]


def _gdn_chunk(kvq, gb, expand, masks, bd, eye, l_lat, per_step):
    c = GDN_CHUNK
    nb, ttot, _ = kvq.shape
    n_lat = l_lat // c
    n_chunks = ttot // c
    fwd = lambda b, i: (b, (i + n_lat) % n_chunks, 0)
    bwd = lambda b, i: (b, n_chunks - 1 - i, 0)
    consts = [expand[0], expand[1], masks[0][0], masks[1][0], masks[0][1], masks[1][1],
              masks[0][2], masks[1][2], bd, eye]
    return pl.pallas_call(
        _gdn_chunk_kernel,
        grid=(nb // per_step, n_chunks),
        in_specs=[pl.BlockSpec((per_step, c, 3 * W), fwd), pl.BlockSpec((per_step, c, 3 * W), bwd),
                  pl.BlockSpec((per_step, c, LANES), fwd), pl.BlockSpec((per_step, c, LANES), bwd)]
                 + [_resident(t.shape) for t in consts],
        out_specs=[pl.BlockSpec((per_step, c, W), fwd), pl.BlockSpec((per_step, c, W), bwd)],
        out_shape=[jax.ShapeDtypeStruct((nb, ttot, W), F32)] * 2,
        scratch_shapes=[pltpu.VMEM((per_step, 2, W, W), F32)],
        compiler_params=_cparams("parallel", "arbitrary"),
        name="gdn_chunk",
    )(kvq, kvq, gb, gb, *consts)


def _merge_kernel(x_ref, g1_ref, yf_ref, yb_ref, u5_ref, dskip_ref, gluw_ref, glub_ref,
                  of_ref, ob_ref, z_ref, gain_ref, avg_ref, yfn_ref, ysg_ref, gate_ref,
                  wbr_ref, wout_ref, o_ref):
    d = x_ref.shape[1]
    y5 = _gelu(yf_ref[...] + yb_ref[...] + dskip_ref[...] * u5_ref[...])
    y5 = y5 * _sigmoid(_dot(y5.astype(BF16), gluw_ref[...]) + glub_ref[...])
    o = of_ref[...] + ob_ref[...]
    yg = o * lax.rsqrt(_dot_sel(o * o, avg_ref[...], 2) + EPS) * gain_ref[...] * _silu(z_ref[...])
    ys = (y5, yfn_ref[...], yg, ysg_ref[...])
    acc = None
    for j in range(N_BRANCH):
        t = gate_ref[:, j * d:(j + 1) * d].astype(F32) * _dot(ys[j].astype(BF16), wbr_ref[j])
        acc = t if acc is None else acc + t
    o_ref[...] = x_ref[...] + g1_ref[...] * _dot(acc.astype(BF16), wout_ref[...])


def _merge(x3, mod, acts, consts, n_lat_tiles, n_tiles):
    nb, ttot, d = x3.shape
    yf, yb, u5, of, ob, z, yfn, ysg, gate = acts
    dskip, gluw, glub, gain, avg, wbr, wout = consts
    return pl.pallas_call(
        _merge_kernel,
        grid=(nb, n_tiles),
        in_specs=[_tile_spec(d), _mod_spec(nb, d, 2, n_lat_tiles),
                  _tile_spec(W), _tile_spec(W), _tile_spec(W),
                  _resident(dskip.shape), _resident(gluw.shape), _resident(glub.shape),
                  _tile_spec(W), _tile_spec(W), _tile_spec(W),
                  _resident(gain.shape), _resident(avg.shape),
                  _tile_spec(W), _tile_spec(W), _tile_spec(N_BRANCH * d),
                  _resident(wbr.shape), _resident(wout.shape)],
        out_specs=_tile_spec(d),
        out_shape=jax.ShapeDtypeStruct((nb, n_tiles * ROW_TILE, d), F32),
        compiler_params=_cparams("parallel", "parallel"),
        name="merge",
    )(x3, mod, yf, yb, u5, dskip, gluw, glub, of, ob, z, gain, avg, yfn, ysg, gate, wbr, wout)


def _ffn_kernel(x_ref, sh_ref, sc_ref, g2_ref, gain_ref, w1_ref, w2_ref, nf_ref, o_ref, *, final):
    dff = w2_ref.shape[0]
    x = x_ref[...]
    h = _modulated_norm(x, gain_ref[...], sc_ref[...], sh_ref[...])
    t = _dot(h.astype(BF16), w1_ref[...])
    act = (_silu(t[:, :dff]) * t[:, dff:]).astype(BF16)
    y = x + g2_ref[...] * _dot(act, w2_ref[...])
    if final:
        y = y * lax.rsqrt(jnp.mean(y * y, axis=-1, keepdims=True) + EPS) * nf_ref[...]
    o_ref[...] = y


def _ffn(x3, mod, gain, w1, w2, norm_f, n_lat_tiles, final):
    nb, ttot, d = x3.shape
    return pl.pallas_call(
        functools.partial(_ffn_kernel, final=final),
        grid=(nb, ttot // ROW_TILE),
        in_specs=[_tile_spec(d), _mod_spec(nb, d, 3, n_lat_tiles), _mod_spec(nb, d, 4, n_lat_tiles),
                  _mod_spec(nb, d, 5, n_lat_tiles),
                  _resident(gain.shape), _resident(w1.shape), _resident(w2.shape),
                  _resident(norm_f.shape)],
        out_specs=_tile_spec(d),
        out_shape=jax.ShapeDtypeStruct(x3.shape, F32),
        compiler_params=_cparams("parallel", "parallel"),
        name="ffn",
    )(x3, mod, mod, mod, gain, w1, w2, norm_f)


def _dft_tables(n):
    idx = np.arange(n, dtype=np.int64)
    ang = 2.0 * np.pi * ((idx[:, None] * idx[None, :]) % n).astype(np.float64) / n
    return np.cos(ang), np.sin(ang)


def _s5_tables(lam_re, lam_im, log_dt, b_re, b_im, c_re, c_im):
    g, p, cg = S5_GROUPS, S5_STATE, S5_GROUP
    lam = lax.complex(lam_re.astype(F32), lam_im.astype(F32))
    a_bar = jnp.exp(lam * jnp.exp(log_dt.astype(F32))[:, None])
    b_bar = ((a_bar - 1.0) / lam)[..., None] * lax.complex(b_re.astype(F32), b_im.astype(F32))
    eye = jnp.eye(g, dtype=F32)
    bm_re = jnp.einsum('gpc,gh->gchp', b_bar.real, eye).reshape(g * cg, g * p)
    bm_im = jnp.einsum('gpc,gh->gchp', b_bar.imag, eye).reshape(g * cg, g * p)
    bmat = jnp.concatenate([bm_re, bm_im], axis=1).astype(BF16)
    cm_re = jnp.einsum('gcp,gh->gphc', c_re.astype(F32), eye).reshape(g * p, g * cg)
    cm_im = jnp.einsum('gcp,gh->gphc', c_im.astype(F32), eye).reshape(g * p, g * cg)
    cmat = jnp.concatenate([cm_re, -cm_im], axis=0).astype(BF16)
    avec = jnp.concatenate([a_bar.real.reshape(1, -1), a_bar.imag.reshape(1, -1)], axis=1)
    return bmat, cmat, avec


def _layer_consts(l, p):
    d = p['w_in'].shape[1]
    w_in = p['w_in'][l]
    sizes = (W, W, W, 2 * GDN_HEADS, 2 * GDN_HEADS, W, W, W, W, W, N_BRANCH * d)
    offs = np.concatenate([[0], np.cumsum(sizes)])
    col = lambda j: w_in[:, offs[j]:offs[j + 1]]
    u5, k, v, a, bt, q, z, ufn, usg, vsg, gate = [col(j) for j in range(len(sizes))]
    cc, sc = _dft_tables(FN_GROUP)
    fw = p['fn_w'][l].astype(F32)
    wc = jnp.einsum('cd,gde->gce', jnp.asarray(cc, F32), fw, precision=HIGHEST)
    ws = jnp.einsum('cd,gde->gce', jnp.asarray(sc, F32), fw, precision=HIGHEST)
    ufn_g = ufn.reshape(d, FN_GROUPS, FN_GROUP)
    fold_c = jnp.einsum('kgc,gce->kge', ufn_g, wc, precision=HIGHEST).reshape(d, W)
    fold_s = jnp.einsum('kgc,gce->kge', ufn_g, ws, precision=HIGHEST).reshape(d, W)
    w_ab = jnp.concatenate([a, bt, jnp.zeros((d, LANES - 4 * GDN_HEADS), F32)], axis=1)
    ws_in = [u5, jnp.concatenate([k, v, q], axis=1), w_ab, z,
             jnp.concatenate([fold_c, fold_s], axis=1), jnp.concatenate([usg, vsg], axis=1), gate]
    ws_in = [w.astype(BF16) for w in ws_in]

    s5 = [_s5_tables(p['s5_lam_re'][l, dr], p['s5_lam_im'][l, dr], p['s5_log_dt'][l, dr],
                     p['s5_b_re'][l, dr], p['s5_b_im'][l, dr], p['s5_c_re'][l, dr],
                     p['s5_c_im'][l, dr]) for dr in range(2)]

    conv = p['gdn_conv'][l].astype(F32)
    cw = jnp.transpose(conv, (2, 0, 1)).reshape(conv.shape[2], 3 * W)
    pad = jnp.zeros((1, LANES - 2 * GDN_HEADS), F32)
    alog = jnp.concatenate([p['gdn_a_log'][l].astype(F32).reshape(1, -1), pad], axis=1)
    dtb = jnp.concatenate([p['gdn_dt_bias'][l].astype(F32).reshape(1, -1), pad], axis=1)

    sg_bias = jnp.repeat(p['sg_b'][l].astype(F32).T, SG_GROUP, axis=1)
    merge_consts = (p['s5_d'][l].astype(F32).reshape(1, W), p['s5_glu_w'][l].astype(BF16),
                    p['s5_glu_b'][l].astype(F32).reshape(1, W),
                    jnp.tile(p['gdn_norm'][l].astype(F32), GDN_HEADS).reshape(1, W))
    return dict(
        ws_in=ws_in, s5=s5, cw=cw, alog=alog, dtb=dtb,
        sg_w=p['sg_w'][l].astype(BF16), sg_bias=sg_bias,
        sg_lng=p['sg_ln_g'][l].astype(F32).reshape(1, W), sg_lnb=p['sg_ln_b'][l].astype(F32).reshape(1, W),
        merge=merge_consts, wbr=p['w_branch'][l].astype(BF16), wout=p['w_out'][l].astype(BF16),
        w1=p['ffn_w1'][l].astype(BF16), w2=p['ffn_w2'][l].astype(BF16),
        norm1=p['norm1'][l].astype(F32).reshape(1, d), norm2=p['norm2'][l].astype(F32).reshape(1, d))


def kernel(x, c, ctx, c_ctx, ada_w, ada_b, norm1, norm2, w_in, s5_lam_re, s5_lam_im, s5_log_dt, s5_b_re, s5_b_im, s5_c_re, s5_c_im, s5_d, s5_glu_w, s5_glu_b, fn_w, gdn_conv, gdn_a_log, gdn_dt_bias, gdn_norm, sg_ln_g, sg_ln_b, sg_w, sg_b, w_branch, w_out, ffn_w1, ffn_w2, norm_f):
    p = dict(w_in=w_in, s5_lam_re=s5_lam_re, s5_lam_im=s5_lam_im, s5_log_dt=s5_log_dt,
             s5_b_re=s5_b_re, s5_b_im=s5_b_im, s5_c_re=s5_c_re, s5_c_im=s5_c_im, s5_d=s5_d,
             s5_glu_w=s5_glu_w, s5_glu_b=s5_glu_b, fn_w=fn_w, gdn_conv=gdn_conv,
             gdn_a_log=gdn_a_log, gdn_dt_bias=gdn_dt_bias, gdn_norm=gdn_norm, sg_ln_g=sg_ln_g,
             sg_ln_b=sg_ln_b, sg_w=sg_w, sg_b=sg_b, w_branch=w_branch, w_out=w_out,
             ffn_w1=ffn_w1, ffn_w2=ffn_w2, norm1=norm1, norm2=norm2)
    nb, l_lat, d = x.shape
    l_ctx = ctx.shape[1]
    depth = ada_w.shape[0]
    ttot = l_lat + l_ctx
    assert nb % SUBLANES == 0 and d == N_BRANCH * W and l_lat % l_ctx == 0
    assert l_lat % ROW_TILE == 0 and l_ctx % ROW_TILE == 0 and ROW_TILE % SG_CHUNK == 0
    n_lat_tiles = l_lat // ROW_TILE
    n_tiles = ttot // ROW_TILE
    sg_rows = max(r for r in range(SG_CHUNK, 7 * SG_CHUNK, SG_CHUNK) if ttot % r == 0)

    x3 = jnp.concatenate([x, ctx], axis=1).astype(F32)

    ada_rows = -(-(nb + 1) // SUBLANES) * SUBLANES
    cc = jnp.zeros((ada_rows, d), F32).at[:nb].set(c.astype(F32)).at[nb].set(c_ctx.astype(F32))
    mods = _ada(cc, ada_w.astype(F32), ada_b.astype(F32))[:, :nb + 1].reshape(depth, nb + 1, 1, 6 * d)

    cos_l, sin_l = _dft_tables(l_lat)
    cos_c, sin_c = _dft_tables(l_ctx)
    sc_l = 1.0 / math.sqrt(l_lat * FN_GROUP)
    sc_c = 1.0 / math.sqrt(l_ctx * FN_GROUP)
    cos_l, nsin_l = jnp.asarray(cos_l * sc_l, BF16), jnp.asarray(-sin_l * sc_l, BF16)
    cos_c, nsin_c = jnp.asarray(cos_c * sc_c, BF16), jnp.asarray(-sin_c * sc_c, BF16)
    nf = norm_f.astype(F32).reshape(1, d)

    lane_head = np.arange(W) // GDN_HEAD_DIM
    seg = (lane_head[:, None] == lane_head[None, :]).astype(np.float32)
    seg_ones = jnp.asarray(seg, BF16)
    seg_avg = jnp.asarray(seg / GDN_HEAD_DIM, BF16)
    expand = []
    for dr in range(2):
        e = np.zeros((LANES, 2 * W), np.float32)
        for h in range(GDN_HEADS):
            e[dr * GDN_HEADS + h, :W] = lane_head == h
            e[2 * GDN_HEADS + dr * GDN_HEADS + h, W:] = lane_head == h
        expand.append(jnp.asarray(e, BF16))
    gdn_masks = [_gdn_masks(False), _gdn_masks(True)]
    bd = jnp.asarray(seg, F32)
    eye = jnp.eye(W, dtype=F32)
    src = (np.arange(nb)[None, :] * SUBLANES + np.arange(SUBLANES)[:, None]).reshape(-1)
    perm_np = np.zeros((nb * SUBLANES, nb * SUBLANES), np.float32)
    perm_np[np.arange(nb * SUBLANES), src] = 1.0
    perm, permt = jnp.asarray(perm_np, BF16), jnp.asarray(perm_np.T, BF16)

    for l in range(depth):
        k = _layer_consts(l, p)
        mod = mods[l]
        last = l == depth - 1
        u5, kvq, ab, z, afn, sg, gate = _kin(x3, mod, k['norm1'], k['ws_in'], n_lat_tiles)
        yf, yb = _s5(u5, perm, permt, [t[0] for t in k['s5']], [t[1] for t in k['s5']],
                     [t[2] for t in k['s5']], l_lat // S5_STEPS)
        yfn = _fnet(afn, cos_l, nsin_l, cos_c, nsin_c, l_lat, l_ctx)
        ysg = _sgu(sg, k['sg_w'], k['sg_bias'], k['sg_lng'], k['sg_lnb'], seg_avg, sg_rows)
        kvq_p, gb = _gdn_prep(kvq, ab, k['cw'], seg_ones, k['alog'], k['dtb'], n_lat_tiles)
        of, ob = _gdn_chunk(kvq_p, gb, expand, gdn_masks, bd, eye, l_lat, 2)
        acts = (yf, yb, u5, of, ob, z, yfn, ysg, gate)
        consts = k['merge'] + (seg_avg, k['wbr'], k['wout'])
        x3 = _merge(x3, mod, acts, consts, n_lat_tiles, n_lat_tiles if last else n_tiles)
        x3 = _ffn(x3, mod, k['norm2'], k['w1'], k['w2'], nf, n_lat_tiles, final=last)

    return x3.astype(x.dtype)
```

```python
import functools
import math

import numpy as np
import jax
import jax.numpy as jnp
from jax import lax
from jax.experimental import pallas as pl
from jax.experimental.pallas import tpu as pltpu

F32 = jnp.float32
BF16 = jnp.bfloat16
HIGHEST = lax.Precision.HIGHEST

EPS = 1e-6
W = 256
N_BRANCH = 4
S5_GROUP = 16
S5_GROUPS = W // S5_GROUP
S5_STATE = 64
N_S5 = S5_GROUPS * S5_STATE
FN_GROUPS = 4
FN_GROUP = W // FN_GROUPS
GDN_HEADS = 4
GDN_HEAD_DIM = W // GDN_HEADS
GDN_CHUNK = 64
GDN_BASE = 8
SG_GROUPS = 4
SG_GROUP = W // SG_GROUPS
SG_CHUNK = 128
LANES = 128
SUBLANES = 8
ROW_TILE = 256
S5_STEPS = 32
S5_LANE_GROUP = 256
VMEM_LIMIT = 56 * 1024 * 1024


def _cparams(*sem):
    return pltpu.CompilerParams(dimension_semantics=sem, vmem_limit_bytes=VMEM_LIMIT)


def _resident(shape):
    nd = len(shape)
    return pl.BlockSpec(shape, lambda *_: (0,) * nd, pipeline_mode=pl.Buffered(1))


def _dot(a, b):
    return jnp.dot(a, b, preferred_element_type=F32)


def _split(x, parts):
    out = []
    for _ in range(parts - 1):
        hi = x.astype(BF16)
        out.append(hi)
        x = x - hi.astype(F32)
    out.append(x.astype(BF16))
    return out


def _dot_sel(x, sel, parts):
    acc = None
    for piece in _split(x, parts):
        t = _dot(piece, sel)
        acc = t if acc is None else acc + t
    return acc


def _gelu(x):
    return 0.5 * x * (1.0 + jnp.tanh(math.sqrt(2.0 / math.pi) * (x + 0.044715 * (x * x * x))))


def _sigmoid(x):
    return 1.0 / (1.0 + jnp.exp(-x))


def _silu(x):
    return x * _sigmoid(x)


def _modulated_norm(x, gain, scale, shift):
    y = x * lax.rsqrt(jnp.mean(x * x, axis=-1, keepdims=True) + EPS) * gain
    return y * (1.0 + scale) + shift


def _ada_kernel(c_ref, w_ref, b_ref, o_ref):
    c = c_ref[...]
    o_ref[0] = jnp.dot(_silu(c), w_ref[0], preferred_element_type=F32, precision=HIGHEST) + b_ref[0]


def _ada(cc, ada_w, ada_b):
    depth, d, n = ada_w.shape
    rows = cc.shape[0]
    return pl.pallas_call(
        _ada_kernel,
        grid=(depth, n // d),
        in_specs=[pl.BlockSpec((rows, d), lambda l, j: (0, 0)),
                  pl.BlockSpec((1, d, d), lambda l, j: (l, 0, j)),
                  pl.BlockSpec((1, 1, d), lambda l, j: (l, 0, j))],
        out_specs=pl.BlockSpec((1, rows, d), lambda l, j: (l, 0, j)),
        out_shape=jax.ShapeDtypeStruct((depth, rows, n), F32),
        compiler_params=_cparams("parallel", "parallel"),
        name="ada",
    )(cc, ada_w, ada_b.reshape(depth, 1, n))


def _tile_spec(n):
    return pl.BlockSpec((None, ROW_TILE, n), lambda b, j: (b, j, 0))


def _mod_spec(nb, d, k, n_lat_tiles):
    return pl.BlockSpec((None, 1, d), lambda b, j: (jnp.where(j < n_lat_tiles, b, nb), 0, k))


def _kin_kernel(x_ref, sh_ref, sc_ref, g_ref, w5, wkvq, wab, wz, wfn, wsg, wgate,
                o5, okvq, oab, oz, ofn, osg, ogate):
    d = x_ref.shape[1]
    hb = _modulated_norm(x_ref[...], g_ref[...], sc_ref[...], sh_ref[...]).astype(BF16)
    for w_ref, o_ref in ((w5, o5), (wkvq, okvq), (wab, oab), (wz, oz), (wfn, ofn), (wsg, osg)):
        o_ref[...] = _dot(hb, w_ref[...]).astype(o_ref.dtype)
    for j in range(N_BRANCH):
        cols = slice(j * d, (j + 1) * d)
        ogate[:, cols] = _sigmoid(_dot(hb, wgate[:, cols])).astype(BF16)


def _kin(x3, mod, gain, ws, n_lat_tiles):
    nb, ttot, d = x3.shape
    widths = [w.shape[1] for w in ws]
    dts = [F32, BF16, F32, BF16, BF16, BF16, BF16]
    return pl.pallas_call(
        _kin_kernel,
        grid=(nb, ttot // ROW_TILE),
        in_specs=[_tile_spec(d), _mod_spec(nb, d, 0, n_lat_tiles), _mod_spec(nb, d, 1, n_lat_tiles),
                  _resident((1, d))] + [_resident(w.shape) for w in ws],
        out_specs=[_tile_spec(n) for n in widths],
        out_shape=[jax.ShapeDtypeStruct((nb, ttot, n), dt) for n, dt in zip(widths, dts)],
        compiler_params=_cparams("parallel", "parallel"),
        name="kin",
    )(x3, mod, mod, gain, *ws)


def _s5_kernel(uf_ref, ub_ref, perm_ref, permt_ref, bmf, bmb, cmf, cmb, af_ref, ab_ref,
               yf_ref, yb_ref, st_ref, bu_ref, s_ref):
    i = pl.program_id(0)
    nb, tt, _ = uf_ref.shape
    sub = SUBLANES
    n_grp, _, gw = bmf.shape
    half = gw // 2

    @pl.when(i == 0)
    def _():
        st_ref[...] = jnp.zeros_like(st_ref)

    dirs = ((uf_ref, bmf, cmf, af_ref, yf_ref, range(tt)),
            (ub_ref, bmb, cmb, ab_ref, yb_ref, range(tt - 1, -1, -1)))

    def regroup(dr):
        u_ref = dirs[dr][0]
        parts = []
        for k in range(tt // sub):
            blk = u_ref[:, k * sub:(k + 1) * sub, :].reshape(nb * sub, W).astype(BF16)
            parts.append(_dot(perm_ref[...], blk).astype(BF16))
        return jnp.concatenate(parts, axis=0)

    u_tb = [regroup(0), regroup(1)]
    y_tb = [None, None]

    def project(dr, g):
        bu_ref[dr, g] = _dot(u_tb[dr], dirs[dr][1][g])

    def recur(dr, g):
        a_ref, order = dirs[dr][3], dirs[dr][5]
        for c in range(half // LANES):
            re = slice(c * LANES, (c + 1) * LANES)
            im = slice(half + c * LANES, half + (c + 1) * LANES)
            a_re = a_ref[g, :, re]
            a_im = a_ref[g, :, im]
            s_re = st_ref[dr, g, :, re]
            s_im = st_ref[dr, g, :, im]
            for t in order:
                r = slice(t * nb, (t + 1) * nb)
                n_re = a_re * s_re - a_im * s_im + bu_ref[dr, g, r, re]
                n_im = a_re * s_im + a_im * s_re + bu_ref[dr, g, r, im]
                s_re, s_im = n_re, n_im
                s_ref[dr, g, r, re] = s_re.astype(BF16)
                s_ref[dr, g, r, im] = s_im.astype(BF16)
            st_ref[dr, g, :, re] = s_re
            st_ref[dr, g, :, im] = s_im

    def read_out(dr, g):
        t = _dot(s_ref[dr, g], dirs[dr][2][g])
        y_tb[dr] = t if y_tb[dr] is None else y_tb[dr] + t

    for dr in range(2):
        project(dr, 0)
    for g in range(n_grp):
        for dr in range(2):
            if g + 1 < n_grp:
                project(dr, g + 1)
            recur(dr, g)
        for dr in range(2):
            read_out(dr, g)
    for dr in range(2):
        y_ref = dirs[dr][4]
        for k in range(tt // sub):
            hi, lo = _split(y_tb[dr][k * sub * nb:(k + 1) * sub * nb], 2)
            y_bt = _dot(permt_ref[...], hi) + _dot(permt_ref[...], lo)
            y_ref[:, k * sub:(k + 1) * sub, :] = y_bt.reshape(nb, sub, W)


def _s5(u5, perm, permt, bmats, cmats, avecs, n_lat_tiles):
    nb, ttot, _ = u5.shape
    tt = S5_STEPS
    n_tiles = ttot // tt
    fwd = lambda i: (0, (i + n_lat_tiles) % n_tiles, 0)
    bwd = lambda i: (0, n_tiles - 1 - i, 0)
    rows = tt * nb
    n_grp, _, gw = bmats[0].shape
    return pl.pallas_call(
        _s5_kernel,
        grid=(n_tiles,),
        in_specs=[pl.BlockSpec((nb, tt, W), fwd), pl.BlockSpec((nb, tt, W), bwd),
                  _resident(perm.shape), _resident(permt.shape),
                  _resident(bmats[0].shape), _resident(bmats[1].shape),
                  _resident(cmats[0].shape), _resident(cmats[1].shape),
                  _resident(avecs[0].shape), _resident(avecs[1].shape)],
        out_specs=[pl.BlockSpec((nb, tt, W), fwd), pl.BlockSpec((nb, tt, W), bwd)],
        out_shape=[jax.ShapeDtypeStruct((nb, ttot, W), F32)] * 2,
        scratch_shapes=[pltpu.VMEM((2, n_grp, nb, gw), F32),
                        pltpu.VMEM((2, n_grp, rows, gw), F32),
                        pltpu.VMEM((2, n_grp, rows, gw), BF16)],
        compiler_params=_cparams("arbitrary"),
        name="s5",
    )(u5, u5, perm, permt, bmats[0], bmats[1], cmats[0], cmats[1], avecs[0], avecs[1])


def _fnet_kernel(a_ref, cos_ref, nsin_ref, y_ref):
    y = _dot(cos_ref[...], a_ref[:, :W]) + _dot(nsin_ref[...], a_ref[:, W:])
    y_ref[...] = y.astype(y_ref.dtype)


def _fnet_alias_kernel(a_ref, cos_ref, nsin_ref, prev_ref, y_ref):
    _fnet_kernel(a_ref, cos_ref, nsin_ref, y_ref)


def _fnet(afn, cos_l, nsin_l, cos_c, nsin_c, l_lat, l_ctx):
    nb, ttot, _ = afn.shape
    lat = pl.pallas_call(
        _fnet_kernel,
        grid=(nb,),
        in_specs=[pl.BlockSpec((None, l_lat, 2 * W), lambda b: (b, 0, 0)),
                  _resident(cos_l.shape), _resident(nsin_l.shape)],
        out_specs=pl.BlockSpec((None, l_lat, W), lambda b: (b, 0, 0)),
        out_shape=jax.ShapeDtypeStruct((nb, ttot, W), BF16),
        compiler_params=_cparams("parallel"),
        name="fnet_lat",
    )(afn, cos_l, nsin_l)
    blk = l_lat // l_ctx
    return pl.pallas_call(
        _fnet_alias_kernel,
        grid=(nb,),
        in_specs=[pl.BlockSpec((None, l_ctx, 2 * W), lambda b: (b, blk, 0)),
                  _resident(cos_c.shape), _resident(nsin_c.shape),
                  pl.BlockSpec(memory_space=pl.ANY)],
        out_specs=pl.BlockSpec((None, l_ctx, W), lambda b: (b, blk, 0)),
        out_shape=jax.ShapeDtypeStruct((nb, ttot, W), BF16),
        input_output_aliases={3: 0},
        compiler_params=_cparams("parallel"),
        name="fnet_ctx",
    )(afn, cos_c, nsin_c, lat)


def _sgu_kernel(uv_ref, w_ref, bias_ref, lng_ref, lnb_ref, avg_ref, y_ref):
    lane_group = lax.broadcasted_iota(jnp.int32, (SG_CHUNK, W), 1) // SG_GROUP
    rows = [slice(n * SG_CHUNK, (n + 1) * SG_CHUNK) for n in range(uv_ref.shape[0] // SG_CHUNK)]
    v = [_gelu(uv_ref[r, W:].astype(F32)) for r in rows]
    dv = [t - _dot_sel(t, avg_ref[...], 2) for t in v]
    var = [_dot_sel(t * t, avg_ref[...], 2) for t in dv]
    vn = [(d_ * lax.rsqrt(s_ + EPS) * lng_ref[...] + lnb_ref[...]).astype(BF16) for d_, s_ in zip(dv, var)]
    sv = [_dot(w_ref[0], t) for t in vn]
    for g in range(1, SG_GROUPS):
        sv = [jnp.where(lane_group == g, _dot(w_ref[g], t), s_) for t, s_ in zip(vn, sv)]
    for r, s_ in zip(rows, sv):
        y_ref[r, :] = (_gelu(uv_ref[r, :W].astype(F32)) * (s_ + bias_ref[...])).astype(y_ref.dtype)


def _sgu(sg, w, bias, lng, lnb, avg, rows):
    nb, ttot, _ = sg.shape
    return pl.pallas_call(
        _sgu_kernel,
        grid=(nb, ttot // rows),
        in_specs=[pl.BlockSpec((None, rows, 2 * W), lambda b, n: (b, n, 0)),
                  _resident(w.shape), _resident(bias.shape), _resident(lng.shape),
                  _resident(lnb.shape), _resident(avg.shape)],
        out_specs=pl.BlockSpec((None, rows, W), lambda b, n: (b, n, 0)),
        out_shape=jax.ShapeDtypeStruct((nb, ttot, W), BF16),
        compiler_params=_cparams("parallel", "parallel"),
        name="sgu",
    )(sg, w, bias, lng, lnb, avg)


def _gdn_prep_kernel(x_ref, xp_ref, xn_ref, ab_ref, cw_ref, ones_ref, alog_ref, dtb_ref,
                     o_ref, gb_ref, *, n_lat_tiles, n_tiles):
    j = pl.program_id(1)
    rows, n = x_ref.shape
    halo = xp_ref.shape[0]
    x = x_ref[...].astype(F32)
    has_prev = jnp.logical_and(j != 0, j != n_lat_tiles).astype(F32)
    has_next = jnp.logical_and(j != n_lat_tiles - 1, j != n_tiles - 1).astype(F32)
    row = lax.broadcasted_iota(jnp.int32, (rows, n), 0)
    x_before = xp_ref[...].astype(F32)[halo - 1:halo, :] * has_prev
    x_after = xn_ref[...].astype(F32)[0:1, :] * has_next
    xp = jnp.where(row == 0, x_before, pltpu.roll(x, 1, 0))
    xn = jnp.where(row == rows - 1, x_after, pltpu.roll(x, rows - 1, 0))
    y = _silu(xp * cw_ref[0:1, :] + x * cw_ref[1:2, :] + xn * cw_ref[2:3, :])
    k = y[:, :W]
    q = y[:, 2 * W:]
    o_ref[:, :W] = (k * lax.rsqrt(_dot_sel(k * k, ones_ref[...], 2) + EPS)).astype(o_ref.dtype)
    o_ref[:, W:2 * W] = y[:, W:2 * W].astype(o_ref.dtype)
    o_ref[:, 2 * W:] = (q * lax.rsqrt(_dot_sel(q * q, ones_ref[...], 2) + EPS)
                        * GDN_HEAD_DIM ** -0.5).astype(o_ref.dtype)
    ab = ab_ref[...]
    z = ab + dtb_ref[...]
    softplus = jnp.maximum(z, 0.0) + jnp.log(1.0 + jnp.exp(-jnp.abs(z)))
    g = -jnp.exp(alog_ref[...]) * softplus
    lane = lax.broadcasted_iota(jnp.int32, ab.shape, 1)
    gb_ref[...] = jnp.where(lane < 2 * GDN_HEADS, g, _sigmoid(ab))


def _gdn_prep(kvq, ab, cw, ones_seg, alog, dtb, n_lat_tiles):
    nb, ttot, n = kvq.shape
    n_tiles = ttot // ROW_TILE
    halo = 2 * SUBLANES
    per = ROW_TILE // halo
    last = ttot // halo - 1
    return pl.pallas_call(
        functools.partial(_gdn_prep_kernel, n_lat_tiles=n_lat_tiles, n_tiles=n_tiles),
        grid=(nb, n_tiles),
        in_specs=[_tile_spec(n),
                  pl.BlockSpec((None, halo, n), lambda b, j: (b, jnp.maximum(j * per - 1, 0), 0)),
                  pl.BlockSpec((None, halo, n), lambda b, j: (b, jnp.minimum((j + 1) * per, last), 0)),
                  _tile_spec(LANES),
                  _resident(cw.shape), _resident(ones_seg.shape),
                  _resident(alog.shape), _resident(dtb.shape)],
        out_specs=[_tile_spec(n), _tile_spec(LANES)],
        out_shape=[jax.ShapeDtypeStruct((nb, ttot, n), BF16),
                   jax.ShapeDtypeStruct((nb, ttot, LANES), F32)],
        compiler_params=_cparams("parallel", "parallel"),
        name="gdn_prep",
    )(kvq, kvq, kvq, ab, cw, ones_seg, alog, dtb)


def _gdn_masks(backward):
    c, heads = GDN_CHUNK, GDN_HEADS
    i = np.arange(c)[:, None]
    j = (np.arange(W) % c)[None, :]
    incl = (i <= j) if backward else (i >= j)
    strict = (i < j) if backward else (i > j)
    j64 = np.arange(c)[None, :]
    incl64 = (i <= j64) if backward else (i >= j64)
    same = lambda n: (i // n) == (j // n)
    levels = [same(GDN_BASE)]
    n = GDN_BASE
    while n < c:
        levels.append(same(2 * n) & ~same(n))
        n *= 2
    f = lambda m: jnp.asarray(m, F32)
    return (f(np.stack([incl, strict, i == j])), jnp.asarray(incl64, BF16), f(np.stack(levels)))


def _gdn_chains(chains, bd_ref, bd16_ref):
    c, heads = GDN_CHUNK, GDN_HEADS
    each = lambda f, *cols: [f(*args) for args in zip(*cols)]

    def expand_heads(t):
        t = t.astype(BF16)
        return jnp.concatenate([t] * heads, axis=0) * bd16_ref[:, :t.shape[1]]

    kvq, gb, expand, m64, incl64, lvls, s_refs = zip(*chains)
    kvq = [t.astype(F32) for t in kvq]
    k = [t[:, :W] for t in kvq]
    v = [t[:, W:2 * W] for t in kvq]
    q = [t[:, 2 * W:] for t in kvq]
    ge = each(lambda g, e: _dot_sel(g, e[...], 2), gb, expand)
    g_l = [t[:, :W] for t in ge]
    beta_l = [t[:, W:] for t in ge]
    incl = [m[0] for m in m64]
    strict = [m[1] for m in m64]
    diag = [m[2] for m in m64]

    def cumulative(g, i64):
        acc = None
        for piece in _split(g, 3):
            t = _dot(i64[...], piece)
            acc = t if acc is None else acc + t
        return acc

    gc = each(cumulative, g_l, incl64)
    dif = each(lambda g, dg: g - jnp.sum(g * dg, axis=0, keepdims=True), gc, diag)
    g_tot = [jnp.sum(t, axis=0, keepdims=True) for t in g_l]
    kb = each(lambda a_, b_: a_ * b_, k, beta_l)
    k_st = [expand_heads(t) for t in k]
    kq = each(lambda a_, b_, st: lax.dot_general(
        jnp.concatenate([a_, b_], axis=0).astype(BF16), st, (((1,), (1,)), ((), ())),
        preferred_element_type=F32), kb, q, k_st)
    rel = each(lambda d_, i: jnp.exp(d_ * i) * i, dif, incl)
    a = each(lambda t, r, st: t[:c] * r * st, kq, rel, strict)
    qk = each(lambda t, r: t[c:] * r, kq, rel)
    pw = each(lambda t, lv: t * lv[0], a, lvls)
    t_inv = each(lambda dg, t: dg - t, diag, pw)
    pw_bd = [expand_heads(t) for t in pw]
    for _ in range(int(math.log2(GDN_BASE)) - 1):
        pw = each(lambda t, bd_: _dot(t.astype(BF16), bd_), pw, pw_bd)
        pw_bd = [expand_heads(t) for t in pw]
        t_inv = each(lambda t, bd_: t + _dot(t.astype(BF16), bd_), t_inv, pw_bd)
    for lvl in range(1, lvls[0].shape[0]):
        z_bd = each(lambda t, lv: expand_heads(t * lv[lvl]), a, lvls)
        tz = each(lambda t, z_: _dot(t.astype(BF16), z_), t_inv, z_bd)
        t_inv = each(lambda t, tz_: t - _dot(tz_.astype(BF16), expand_heads(t)), t_inv, tz)
    e_gc = [jnp.exp(t) for t in gc]
    rhs = each(lambda v_, b_, kb_, e: expand_heads(jnp.concatenate([v_ * b_, kb_ * e], axis=1)),
               v, beta_l, kb, e_gc)
    w_all = each(lambda t, r: _dot(t.astype(BF16), r), t_inv, rhs)
    s = [r[...] for r in s_refs]
    sb = [t.astype(BF16) for t in s]
    ws = each(lambda w_, q_, e, sb_: _dot(jnp.concatenate([w_[:, W:], q_ * e], axis=0).astype(BF16), sb_),
              w_all, q, e_gc, sb)
    u = each(lambda w_, t: w_[:, :W] - t[:c], w_all, ws)
    o = each(lambda t, qk_, u_: t[c:] + _dot(qk_.astype(BF16), expand_heads(u_)), ws, qk, u)
    k_dec = each(lambda k_, gt, g: k_ * jnp.exp(gt - g), k, g_tot, gc)
    upd = each(lambda kd, u_: lax.dot_general(kd.astype(BF16), u_.astype(BF16), (((0,), (0,)), ((), ())),
                                              preferred_element_type=F32), k_dec, u)
    for r, s_, gt, up in zip(s_refs, s, g_tot, upd):
        r[...] = s_ * jnp.exp(gt) + up * bd_ref[...]
    return o


def _gdn_chunk_kernel(kvqf_ref, kvqb_ref, gbf_ref, gbb_ref, ef_ref, eb_ref, m64f_ref, m64b_ref,
                      i64f_ref, i64b_ref, lvlf_ref, lvlb_ref, bd_ref, bd16_ref, of_ref, ob_ref, s_ref):
    @pl.when(pl.program_id(1) == 0)
    def _():
        s_ref[...] = jnp.zeros_like(s_ref)

    chains = []
    for bi in range(kvqf_ref.shape[0]):
        chains.append((kvqf_ref[bi], gbf_ref[bi], ef_ref, m64f_ref, i64f_ref, lvlf_ref, s_ref.at[bi, 0]))
        chains.append((kvqb_ref[bi], gbb_ref[bi], eb_ref, m64b_ref, i64b_ref, lvlb_ref, s_ref.at[bi, 1]))
    outs = _gdn_chains(chains, bd_ref, bd16_ref)
    for bi in range(kvqf_ref.shape[0]):
        of_ref[bi] = outs[2 * bi].astype(of_ref.dtype)
        ob_ref[bi] = outs[2 * bi + 1].astype(ob_ref.dtype)


def _gdn_chunk(kvq, gb, expand, masks, bd, bd16, l_lat, per_step):
    c = GDN_CHUNK
    nb, ttot, _ = kvq.shape
    n_lat = l_lat // c
    n_chunks = ttot // c
    fwd = lambda b, i: (b, (i + n_lat) % n_chunks, 0)
    bwd = lambda b, i: (b, n_chunks - 1 - i, 0)
    consts = [expand[0], expand[1], masks[0][0], masks[1][0], masks[0][1], masks[1][1],
              masks[0][2], masks[1][2], bd, bd16]
    return pl.pallas_call(
        _gdn_chunk_kernel,
        grid=(nb // per_step, n_chunks),
        in_specs=[pl.BlockSpec((per_step, c, 3 * W), fwd), pl.BlockSpec((per_step, c, 3 * W), bwd),
                  pl.BlockSpec((per_step, c, LANES), fwd), pl.BlockSpec((per_step, c, LANES), bwd)]
                 + [_resident(t.shape) for t in consts],
        out_specs=[pl.BlockSpec((per_step, c, W), fwd), pl.BlockSpec((per_step, c, W), bwd)],
        out_shape=[jax.ShapeDtypeStruct((nb, ttot, W), BF16)] * 2,
        scratch_shapes=[pltpu.VMEM((per_step, 2, W, W), F32)],
        compiler_params=_cparams("parallel", "arbitrary"),
        name="gdn_chunk",
    )(kvq, kvq, gb, gb, *consts)


def _merge_kernel(x_ref, g1_ref, yf_ref, yb_ref, u5_ref, dskip_ref, gluw_ref, glub_ref,
                  of_ref, ob_ref, z_ref, gain_ref, avg_ref, yfn_ref, ysg_ref, gate_ref,
                  wbr_ref, wout_ref, o_ref):
    d = x_ref.shape[1]
    y5 = _gelu(yf_ref[...] + yb_ref[...] + dskip_ref[...] * u5_ref[...])
    y5 = y5 * _sigmoid(_dot(y5.astype(BF16), gluw_ref[...]) + glub_ref[...])
    o = of_ref[...].astype(F32) + ob_ref[...].astype(F32)
    yg = (o * lax.rsqrt(_dot_sel(o * o, avg_ref[...], 2) + EPS) * gain_ref[...]
          * _silu(z_ref[...].astype(F32)))
    ys = (y5, yfn_ref[...], yg, ysg_ref[...])
    acc = None
    for j in range(N_BRANCH):
        t = gate_ref[:, j * d:(j + 1) * d].astype(F32) * _dot(ys[j].astype(BF16), wbr_ref[j])
        acc = t if acc is None else acc + t
    o_ref[...] = x_ref[...] + g1_ref[...] * _dot(acc.astype(BF16), wout_ref[...])


def _merge(x3, mod, acts, consts, n_lat_tiles, n_tiles):
    nb, ttot, d = x3.shape
    yf, yb, u5, of, ob, z, yfn, ysg, gate = acts
    dskip, gluw, glub, gain, avg, wbr, wout = consts
    return pl.pallas_call(
        _merge_kernel,
        grid=(nb, n_tiles),
        in_specs=[_tile_spec(d), _mod_spec(nb, d, 2, n_lat_tiles),
                  _tile_spec(W), _tile_spec(W), _tile_spec(W),
                  _resident(dskip.shape), _resident(gluw.shape), _resident(glub.shape),
                  _tile_spec(W), _tile_spec(W), _tile_spec(W),
                  _resident(gain.shape), _resident(avg.shape),
                  _tile_spec(W), _tile_spec(W), _tile_spec(N_BRANCH * d),
                  _resident(wbr.shape), _resident(wout.shape)],
        out_specs=_tile_spec(d),
        out_shape=jax.ShapeDtypeStruct((nb, n_tiles * ROW_TILE, d), F32),
        compiler_params=_cparams("parallel", "parallel"),
        name="merge",
    )(x3, mod, yf, yb, u5, dskip, gluw, glub, of, ob, z, gain, avg, yfn, ysg, gate, wbr, wout)


def _ffn_kernel(x_ref, sh_ref, sc_ref, g2_ref, gain_ref, w1_ref, w2_ref, nf_ref, o_ref, *, final):
    dff = w2_ref.shape[0]
    x = x_ref[...]
    h = _modulated_norm(x, gain_ref[...], sc_ref[...], sh_ref[...])
    t = _dot(h.astype(BF16), w1_ref[...])
    act = (_silu(t[:, :dff]) * t[:, dff:]).astype(BF16)
    y = x + g2_ref[...] * _dot(act, w2_ref[...])
    if final:
        y = y * lax.rsqrt(jnp.mean(y * y, axis=-1, keepdims=True) + EPS) * nf_ref[...]
    o_ref[...] = y


def _ffn(x3, mod, gain, w1, w2, norm_f, n_lat_tiles, final):
    nb, ttot, d = x3.shape
    return pl.pallas_call(
        functools.partial(_ffn_kernel, final=final),
        grid=(nb, ttot // ROW_TILE),
        in_specs=[_tile_spec(d), _mod_spec(nb, d, 3, n_lat_tiles), _mod_spec(nb, d, 4, n_lat_tiles),
                  _mod_spec(nb, d, 5, n_lat_tiles),
                  _resident(gain.shape), _resident(w1.shape), _resident(w2.shape),
                  _resident(norm_f.shape)],
        out_specs=_tile_spec(d),
        out_shape=jax.ShapeDtypeStruct(x3.shape, F32),
        compiler_params=_cparams("parallel", "parallel"),
        name="ffn",
    )(x3, mod, mod, mod, gain, w1, w2, norm_f)


def _dft_tables(n):
    idx = np.arange(n, dtype=np.int64)
    ang = 2.0 * np.pi * ((idx[:, None] * idx[None, :]) % n).astype(np.float64) / n
    return np.cos(ang), np.sin(ang)


def _s5_tables(lam_re, lam_im, log_dt, b_re, b_im, c_re, c_im):
    g, p, cg = S5_GROUPS, S5_STATE, S5_GROUP
    lam = lax.complex(lam_re.astype(F32), lam_im.astype(F32))
    a_bar = jnp.exp(lam * jnp.exp(log_dt.astype(F32))[:, None])
    b_bar = ((a_bar - 1.0) / lam)[..., None] * lax.complex(b_re.astype(F32), b_im.astype(F32))
    eye = jnp.eye(g, dtype=F32)
    bm_re = jnp.einsum('gpc,gh->gchp', b_bar.real, eye).reshape(g * cg, g * p)
    bm_im = jnp.einsum('gpc,gh->gchp', b_bar.imag, eye).reshape(g * cg, g * p)
    cm_re = jnp.einsum('gcp,gh->gphc', c_re.astype(F32), eye).reshape(g * p, g * cg)
    cm_im = jnp.einsum('gcp,gh->gphc', c_im.astype(F32), eye).reshape(g * p, g * cg)
    ng, h = N_S5 // S5_LANE_GROUP, S5_LANE_GROUP
    bmat = jnp.concatenate([bm_re.reshape(W, ng, h), bm_im.reshape(W, ng, h)], axis=2)
    bmat = jnp.transpose(bmat, (1, 0, 2)).astype(BF16)
    cmat = jnp.concatenate([cm_re.reshape(ng, h, W), -cm_im.reshape(ng, h, W)], axis=1).astype(BF16)
    avec = jnp.concatenate([a_bar.real.reshape(ng, 1, h), a_bar.imag.reshape(ng, 1, h)], axis=2)
    return bmat, cmat, avec


def _layer_consts(l, p):
    d = p['w_in'].shape[1]
    w_in = p['w_in'][l]
    sizes = (W, W, W, 2 * GDN_HEADS, 2 * GDN_HEADS, W, W, W, W, W, N_BRANCH * d)
    offs = np.concatenate([[0], np.cumsum(sizes)])
    col = lambda j: w_in[:, offs[j]:offs[j + 1]]
    u5, k, v, a, bt, q, z, ufn, usg, vsg, gate = [col(j) for j in range(len(sizes))]
    cc, sc = _dft_tables(FN_GROUP)
    fw = p['fn_w'][l].astype(F32)
    wc = jnp.einsum('cd,gde->gce', jnp.asarray(cc, F32), fw, precision=HIGHEST)
    ws = jnp.einsum('cd,gde->gce', jnp.asarray(sc, F32), fw, precision=HIGHEST)
    ufn_g = ufn.reshape(d, FN_GROUPS, FN_GROUP)
    fold_c = jnp.einsum('kgc,gce->kge', ufn_g, wc, precision=HIGHEST).reshape(d, W)
    fold_s = jnp.einsum('kgc,gce->kge', ufn_g, ws, precision=HIGHEST).reshape(d, W)
    w_ab = jnp.concatenate([a, bt, jnp.zeros((d, LANES - 4 * GDN_HEADS), F32)], axis=1)
    ws_in = [u5, jnp.concatenate([k, v, q], axis=1), w_ab, z,
             jnp.concatenate([fold_c, fold_s], axis=1), jnp.concatenate([usg, vsg], axis=1), gate]
    ws_in = [w.astype(BF16) for w in ws_in]

    s5 = [_s5_tables(p['s5_lam_re'][l, dr], p['s5_lam_im'][l, dr], p['s5_log_dt'][l, dr],
                     p['s5_b_re'][l, dr], p['s5_b_im'][l, dr], p['s5_c_re'][l, dr],
                     p['s5_c_im'][l, dr]) for dr in range(2)]

    conv = p['gdn_conv'][l].astype(F32)
    cw = jnp.transpose(conv, (2, 0, 1)).reshape(conv.shape[2], 3 * W)
    pad = jnp.zeros((1, LANES - 2 * GDN_HEADS), F32)
    alog = jnp.concatenate([p['gdn_a_log'][l].astype(F32).reshape(1, -1), pad], axis=1)
    dtb = jnp.concatenate([p['gdn_dt_bias'][l].astype(F32).reshape(1, -1), pad], axis=1)

    sg_bias = jnp.repeat(p['sg_b'][l].astype(F32).T, SG_GROUP, axis=1)
    merge_consts = (p['s5_d'][l].astype(F32).reshape(1, W), p['s5_glu_w'][l].astype(BF16),
                    p['s5_glu_b'][l].astype(F32).reshape(1, W),
                    jnp.tile(p['gdn_norm'][l].astype(F32), GDN_HEADS).reshape(1, W))
    return dict(
        ws_in=ws_in, s5=s5, cw=cw, alog=alog, dtb=dtb,
        sg_w=p['sg_w'][l].astype(BF16), sg_bias=sg_bias,
        sg_lng=p['sg_ln_g'][l].astype(F32).reshape(1, W), sg_lnb=p['sg_ln_b'][l].astype(F32).reshape(1, W),
        merge=merge_consts, wbr=p['w_branch'][l].astype(BF16), wout=p['w_out'][l].astype(BF16),
        w1=p['ffn_w1'][l].astype(BF16), w2=p['ffn_w2'][l].astype(BF16),
        norm1=p['norm1'][l].astype(F32).reshape(1, d), norm2=p['norm2'][l].astype(F32).reshape(1, d))


def kernel(x, c, ctx, c_ctx, ada_w, ada_b, norm1, norm2, w_in, s5_lam_re, s5_lam_im, s5_log_dt, s5_b_re, s5_b_im, s5_c_re, s5_c_im, s5_d, s5_glu_w, s5_glu_b, fn_w, gdn_conv, gdn_a_log, gdn_dt_bias, gdn_norm, sg_ln_g, sg_ln_b, sg_w, sg_b, w_branch, w_out, ffn_w1, ffn_w2, norm_f):
    p = dict(w_in=w_in, s5_lam_re=s5_lam_re, s5_lam_im=s5_lam_im, s5_log_dt=s5_log_dt,
             s5_b_re=s5_b_re, s5_b_im=s5_b_im, s5_c_re=s5_c_re, s5_c_im=s5_c_im, s5_d=s5_d,
             s5_glu_w=s5_glu_w, s5_glu_b=s5_glu_b, fn_w=fn_w, gdn_conv=gdn_conv,
             gdn_a_log=gdn_a_log, gdn_dt_bias=gdn_dt_bias, gdn_norm=gdn_norm, sg_ln_g=sg_ln_g,
             sg_ln_b=sg_ln_b, sg_w=sg_w, sg_b=sg_b, w_branch=w_branch, w_out=w_out,
             ffn_w1=ffn_w1, ffn_w2=ffn_w2, norm1=norm1, norm2=norm2)
    nb, l_lat, d = x.shape
    l_ctx = ctx.shape[1]
    depth = ada_w.shape[0]
    ttot = l_lat + l_ctx
    assert nb % SUBLANES == 0 and d == N_BRANCH * W and l_lat % l_ctx == 0
    assert l_lat % ROW_TILE == 0 and l_ctx % ROW_TILE == 0 and ROW_TILE % SG_CHUNK == 0
    n_lat_tiles = l_lat // ROW_TILE
    n_tiles = ttot // ROW_TILE
    sg_rows = max(r for r in range(SG_CHUNK, 7 * SG_CHUNK, SG_CHUNK) if ttot % r == 0)

    x3 = jnp.concatenate([x, ctx], axis=1).astype(F32)

    ada_rows = -(-(nb + 1) // SUBLANES) * SUBLANES
    cc = jnp.zeros((ada_rows, d), F32).at[:nb].set(c.astype(F32)).at[nb].set(c_ctx.astype(F32))
    mods = _ada(cc, ada_w.astype(F32), ada_b.astype(F32))[:, :nb + 1].reshape(depth, nb + 1, 1, 6 * d)

    cos_l, sin_l = _dft_tables(l_lat)
    cos_c, sin_c = _dft_tables(l_ctx)
    sc_l = 1.0 / math.sqrt(l_lat * FN_GROUP)
    sc_c = 1.0 / math.sqrt(l_ctx * FN_GROUP)
    cos_l, nsin_l = jnp.asarray(cos_l * sc_l, BF16), jnp.asarray(-sin_l * sc_l, BF16)
    cos_c, nsin_c = jnp.asarray(cos_c * sc_c, BF16), jnp.asarray(-sin_c * sc_c, BF16)
    nf = norm_f.astype(F32).reshape(1, d)

    lane_head = np.arange(W) // GDN_HEAD_DIM
    seg = (lane_head[:, None] == lane_head[None, :]).astype(np.float32)
    seg_ones = jnp.asarray(seg, BF16)
    seg_avg = jnp.asarray(seg / GDN_HEAD_DIM, BF16)
    expand = []
    for dr in range(2):
        e = np.zeros((LANES, 2 * W), np.float32)
        for h in range(GDN_HEADS):
            e[dr * GDN_HEADS + h, :W] = lane_head == h
            e[2 * GDN_HEADS + dr * GDN_HEADS + h, W:] = lane_head == h
        expand.append(jnp.asarray(e, BF16))
    gdn_masks = [_gdn_masks(False), _gdn_masks(True)]
    bd = jnp.asarray(seg, F32)
    bd16 = jnp.asarray(np.concatenate([seg, seg], axis=1), BF16)
    src = (np.arange(nb)[None, :] * SUBLANES + np.arange(SUBLANES)[:, None]).reshape(-1)
    perm_np = np.zeros((nb * SUBLANES, nb * SUBLANES), np.float32)
    perm_np[np.arange(nb * SUBLANES), src] = 1.0
    perm, permt = jnp.asarray(perm_np, BF16), jnp.asarray(perm_np.T, BF16)

    for l in range(depth):
        k = _layer_consts(l, p)
        mod = mods[l]
        last = l == depth - 1
        u5, kvq, ab, z, afn, sg, gate = _kin(x3, mod, k['norm1'], k['ws_in'], n_lat_tiles)
        yf, yb = _s5(u5, perm, permt, [t[0] for t in k['s5']], [t[1] for t in k['s5']],
                     [t[2] for t in k['s5']], l_lat // S5_STEPS)
        yfn = _fnet(afn, cos_l, nsin_l, cos_c, nsin_c, l_lat, l_ctx)
        ysg = _sgu(sg, k['sg_w'], k['sg_bias'], k['sg_lng'], k['sg_lnb'], seg_avg, sg_rows)
        kvq_p, gb = _gdn_prep(kvq, ab, k['cw'], seg_ones, k['alog'], k['dtb'], n_lat_tiles)
        of, ob = _gdn_chunk(kvq_p, gb, expand, gdn_masks, bd, bd16, l_lat, 4)
        acts = (yf, yb, u5, of, ob, z, yfn, ysg, gate)
        consts = k['merge'] + (seg_avg, k['wbr'], k['wout'])
        x3 = _merge(x3, mod, acts, consts, n_lat_tiles, n_lat_tiles if last else n_tiles)
        x3 = _ffn(x3, mod, k['norm2'], k['w1'], k['w2'], nf, n_lat_tiles, final=last)

    return x3.astype(x.dtype)
```

```python
import functools
import math

import numpy as np
import jax
import jax.numpy as jnp
from jax import lax
from jax.experimental import pallas as pl
from jax.experimental.pallas import tpu as pltpu

F32 = jnp.float32
BF16 = jnp.bfloat16
HIGHEST = lax.Precision.HIGHEST

EPS = 1e-6
W = 256
N_BRANCH = 4
S5_GROUP = 16
S5_GROUPS = W // S5_GROUP
S5_STATE = 64
N_S5 = S5_GROUPS * S5_STATE
FN_GROUPS = 4
FN_GROUP = W // FN_GROUPS
GDN_HEADS = 4
GDN_HEAD_DIM = W // GDN_HEADS
GDN_CHUNK = 64
GDN_BASE = 8
SG_GROUPS = 4
SG_GROUP = W // SG_GROUPS
SG_CHUNK = 128
LANES = 128
SUBLANES = 8
ROW_TILE = 256
PAIR = 2
S5_STEPS = 32
S5_LANE_GROUP = 256
VMEM_LIMIT = 56 * 1024 * 1024


def _cparams(*sem):
    return pltpu.CompilerParams(dimension_semantics=sem, vmem_limit_bytes=VMEM_LIMIT)


def _resident(shape):
    nd = len(shape)
    return pl.BlockSpec(shape, lambda *_: (0,) * nd, pipeline_mode=pl.Buffered(1))


def _dot(a, b):
    return jnp.dot(a, b, preferred_element_type=F32)


def _split(x, parts):
    out = []
    for _ in range(parts - 1):
        hi = x.astype(BF16)
        out.append(hi)
        x = x - hi.astype(F32)
    out.append(x.astype(BF16))
    return out


def _dot_sel(x, sel, parts):
    acc = None
    for piece in _split(x, parts):
        t = _dot(piece, sel)
        acc = t if acc is None else acc + t
    return acc


def _gelu(x):
    return 0.5 * x * (1.0 + jnp.tanh(math.sqrt(2.0 / math.pi) * (x + 0.044715 * (x * x * x))))


def _sigmoid(x):
    return 0.5 * jnp.tanh(0.5 * x) + 0.5


def _silu(x):
    return x * _sigmoid(x)


def _modulated_norm(x, gain, scale, shift):
    y = x * lax.rsqrt(jnp.mean(x * x, axis=-1, keepdims=True) + EPS) * gain
    return y * (1.0 + scale) + shift


def _ada_kernel(c_ref, w_ref, b_ref, o_ref):
    c = c_ref[...]
    o_ref[0] = jnp.dot(_silu(c), w_ref[0], preferred_element_type=F32, precision=HIGHEST) + b_ref[0]


def _ada(cc, ada_w, ada_b):
    depth, d, n = ada_w.shape
    rows = cc.shape[0]
    return pl.pallas_call(
        _ada_kernel,
        grid=(depth, n // d),
        in_specs=[pl.BlockSpec((rows, d), lambda l, j: (0, 0)),
                  pl.BlockSpec((1, d, d), lambda l, j: (l, 0, j)),
                  pl.BlockSpec((1, 1, d), lambda l, j: (l, 0, j))],
        out_specs=pl.BlockSpec((1, rows, d), lambda l, j: (l, 0, j)),
        out_shape=jax.ShapeDtypeStruct((depth, rows, n), F32),
        compiler_params=_cparams("parallel", "parallel"),
        name="ada",
    )(cc, ada_w, ada_b.reshape(depth, 1, n))


def _tile_spec(n):
    return pl.BlockSpec((PAIR, ROW_TILE, n), lambda b, j: (b, j, 0))


def _mod_spec(nb, d, k, n_lat_tiles):
    return pl.BlockSpec((PAIR, 1, d), lambda b, j: (jnp.where(j < n_lat_tiles, b, nb // PAIR), 0, k))


def _kin_kernel(x_ref, sh_ref, sc_ref, g_ref, w5, wkvq, wab, wz, wfn, wsg,
                o5, okvq, oab, oz, ofn, osg, oh):
    subs = range(x_ref.shape[0])
    hb = [_modulated_norm(x_ref[s], g_ref[...], sc_ref[s], sh_ref[s]).astype(BF16) for s in subs]
    for s in subs:
        oh[s] = hb[s]
    for w_ref, o_ref in ((w5, o5), (wkvq, okvq), (wab, oab), (wz, oz), (wfn, ofn), (wsg, osg)):
        for s in subs:
            o_ref[s] = _dot(hb[s], w_ref[...]).astype(o_ref.dtype)


def _kin(x3, mod, gain, ws, n_lat_tiles):
    nb, ttot, d = x3.shape
    widths = [w.shape[1] for w in ws] + [d]
    dts = [F32, BF16, F32, BF16, BF16, BF16, BF16]
    return pl.pallas_call(
        _kin_kernel,
        grid=(nb // PAIR, ttot // ROW_TILE),
        in_specs=[_tile_spec(d), _mod_spec(nb, d, 0, n_lat_tiles), _mod_spec(nb, d, 1, n_lat_tiles),
                  _resident((1, d))] + [_resident(w.shape) for w in ws],
        out_specs=[_tile_spec(n) for n in widths],
        out_shape=[jax.ShapeDtypeStruct((nb, ttot, n), dt) for n, dt in zip(widths, dts)],
        compiler_params=_cparams("parallel", "parallel"),
        name="kin",
    )(x3, mod, mod, gain, *ws)


def _s5_kernel(uf_ref, ub_ref, perm_ref, permt_ref, bmf, bmb, cmf, cmb, af_ref, ab_ref,
               yf_ref, yb_ref, st_ref, bu_ref, s_ref):
    i = pl.program_id(0)
    nb, tt, _ = uf_ref.shape
    sub = SUBLANES
    n_grp, _, gw = bmf.shape
    half = gw // 2

    @pl.when(i == 0)
    def _():
        st_ref[...] = jnp.zeros_like(st_ref)

    dirs = ((uf_ref, bmf, cmf, af_ref, yf_ref, range(tt)),
            (ub_ref, bmb, cmb, ab_ref, yb_ref, range(tt - 1, -1, -1)))

    def regroup(dr):
        u_ref = dirs[dr][0]
        parts = []
        for k in range(tt // sub):
            blk = u_ref[:, k * sub:(k + 1) * sub, :].reshape(nb * sub, W).astype(BF16)
            parts.append(_dot(perm_ref[...], blk).astype(BF16))
        return jnp.concatenate(parts, axis=0)

    u_tb = [regroup(0), regroup(1)]
    y_tb = [None, None]

    def project(dr, g):
        bu_ref[dr, g] = _dot(u_tb[dr], dirs[dr][1][g])

    def recur(dr, g):
        a_ref, order = dirs[dr][3], dirs[dr][5]
        for c in range(half // LANES):
            re = slice(c * LANES, (c + 1) * LANES)
            im = slice(half + c * LANES, half + (c + 1) * LANES)
            a_re = a_ref[g, :, re]
            a_im = a_ref[g, :, im]
            s_re = st_ref[dr, g, :, re]
            s_im = st_ref[dr, g, :, im]
            for t in order:
                r = slice(t * nb, (t + 1) * nb)
                n_re = a_re * s_re - a_im * s_im + bu_ref[dr, g, r, re]
                n_im = a_re * s_im + a_im * s_re + bu_ref[dr, g, r, im]
                s_re, s_im = n_re, n_im
                s_ref[dr, g, r, re] = s_re.astype(BF16)
                s_ref[dr, g, r, im] = s_im.astype(BF16)
            st_ref[dr, g, :, re] = s_re
            st_ref[dr, g, :, im] = s_im

    def read_out(dr, g):
        t = _dot(s_ref[dr, g], dirs[dr][2][g])
        y_tb[dr] = t if y_tb[dr] is None else y_tb[dr] + t

    for dr in range(2):
        project(dr, 0)
    for g in range(n_grp):
        for dr in range(2):
            if g + 1 < n_grp:
                project(dr, g + 1)
            recur(dr, g)
        for dr in range(2):
            read_out(dr, g)
    for dr in range(2):
        y_ref = dirs[dr][4]
        for k in range(tt // sub):
            hi, lo = _split(y_tb[dr][k * sub * nb:(k + 1) * sub * nb], 2)
            y_bt = _dot(permt_ref[...], hi) + _dot(permt_ref[...], lo)
            y_ref[:, k * sub:(k + 1) * sub, :] = y_bt.reshape(nb, sub, W)


def _s5(u5, perm, permt, bmats, cmats, avecs, n_lat_tiles):
    nb, ttot, _ = u5.shape
    tt = S5_STEPS
    n_tiles = ttot // tt
    fwd = lambda i: (0, (i + n_lat_tiles) % n_tiles, 0)
    bwd = lambda i: (0, n_tiles - 1 - i, 0)
    rows = tt * nb
    n_grp, _, gw = bmats[0].shape
    return pl.pallas_call(
        _s5_kernel,
        grid=(n_tiles,),
        in_specs=[pl.BlockSpec((nb, tt, W), fwd), pl.BlockSpec((nb, tt, W), bwd),
                  _resident(perm.shape), _resident(permt.shape),
                  _resident(bmats[0].shape), _resident(bmats[1].shape),
                  _resident(cmats[0].shape), _resident(cmats[1].shape),
                  _resident(avecs[0].shape), _resident(avecs[1].shape)],
        out_specs=[pl.BlockSpec((nb, tt, W), fwd), pl.BlockSpec((nb, tt, W), bwd)],
        out_shape=[jax.ShapeDtypeStruct((nb, ttot, W), F32)] * 2,
        scratch_shapes=[pltpu.VMEM((2, n_grp, nb, gw), F32),
                        pltpu.VMEM((2, n_grp, rows, gw), F32),
                        pltpu.VMEM((2, n_grp, rows, gw), BF16)],
        compiler_params=_cparams("arbitrary"),
        name="s5",
    )(u5, u5, perm, permt, bmats[0], bmats[1], cmats[0], cmats[1], avecs[0], avecs[1])


def _fnet_kernel(a_ref, cos_ref, nsin_ref, y_ref):
    y = _dot(cos_ref[...], a_ref[:, :W]) + _dot(nsin_ref[...], a_ref[:, W:])
    y_ref[...] = y.astype(y_ref.dtype)


def _fnet_alias_kernel(a_ref, cos_ref, nsin_ref, prev_ref, y_ref):
    _fnet_kernel(a_ref, cos_ref, nsin_ref, y_ref)


def _fnet(afn, cos_l, nsin_l, cos_c, nsin_c, l_lat, l_ctx):
    nb, ttot, _ = afn.shape
    lat = pl.pallas_call(
        _fnet_kernel,
        grid=(nb,),
        in_specs=[pl.BlockSpec((None, l_lat, 2 * W), lambda b: (b, 0, 0)),
                  _resident(cos_l.shape), _resident(nsin_l.shape)],
        out_specs=pl.BlockSpec((None, l_lat, W), lambda b: (b, 0, 0)),
        out_shape=jax.ShapeDtypeStruct((nb, ttot, W), BF16),
        compiler_params=_cparams("parallel"),
        name="fnet_lat",
    )(afn, cos_l, nsin_l)
    blk = l_lat // l_ctx
    return pl.pallas_call(
        _fnet_alias_kernel,
        grid=(nb,),
        in_specs=[pl.BlockSpec((None, l_ctx, 2 * W), lambda b: (b, blk, 0)),
                  _resident(cos_c.shape), _resident(nsin_c.shape),
                  pl.BlockSpec(memory_space=pl.ANY)],
        out_specs=pl.BlockSpec((None, l_ctx, W), lambda b: (b, blk, 0)),
        out_shape=jax.ShapeDtypeStruct((nb, ttot, W), BF16),
        input_output_aliases={3: 0},
        compiler_params=_cparams("parallel"),
        name="fnet_ctx",
    )(afn, cos_c, nsin_c, lat)


def _sgu_kernel(uv_ref, w_ref, bias_ref, lng_ref, lnb_ref, avg_ref, y_ref):
    lane_group = lax.broadcasted_iota(jnp.int32, (SG_CHUNK, W), 1) // SG_GROUP
    rows = [slice(n * SG_CHUNK, (n + 1) * SG_CHUNK) for n in range(uv_ref.shape[0] // SG_CHUNK)]
    v = [_gelu(uv_ref[r, W:].astype(F32)) for r in rows]
    dv = [t - _dot_sel(t, avg_ref[...], 2) for t in v]
    var = [_dot_sel(t * t, avg_ref[...], 2) for t in dv]
    vn = [(d_ * lax.rsqrt(s_ + EPS) * lng_ref[...] + lnb_ref[...]).astype(BF16) for d_, s_ in zip(dv, var)]
    sv = [_dot(w_ref[0], t) for t in vn]
    for g in range(1, SG_GROUPS):
        sv = [jnp.where(lane_group == g, _dot(w_ref[g], t), s_) for t, s_ in zip(vn, sv)]
    for r, s_ in zip(rows, sv):
        y_ref[r, :] = (_gelu(uv_ref[r, :W].astype(F32)) * (s_ + bias_ref[...])).astype(y_ref.dtype)


def _sgu(sg, w, bias, lng, lnb, avg, rows):
    nb, ttot, _ = sg.shape
    return pl.pallas_call(
        _sgu_kernel,
        grid=(nb, ttot // rows),
        in_specs=[pl.BlockSpec((None, rows, 2 * W), lambda b, n: (b, n, 0)),
                  _resident(w.shape), _resident(bias.shape), _resident(lng.shape),
                  _resident(lnb.shape), _resident(avg.shape)],
        out_specs=pl.BlockSpec((None, rows, W), lambda b, n: (b, n, 0)),
        out_shape=jax.ShapeDtypeStruct((nb, ttot, W), BF16),
        compiler_params=_cparams("parallel", "parallel"),
        name="sgu",
    )(sg, w, bias, lng, lnb, avg)


def _gdn_prep_kernel(x_ref, xp_ref, xn_ref, ab_ref, cw_ref, ones_ref, alog_ref, dtb_ref,
                     o_ref, gb_ref, *, n_lat_tiles, n_tiles):
    j = pl.program_id(1)
    rows, n = x_ref.shape
    halo = xp_ref.shape[0]
    x = x_ref[...].astype(F32)
    has_prev = jnp.logical_and(j != 0, j != n_lat_tiles).astype(F32)
    has_next = jnp.logical_and(j != n_lat_tiles - 1, j != n_tiles - 1).astype(F32)
    row = lax.broadcasted_iota(jnp.int32, (rows, n), 0)
    x_before = xp_ref[...].astype(F32)[halo - 1:halo, :] * has_prev
    x_after = xn_ref[...].astype(F32)[0:1, :] * has_next
    xp = jnp.where(row == 0, x_before, pltpu.roll(x, 1, 0))
    xn = jnp.where(row == rows - 1, x_after, pltpu.roll(x, rows - 1, 0))
    y = _silu(xp * cw_ref[0:1, :] + x * cw_ref[1:2, :] + xn * cw_ref[2:3, :])
    k = y[:, :W]
    q = y[:, 2 * W:]
    o_ref[:, :W] = (k * lax.rsqrt(_dot_sel(k * k, ones_ref[...], 2) + EPS)).astype(o_ref.dtype)
    o_ref[:, W:2 * W] = y[:, W:2 * W].astype(o_ref.dtype)
    o_ref[:, 2 * W:] = (q * lax.rsqrt(_dot_sel(q * q, ones_ref[...], 2) + EPS)
                        * GDN_HEAD_DIM ** -0.5).astype(o_ref.dtype)
    ab = ab_ref[...]
    z = ab + dtb_ref[...]
    softplus = jnp.maximum(z, 0.0) + jnp.log(1.0 + jnp.exp(-jnp.abs(z)))
    g = -jnp.exp(alog_ref[...]) * softplus
    lane = lax.broadcasted_iota(jnp.int32, ab.shape, 1)
    gb_ref[...] = jnp.where(lane < 2 * GDN_HEADS, g, _sigmoid(ab))


def _gdn_prep(kvq, ab, cw, ones_seg, alog, dtb, n_lat_tiles):
    nb, ttot, n = kvq.shape
    n_tiles = ttot // ROW_TILE
    halo = 2 * SUBLANES
    per = ROW_TILE // halo
    last = ttot // halo - 1
    tile = lambda m: pl.BlockSpec((None, ROW_TILE, m), lambda b, j: (b, j, 0))
    return pl.pallas_call(
        functools.partial(_gdn_prep_kernel, n_lat_tiles=n_lat_tiles, n_tiles=n_tiles),
        grid=(nb, n_tiles),
        in_specs=[tile(n),
                  pl.BlockSpec((None, halo, n), lambda b, j: (b, jnp.maximum(j * per - 1, 0), 0)),
                  pl.BlockSpec((None, halo, n), lambda b, j: (b, jnp.minimum((j + 1) * per, last), 0)),
                  tile(LANES),
                  _resident(cw.shape), _resident(ones_seg.shape),
                  _resident(alog.shape), _resident(dtb.shape)],
        out_specs=[tile(n), tile(LANES)],
        out_shape=[jax.ShapeDtypeStruct((nb, ttot, n), BF16),
                   jax.ShapeDtypeStruct((nb, ttot, LANES), F32)],
        compiler_params=_cparams("parallel", "parallel"),
        name="gdn_prep",
    )(kvq, kvq, kvq, ab, cw, ones_seg, alog, dtb)


def _gdn_masks(backward):
    c, heads = GDN_CHUNK, GDN_HEADS
    i = np.arange(c)[:, None]
    j = (np.arange(W) % c)[None, :]
    incl = (i <= j) if backward else (i >= j)
    strict = (i < j) if backward else (i > j)
    j64 = np.arange(c)[None, :]
    incl64 = (i <= j64) if backward else (i >= j64)
    same = lambda n: (i // n) == (j // n)
    levels = [same(GDN_BASE)]
    n = GDN_BASE
    while n < c:
        levels.append(same(2 * n) & ~same(n))
        n *= 2
    f = lambda m: jnp.asarray(m, F32)
    return (f(np.stack([incl, strict, i == j])), jnp.asarray(incl64, BF16), f(np.stack(levels)))


def _gdn_chains(chains, bd_ref, bd16_ref):
    c, heads = GDN_CHUNK, GDN_HEADS
    each = lambda f, *cols: [f(*args) for args in zip(*cols)]

    def expand_heads(t):
        t = t.astype(BF16)
        return jnp.concatenate([t] * heads, axis=0) * bd16_ref[:, :t.shape[1]]

    kvq, gb, expand, m64, incl64, lvls, s_refs = zip(*chains)
    kvq = [t.astype(F32) for t in kvq]
    k = [t[:, :W] for t in kvq]
    v = [t[:, W:2 * W] for t in kvq]
    q = [t[:, 2 * W:] for t in kvq]
    ge = each(lambda g, e: _dot_sel(g, e[...], 2), gb, expand)
    g_l = [t[:, :W] for t in ge]
    beta_l = [t[:, W:] for t in ge]
    incl = [m[0] for m in m64]
    strict = [m[1] for m in m64]
    diag = [m[2] for m in m64]

    def cumulative(g, i64):
        acc = None
        for piece in _split(g, 3):
            t = _dot(i64[...], piece)
            acc = t if acc is None else acc + t
        return acc

    gc = each(cumulative, g_l, incl64)
    dif = each(lambda g, dg: g - jnp.sum(g * dg, axis=0, keepdims=True), gc, diag)
    g_tot = [jnp.sum(t, axis=0, keepdims=True) for t in g_l]
    kb = each(lambda a_, b_: a_ * b_, k, beta_l)
    k_st = [expand_heads(t) for t in k]
    kq = each(lambda a_, b_, st: lax.dot_general(
        jnp.concatenate([a_, b_], axis=0).astype(BF16), st, (((1,), (1,)), ((), ())),
        preferred_element_type=F32), kb, q, k_st)
    rel = each(lambda d_, i: jnp.exp(d_ * i) * i, dif, incl)
    a = each(lambda t, r, st: t[:c] * r * st, kq, rel, strict)
    qk = each(lambda t, r: t[c:] * r, kq, rel)
    pw = each(lambda t, lv: t * lv[0], a, lvls)
    t_inv = each(lambda dg, t: dg - t, diag, pw)
    pw_bd = [expand_heads(t) for t in pw]
    for _ in range(int(math.log2(GDN_BASE)) - 1):
        pw = each(lambda t, bd_: _dot(t.astype(BF16), bd_), pw, pw_bd)
        pw_bd = [expand_heads(t) for t in pw]
        t_inv = each(lambda t, bd_: t + _dot(t.astype(BF16), bd_), t_inv, pw_bd)
    for lvl in range(1, lvls[0].shape[0]):
        z_bd = each(lambda t, lv: expand_heads(t * lv[lvl]), a, lvls)
        tz = each(lambda t, z_: _dot(t.astype(BF16), z_), t_inv, z_bd)
        t_inv = each(lambda t, tz_: t - _dot(tz_.astype(BF16), expand_heads(t)), t_inv, tz)
    e_gc = [jnp.exp(t) for t in gc]
    rhs = each(lambda v_, b_, kb_, e: expand_heads(jnp.concatenate([v_ * b_, kb_ * e], axis=1)),
               v, beta_l, kb, e_gc)
    w_all = each(lambda t, r: _dot(t.astype(BF16), r), t_inv, rhs)
    s = [r[...] for r in s_refs]
    sb = [t.astype(BF16) for t in s]
    ws = each(lambda w_, q_, e, sb_: _dot(jnp.concatenate([w_[:, W:], q_ * e], axis=0).astype(BF16), sb_),
              w_all, q, e_gc, sb)
    u = each(lambda w_, t: w_[:, :W] - t[:c], w_all, ws)
    o = each(lambda t, qk_, u_: t[c:] + _dot(qk_.astype(BF16), expand_heads(u_)), ws, qk, u)
    k_dec = each(lambda k_, gt, g: k_ * jnp.exp(gt - g), k, g_tot, gc)
    upd = each(lambda kd, u_: lax.dot_general(kd.astype(BF16), u_.astype(BF16), (((0,), (0,)), ((), ())),
                                              preferred_element_type=F32), k_dec, u)
    for r, s_, gt, up in zip(s_refs, s, g_tot, upd):
        r[...] = s_ * jnp.exp(gt) + up * bd_ref[...]
    return o


def _gdn_chunk_kernel(kvqf_ref, kvqb_ref, gbf_ref, gbb_ref, ef_ref, eb_ref, m64f_ref, m64b_ref,
                      i64f_ref, i64b_ref, lvlf_ref, lvlb_ref, bd_ref, bd16_ref, of_ref, ob_ref, s_ref):
    @pl.when(pl.program_id(1) == 0)
    def _():
        s_ref[...] = jnp.zeros_like(s_ref)

    chains = []
    for bi in range(kvqf_ref.shape[0]):
        chains.append((kvqf_ref[bi], gbf_ref[bi], ef_ref, m64f_ref, i64f_ref, lvlf_ref, s_ref.at[bi, 0]))
        chains.append((kvqb_ref[bi], gbb_ref[bi], eb_ref, m64b_ref, i64b_ref, lvlb_ref, s_ref.at[bi, 1]))
    outs = _gdn_chains(chains, bd_ref, bd16_ref)
    for bi in range(kvqf_ref.shape[0]):
        of_ref[bi] = outs[2 * bi].astype(of_ref.dtype)
        ob_ref[bi] = outs[2 * bi + 1].astype(ob_ref.dtype)


def _gdn_chunk(kvq, gb, expand, masks, bd, bd16, l_lat, per_step):
    c = GDN_CHUNK
    nb, ttot, _ = kvq.shape
    n_lat = l_lat // c
    n_chunks = ttot // c
    fwd = lambda b, i: (b, (i + n_lat) % n_chunks, 0)
    bwd = lambda b, i: (b, n_chunks - 1 - i, 0)
    consts = [expand[0], expand[1], masks[0][0], masks[1][0], masks[0][1], masks[1][1],
              masks[0][2], masks[1][2], bd, bd16]
    return pl.pallas_call(
        _gdn_chunk_kernel,
        grid=(nb // per_step, n_chunks),
        in_specs=[pl.BlockSpec((per_step, c, 3 * W), fwd), pl.BlockSpec((per_step, c, 3 * W), bwd),
                  pl.BlockSpec((per_step, c, LANES), fwd), pl.BlockSpec((per_step, c, LANES), bwd)]
                 + [_resident(t.shape) for t in consts],
        out_specs=[pl.BlockSpec((per_step, c, W), fwd), pl.BlockSpec((per_step, c, W), bwd)],
        out_shape=[jax.ShapeDtypeStruct((nb, ttot, W), BF16)] * 2,
        scratch_shapes=[pltpu.VMEM((per_step, 2, W, W), F32)],
        compiler_params=_cparams("parallel", "arbitrary"),
        name="gdn_chunk",
    )(kvq, kvq, gb, gb, *consts)


def _merge_kernel(x_ref, h_ref, g1_ref, yf_ref, yb_ref, u5_ref, dskip_ref, gluw_ref,
                  glub_ref, of_ref, ob_ref, z_ref, gain_ref, avg_ref, yfn_ref, ysg_ref, wgate_ref,
                  wbr_ref, wout_ref, o_ref):
    d = x_ref.shape[2]
    subs = range(x_ref.shape[0])
    y5 = [_gelu(yf_ref[s] + yb_ref[s] + dskip_ref[...] * u5_ref[s]) for s in subs]
    glu = [_dot(t.astype(BF16), gluw_ref[...]) for t in y5]
    o = [of_ref[s].astype(F32) + ob_ref[s].astype(F32) for s in subs]
    ms = [_dot_sel(t * t, avg_ref[...], 2) for t in o]
    y5 = [t * _sigmoid(g_ + glub_ref[...]) for t, g_ in zip(y5, glu)]
    yg = [o[s] * lax.rsqrt(ms[s] + EPS) * gain_ref[...] * _silu(z_ref[s].astype(F32)) for s in subs]
    ys = [(y5[s], yfn_ref[s], yg[s], ysg_ref[s]) for s in subs]
    acc = [None for _ in subs]
    for j in range(N_BRANCH):
        for s in subs:
            th = jnp.tanh(_dot(h_ref[s], wgate_ref[:, j * d:(j + 1) * d]))
            b_half = _dot(ys[s][j].astype(BF16), wbr_ref[j])
            t = th * b_half + b_half
            acc[s] = t if acc[s] is None else acc[s] + t
    out = [_dot(t.astype(BF16), wout_ref[...]) for t in acc]
    for s in subs:
        o_ref[s] = x_ref[s] + g1_ref[s] * out[s]


def _merge(x3, mod, acts, consts, n_lat_tiles, n_tiles):
    nb, ttot, d = x3.shape
    h, yf, yb, u5, of, ob, z, yfn, ysg = acts
    dskip, gluw, glub, gain, avg, wgate, wbr, wout = consts
    return pl.pallas_call(
        _merge_kernel,
        grid=(nb // PAIR, n_tiles),
        in_specs=[_tile_spec(d), _tile_spec(d), _mod_spec(nb, d, 2, n_lat_tiles),
                  _tile_spec(W), _tile_spec(W), _tile_spec(W),
                  _resident(dskip.shape), _resident(gluw.shape), _resident(glub.shape),
                  _tile_spec(W), _tile_spec(W), _tile_spec(W),
                  _resident(gain.shape), _resident(avg.shape),
                  _tile_spec(W), _tile_spec(W), _resident(wgate.shape),
                  _resident(wbr.shape), _resident(wout.shape)],
        out_specs=_tile_spec(d),
        out_shape=jax.ShapeDtypeStruct((nb, n_tiles * ROW_TILE, d), F32),
        compiler_params=_cparams("parallel", "parallel"),
        name="merge",
    )(x3, h, mod, yf, yb, u5, dskip, gluw, glub, of, ob, z, gain, avg, yfn, ysg, wgate, wbr, wout)


def _ffn_kernel(x_ref, sh_ref, sc_ref, g2_ref, gain_ref, w1_ref, w2_ref, nf_ref, o_ref, *, final):
    dff = w2_ref.shape[0]
    subs = range(x_ref.shape[0])
    h = [_modulated_norm(x_ref[s], gain_ref[...], sc_ref[s], sh_ref[s]).astype(BF16) for s in subs]
    t = [_dot(h_, w1_ref[...]) for h_ in h]
    act = [(_silu(t_[:, :dff]) * t_[:, dff:]).astype(BF16) for t_ in t]
    out = [_dot(a_, w2_ref[...]) for a_ in act]
    for s in subs:
        y = x_ref[s] + g2_ref[s] * out[s]
        if final:
            y = y * lax.rsqrt(jnp.mean(y * y, axis=-1, keepdims=True) + EPS) * nf_ref[...]
        o_ref[s] = y


def _ffn(x3, mod, gain, w1, w2, norm_f, n_lat_tiles, final):
    nb, ttot, d = x3.shape
    return pl.pallas_call(
        functools.partial(_ffn_kernel, final=final),
        grid=(nb // PAIR, ttot // ROW_TILE),
        in_specs=[_tile_spec(d), _mod_spec(nb, d, 3, n_lat_tiles), _mod_spec(nb, d, 4, n_lat_tiles),
                  _mod_spec(nb, d, 5, n_lat_tiles),
                  _resident(gain.shape), _resident(w1.shape), _resident(w2.shape),
                  _resident(norm_f.shape)],
        out_specs=_tile_spec(d),
        out_shape=jax.ShapeDtypeStruct(x3.shape, F32),
        compiler_params=_cparams("parallel", "parallel"),
        name="ffn",
    )(x3, mod, mod, mod, gain, w1, w2, norm_f)


def _dft_tables(n):
    idx = np.arange(n, dtype=np.int64)
    ang = 2.0 * np.pi * ((idx[:, None] * idx[None, :]) % n).astype(np.float64) / n
    return np.cos(ang), np.sin(ang)


def _s5_tables(lam_re, lam_im, log_dt, b_re, b_im, c_re, c_im):
    g, p, cg = S5_GROUPS, S5_STATE, S5_GROUP
    lam = lax.complex(lam_re.astype(F32), lam_im.astype(F32))
    a_bar = jnp.exp(lam * jnp.exp(log_dt.astype(F32))[:, None])
    b_bar = ((a_bar - 1.0) / lam)[..., None] * lax.complex(b_re.astype(F32), b_im.astype(F32))
    eye = jnp.eye(g, dtype=F32)
    bm_re = jnp.einsum('gpc,gh->gchp', b_bar.real, eye).reshape(g * cg, g * p)
    bm_im = jnp.einsum('gpc,gh->gchp', b_bar.imag, eye).reshape(g * cg, g * p)
    cm_re = jnp.einsum('gcp,gh->gphc', c_re.astype(F32), eye).reshape(g * p, g * cg)
    cm_im = jnp.einsum('gcp,gh->gphc', c_im.astype(F32), eye).reshape(g * p, g * cg)
    ng, h = N_S5 // S5_LANE_GROUP, S5_LANE_GROUP
    bmat = jnp.concatenate([bm_re.reshape(W, ng, h), bm_im.reshape(W, ng, h)], axis=2)
    bmat = jnp.transpose(bmat, (1, 0, 2)).astype(BF16)
    cmat = jnp.concatenate([cm_re.reshape(ng, h, W), -cm_im.reshape(ng, h, W)], axis=1).astype(BF16)
    avec = jnp.concatenate([a_bar.real.reshape(ng, 1, h), a_bar.imag.reshape(ng, 1, h)], axis=2)
    return bmat, cmat, avec


def _layer_consts(l, p):
    d = p['w_in'].shape[1]
    w_in = p['w_in'][l]
    sizes = (W, W, W, 2 * GDN_HEADS, 2 * GDN_HEADS, W, W, W, W, W, N_BRANCH * d)
    offs = np.concatenate([[0], np.cumsum(sizes)])
    col = lambda j: w_in[:, offs[j]:offs[j + 1]]
    u5, k, v, a, bt, q, z, ufn, usg, vsg, gate = [col(j) for j in range(len(sizes))]
    cc, sc = _dft_tables(FN_GROUP)
    fw = p['fn_w'][l].astype(F32)
    wc = jnp.einsum('cd,gde->gce', jnp.asarray(cc, F32), fw, precision=HIGHEST)
    ws = jnp.einsum('cd,gde->gce', jnp.asarray(sc, F32), fw, precision=HIGHEST)
    ufn_g = ufn.reshape(d, FN_GROUPS, FN_GROUP)
    fold_c = jnp.einsum('kgc,gce->kge', ufn_g, wc, precision=HIGHEST).reshape(d, W)
    fold_s = jnp.einsum('kgc,gce->kge', ufn_g, ws, precision=HIGHEST).reshape(d, W)
    w_ab = jnp.concatenate([a, bt, jnp.zeros((d, LANES - 4 * GDN_HEADS), F32)], axis=1)
    ws_in = [u5, jnp.concatenate([k, v, q], axis=1), w_ab, z,
             jnp.concatenate([fold_c, fold_s], axis=1), jnp.concatenate([usg, vsg], axis=1)]
    ws_in = [w.astype(BF16) for w in ws_in]
    w_gate = (0.5 * gate).astype(BF16)

    s5 = [_s5_tables(p['s5_lam_re'][l, dr], p['s5_lam_im'][l, dr], p['s5_log_dt'][l, dr],
                     p['s5_b_re'][l, dr], p['s5_b_im'][l, dr], p['s5_c_re'][l, dr],
                     p['s5_c_im'][l, dr]) for dr in range(2)]

    conv = p['gdn_conv'][l].astype(F32)
    cw = jnp.transpose(conv, (2, 0, 1)).reshape(conv.shape[2], 3 * W)
    pad = jnp.zeros((1, LANES - 2 * GDN_HEADS), F32)
    alog = jnp.concatenate([p['gdn_a_log'][l].astype(F32).reshape(1, -1), pad], axis=1)
    dtb = jnp.concatenate([p['gdn_dt_bias'][l].astype(F32).reshape(1, -1), pad], axis=1)

    sg_bias = jnp.repeat(p['sg_b'][l].astype(F32).T, SG_GROUP, axis=1)
    merge_consts = (p['s5_d'][l].astype(F32).reshape(1, W), p['s5_glu_w'][l].astype(BF16),
                    p['s5_glu_b'][l].astype(F32).reshape(1, W),
                    jnp.tile(p['gdn_norm'][l].astype(F32), GDN_HEADS).reshape(1, W))
    return dict(
        ws_in=ws_in, w_gate=w_gate, s5=s5, cw=cw, alog=alog, dtb=dtb,
        sg_w=p['sg_w'][l].astype(BF16), sg_bias=sg_bias,
        sg_lng=p['sg_ln_g'][l].astype(F32).reshape(1, W), sg_lnb=p['sg_ln_b'][l].astype(F32).reshape(1, W),
        merge=merge_consts, wbr_half=(0.5 * p['w_branch'][l]).astype(BF16),
        wout=p['w_out'][l].astype(BF16),
        w1=p['ffn_w1'][l].astype(BF16), w2=p['ffn_w2'][l].astype(BF16),
        norm1=p['norm1'][l].astype(F32).reshape(1, d), norm2=p['norm2'][l].astype(F32).reshape(1, d))


def kernel(x, c, ctx, c_ctx, ada_w, ada_b, norm1, norm2, w_in, s5_lam_re, s5_lam_im, s5_log_dt, s5_b_re, s5_b_im, s5_c_re, s5_c_im, s5_d, s5_glu_w, s5_glu_b, fn_w, gdn_conv, gdn_a_log, gdn_dt_bias, gdn_norm, sg_ln_g, sg_ln_b, sg_w, sg_b, w_branch, w_out, ffn_w1, ffn_w2, norm_f):
    p = dict(w_in=w_in, s5_lam_re=s5_lam_re, s5_lam_im=s5_lam_im, s5_log_dt=s5_log_dt,
             s5_b_re=s5_b_re, s5_b_im=s5_b_im, s5_c_re=s5_c_re, s5_c_im=s5_c_im, s5_d=s5_d,
             s5_glu_w=s5_glu_w, s5_glu_b=s5_glu_b, fn_w=fn_w, gdn_conv=gdn_conv,
             gdn_a_log=gdn_a_log, gdn_dt_bias=gdn_dt_bias, gdn_norm=gdn_norm, sg_ln_g=sg_ln_g,
             sg_ln_b=sg_ln_b, sg_w=sg_w, sg_b=sg_b, w_branch=w_branch, w_out=w_out,
             ffn_w1=ffn_w1, ffn_w2=ffn_w2, norm1=norm1, norm2=norm2)
    nb, l_lat, d = x.shape
    l_ctx = ctx.shape[1]
    depth = ada_w.shape[0]
    ttot = l_lat + l_ctx
    assert nb % SUBLANES == 0 and nb % PAIR == 0 and d == N_BRANCH * W and l_lat % l_ctx == 0
    assert l_lat % ROW_TILE == 0 and l_ctx % ROW_TILE == 0 and ROW_TILE % SG_CHUNK == 0
    n_lat_tiles = l_lat // ROW_TILE
    n_tiles = ttot // ROW_TILE
    sg_rows = max(r for r in range(SG_CHUNK, 7 * SG_CHUNK, SG_CHUNK) if ttot % r == 0)

    x3 = jnp.concatenate([x, ctx], axis=1).astype(F32)

    ada_rows = -(-(nb + PAIR) // SUBLANES) * SUBLANES
    cc = jnp.zeros((ada_rows, d), F32).at[:nb].set(c.astype(F32)).at[nb:nb + PAIR].set(c_ctx.astype(F32))
    mods = _ada(cc, ada_w.astype(F32), ada_b.astype(F32))[:, :nb + PAIR].reshape(depth, nb + PAIR, 1, 6 * d)

    cos_l, sin_l = _dft_tables(l_lat)
    cos_c, sin_c = _dft_tables(l_ctx)
    sc_l = 1.0 / math.sqrt(l_lat * FN_GROUP)
    sc_c = 1.0 / math.sqrt(l_ctx * FN_GROUP)
    cos_l, nsin_l = jnp.asarray(cos_l * sc_l, BF16), jnp.asarray(-sin_l * sc_l, BF16)
    cos_c, nsin_c = jnp.asarray(cos_c * sc_c, BF16), jnp.asarray(-sin_c * sc_c, BF16)
    nf = norm_f.astype(F32).reshape(1, d)

    lane_head = np.arange(W) // GDN_HEAD_DIM
    seg = (lane_head[:, None] == lane_head[None, :]).astype(np.float32)
    seg_ones = jnp.asarray(seg, BF16)
    seg_avg = jnp.asarray(seg / GDN_HEAD_DIM, BF16)
    expand = []
    for dr in range(2):
        e = np.zeros((LANES, 2 * W), np.float32)
        for h in range(GDN_HEADS):
            e[dr * GDN_HEADS + h, :W] = lane_head == h
            e[2 * GDN_HEADS + dr * GDN_HEADS + h, W:] = lane_head == h
        expand.append(jnp.asarray(e, BF16))
    gdn_masks = [_gdn_masks(False), _gdn_masks(True)]
    bd = jnp.asarray(seg, F32)
    bd16 = jnp.asarray(np.concatenate([seg, seg], axis=1), BF16)
    src = (np.arange(nb)[None, :] * SUBLANES + np.arange(SUBLANES)[:, None]).reshape(-1)
    perm_np = np.zeros((nb * SUBLANES, nb * SUBLANES), np.float32)
    perm_np[np.arange(nb * SUBLANES), src] = 1.0
    perm, permt = jnp.asarray(perm_np, BF16), jnp.asarray(perm_np.T, BF16)

    for l in range(depth):
        k = _layer_consts(l, p)
        mod = mods[l]
        last = l == depth - 1
        u5, kvq, ab, z, afn, sg, h = _kin(x3, mod, k['norm1'], k['ws_in'], n_lat_tiles)
        yf, yb = _s5(u5, perm, permt, [t[0] for t in k['s5']], [t[1] for t in k['s5']],
                     [t[2] for t in k['s5']], l_lat // S5_STEPS)
        yfn = _fnet(afn, cos_l, nsin_l, cos_c, nsin_c, l_lat, l_ctx)
        ysg = _sgu(sg, k['sg_w'], k['sg_bias'], k['sg_lng'], k['sg_lnb'], seg_avg, sg_rows)
        kvq_p, gb = _gdn_prep(kvq, ab, k['cw'], seg_ones, k['alog'], k['dtb'], n_lat_tiles)
        of, ob = _gdn_chunk(kvq_p, gb, expand, gdn_masks, bd, bd16, l_lat, 8)
        acts = (h, yf, yb, u5, of, ob, z, yfn, ysg)
        consts = k['merge'] + (seg_avg, k['w_gate'], k['wbr_half'], k['wout'])
        x3 = _merge(x3, mod, acts, consts, n_lat_tiles, n_lat_tiles if last else n_tiles)
        x3 = _ffn(x3, mod, k['norm2'], k['w1'], k['w2'], nf, n_lat_tiles, final=last)

    return x3.astype(x.dtype)
```

```python
import functools
import math

import numpy as np
import jax
import jax.numpy as jnp
from jax import lax
from jax.experimental import pallas as pl
from jax.experimental.pallas import tpu as pltpu

F32 = jnp.float32
BF16 = jnp.bfloat16
HIGHEST = lax.Precision.HIGHEST

EPS = 1e-6
W = 256
N_BRANCH = 4
S5_GROUP = 16
S5_GROUPS = W // S5_GROUP
S5_STATE = 64
N_S5 = S5_GROUPS * S5_STATE
FN_GROUPS = 4
FN_GROUP = W // FN_GROUPS
GDN_HEADS = 4
GDN_HEAD_DIM = W // GDN_HEADS
GDN_CHUNK = 64
GDN_BASE = 8
SG_GROUPS = 4
SG_GROUP = W // SG_GROUPS
SG_CHUNK = 128
LANES = 128
SUBLANES = 8
ROW_TILE = 256
PAIR = 2
S5_STEPS = 32
S5_LANE_GROUP = 256
VMEM_LIMIT = 56 * 1024 * 1024


def _cparams(*sem):
    return pltpu.CompilerParams(dimension_semantics=sem, vmem_limit_bytes=VMEM_LIMIT)


def _resident(shape):
    nd = len(shape)
    return pl.BlockSpec(shape, lambda *_: (0,) * nd, pipeline_mode=pl.Buffered(1))


def _dot(a, b):
    return jnp.dot(a, b, preferred_element_type=F32)


def _split(x, parts):
    out = []
    for _ in range(parts - 1):
        hi = x.astype(BF16)
        out.append(hi)
        x = x - hi.astype(F32)
    out.append(x.astype(BF16))
    return out


def _dot_sel(x, sel, parts):
    acc = None
    for piece in _split(x, parts):
        t = _dot(piece, sel)
        acc = t if acc is None else acc + t
    return acc


def _gelu(x):
    return 0.5 * x * (1.0 + jnp.tanh(math.sqrt(2.0 / math.pi) * (x + 0.044715 * (x * x * x))))


def _sigmoid(x):
    return 0.5 * jnp.tanh(0.5 * x) + 0.5


def _silu(x):
    return x * _sigmoid(x)


def _modulated_norm(x, gain, scale, shift):
    y = x * lax.rsqrt(jnp.mean(x * x, axis=-1, keepdims=True) + EPS) * gain
    return y * (1.0 + scale) + shift


def _ada_kernel(c_ref, w_ref, b_ref, o_ref):
    c = c_ref[...]
    o_ref[0] = jnp.dot(_silu(c), w_ref[0], preferred_element_type=F32, precision=HIGHEST) + b_ref[0]


def _ada(cc, ada_w, ada_b):
    depth, d, n = ada_w.shape
    rows = cc.shape[0]
    return pl.pallas_call(
        _ada_kernel,
        grid=(depth, n // d),
        in_specs=[pl.BlockSpec((rows, d), lambda l, j: (0, 0)),
                  pl.BlockSpec((1, d, d), lambda l, j: (l, 0, j)),
                  pl.BlockSpec((1, 1, d), lambda l, j: (l, 0, j))],
        out_specs=pl.BlockSpec((1, rows, d), lambda l, j: (l, 0, j)),
        out_shape=jax.ShapeDtypeStruct((depth, rows, n), F32),
        compiler_params=_cparams("parallel", "parallel"),
        name="ada",
    )(cc, ada_w, ada_b.reshape(depth, 1, n))


def _tile_spec(n):
    return pl.BlockSpec((PAIR, ROW_TILE, n), lambda b, j: (b, j, 0))


def _mod_spec(nb, d, k, n_lat_tiles):
    return pl.BlockSpec((PAIR, 1, d), lambda b, j: (jnp.where(j < n_lat_tiles, b, nb // PAIR), 0, k))


def _gdn_features(kvq, before, after, cw_ref, ones_ref):
    rows, n = kvq.shape
    row = lax.broadcasted_iota(jnp.int32, (rows, n), 0)
    xp = jnp.where(row == 0, before, pltpu.roll(kvq, 1, 0))
    xn = jnp.where(row == rows - 1, after, pltpu.roll(kvq, rows - 1, 0))
    y = _silu(xp * cw_ref[0:1, :] + kvq * cw_ref[1:2, :] + xn * cw_ref[2:3, :])
    k = y[:, :W]
    q = y[:, 2 * W:]
    k = k * lax.rsqrt(_dot_sel(k * k, ones_ref[...], 2) + EPS)
    q = q * lax.rsqrt(_dot_sel(q * q, ones_ref[...], 2) + EPS) * GDN_HEAD_DIM ** -0.5
    return k, y[:, W:2 * W], q


def _gdn_gates(ab, alog_ref, dtb_ref):
    z = ab + dtb_ref[...]
    softplus = jnp.maximum(z, 0.0) + jnp.log(1.0 + jnp.exp(-jnp.abs(z)))
    g = -jnp.exp(alog_ref[...]) * softplus
    lane = lax.broadcasted_iota(jnp.int32, ab.shape, 1)
    return jnp.where(lane < 2 * GDN_HEADS, g, _sigmoid(ab))


def _kin_kernel(x_ref, xp_ref, xn_ref, sh_ref, sc_ref, g_ref, w5, wkvq, wab, wz, wfn, wsg,
                cw_ref, ones_ref, alog_ref, dtb_ref, o5, okvq, ogb, oz, ofn, osg, oh,
                *, n_lat_tiles, n_tiles):
    j = pl.program_id(1)
    subs = range(x_ref.shape[0])
    halo = xp_ref.shape[1]
    hb = [_modulated_norm(x_ref[s], g_ref[...], sc_ref[s], sh_ref[s]).astype(BF16) for s in subs]
    edge = [jnp.concatenate([xp_ref[s], xn_ref[s]], axis=0) for s in subs]
    eb = [_modulated_norm(edge[s], g_ref[...], sc_ref[s], sh_ref[s]).astype(BF16) for s in subs]
    has_prev = jnp.logical_and(j != 0, j != n_lat_tiles).astype(F32)
    has_next = jnp.logical_and(j != n_lat_tiles - 1, j != n_tiles - 1).astype(F32)
    for s in subs:
        oh[s] = hb[s]
    for w_ref, o_ref in ((w5, o5), (wz, oz), (wfn, ofn), (wsg, osg)):
        for s in subs:
            o_ref[s] = _dot(hb[s], w_ref[...]).astype(o_ref.dtype)
    kvq = [_dot(hb[s], wkvq[...]) for s in subs]
    kvq_e = [_dot(eb[s], wkvq[...]) for s in subs]
    for s in subs:
        k, v, q = _gdn_features(kvq[s], kvq_e[s][halo - 1:halo, :] * has_prev,
                                kvq_e[s][halo:halo + 1, :] * has_next, cw_ref, ones_ref)
        okvq[s, :, :W] = k.astype(okvq.dtype)
        okvq[s, :, W:2 * W] = v.astype(okvq.dtype)
        okvq[s, :, 2 * W:] = q.astype(okvq.dtype)
    for s in subs:
        ogb[s] = _gdn_gates(_dot(hb[s], wab[...]), alog_ref, dtb_ref)


def _kin(x3, mod, gain, ws, gdn_consts, n_lat_tiles):
    nb, ttot, d = x3.shape
    n_tiles = ttot // ROW_TILE
    widths = [W, 3 * W, LANES, W, 2 * W, 2 * W, d]
    dts = [F32, BF16, F32, BF16, BF16, BF16, BF16]
    per = ROW_TILE // SUBLANES
    last = ttot // SUBLANES - 1
    return pl.pallas_call(
        functools.partial(_kin_kernel, n_lat_tiles=n_lat_tiles, n_tiles=n_tiles),
        grid=(nb // PAIR, n_tiles),
        in_specs=[_tile_spec(d),
                  pl.BlockSpec((PAIR, SUBLANES, d), lambda b, j: (b, jnp.maximum(j * per - 1, 0), 0)),
                  pl.BlockSpec((PAIR, SUBLANES, d), lambda b, j: (b, jnp.minimum((j + 1) * per, last), 0)),
                  _mod_spec(nb, d, 0, n_lat_tiles), _mod_spec(nb, d, 1, n_lat_tiles),
                  _resident((1, d))] + [_resident(w.shape) for w in ws]
                 + [_resident(t.shape) for t in gdn_consts],
        out_specs=[_tile_spec(n) for n in widths],
        out_shape=[jax.ShapeDtypeStruct((nb, ttot, n), dt) for n, dt in zip(widths, dts)],
        compiler_params=_cparams("parallel", "parallel"),
        name="kin",
    )(x3, x3, x3, mod, mod, gain, *ws, *gdn_consts)


def _s5_kernel(uf_ref, ub_ref, perm_ref, permt_ref, bmf, bmb, cmf, cmb, af_ref, ab_ref,
               yf_ref, yb_ref, st_ref, bu_ref, s_ref):
    i = pl.program_id(0)
    nb, tt, _ = uf_ref.shape
    sub = SUBLANES
    n_grp, _, gw = bmf.shape
    half = gw // 2

    @pl.when(i == 0)
    def _():
        st_ref[...] = jnp.zeros_like(st_ref)

    dirs = ((uf_ref, bmf, cmf, af_ref, yf_ref, range(tt)),
            (ub_ref, bmb, cmb, ab_ref, yb_ref, range(tt - 1, -1, -1)))

    def regroup(dr):
        u_ref = dirs[dr][0]
        parts = []
        for k in range(tt // sub):
            blk = u_ref[:, k * sub:(k + 1) * sub, :].reshape(nb * sub, W).astype(BF16)
            parts.append(_dot(perm_ref[...], blk).astype(BF16))
        return jnp.concatenate(parts, axis=0)

    u_tb = [regroup(0), regroup(1)]
    y_tb = [None, None]

    def project(dr, g):
        bu_ref[dr, g] = _dot(u_tb[dr], dirs[dr][1][g])

    def recur(dr, g):
        a_ref, order = dirs[dr][3], dirs[dr][5]
        for c in range(half // LANES):
            re = slice(c * LANES, (c + 1) * LANES)
            im = slice(half + c * LANES, half + (c + 1) * LANES)
            a_re = a_ref[g, :, re]
            a_im = a_ref[g, :, im]
            s_re = st_ref[dr, g, :, re]
            s_im = st_ref[dr, g, :, im]
            for t in order:
                r = slice(t * nb, (t + 1) * nb)
                n_re = a_re * s_re - a_im * s_im + bu_ref[dr, g, r, re]
                n_im = a_re * s_im + a_im * s_re + bu_ref[dr, g, r, im]
                s_re, s_im = n_re, n_im
                s_ref[dr, g, r, re] = s_re.astype(BF16)
                s_ref[dr, g, r, im] = s_im.astype(BF16)
            st_ref[dr, g, :, re] = s_re
            st_ref[dr, g, :, im] = s_im

    def read_out(dr, g):
        t = _dot(s_ref[dr, g], dirs[dr][2][g])
        y_tb[dr] = t if y_tb[dr] is None else y_tb[dr] + t

    for dr in range(2):
        project(dr, 0)
    for g in range(n_grp):
        for dr in range(2):
            if g + 1 < n_grp:
                project(dr, g + 1)
            recur(dr, g)
        for dr in range(2):
            read_out(dr, g)
    for dr in range(2):
        y_ref = dirs[dr][4]
        for k in range(tt // sub):
            hi, lo = _split(y_tb[dr][k * sub * nb:(k + 1) * sub * nb], 2)
            y_bt = _dot(permt_ref[...], hi) + _dot(permt_ref[...], lo)
            y_ref[:, k * sub:(k + 1) * sub, :] = y_bt.reshape(nb, sub, W)


def _s5(u5, perm, permt, bmats, cmats, avecs, n_lat_tiles):
    nb, ttot, _ = u5.shape
    tt = S5_STEPS
    n_tiles = ttot // tt
    fwd = lambda i: (0, (i + n_lat_tiles) % n_tiles, 0)
    bwd = lambda i: (0, n_tiles - 1 - i, 0)
    rows = tt * nb
    n_grp, _, gw = bmats[0].shape
    return pl.pallas_call(
        _s5_kernel,
        grid=(n_tiles,),
        in_specs=[pl.BlockSpec((nb, tt, W), fwd), pl.BlockSpec((nb, tt, W), bwd),
                  _resident(perm.shape), _resident(permt.shape),
                  _resident(bmats[0].shape), _resident(bmats[1].shape),
                  _resident(cmats[0].shape), _resident(cmats[1].shape),
                  _resident(avecs[0].shape), _resident(avecs[1].shape)],
        out_specs=[pl.BlockSpec((nb, tt, W), fwd), pl.BlockSpec((nb, tt, W), bwd)],
        out_shape=[jax.ShapeDtypeStruct((nb, ttot, W), F32)] * 2,
        scratch_shapes=[pltpu.VMEM((2, n_grp, nb, gw), F32),
                        pltpu.VMEM((2, n_grp, rows, gw), F32),
                        pltpu.VMEM((2, n_grp, rows, gw), BF16)],
        compiler_params=_cparams("arbitrary"),
        name="s5",
    )(u5, u5, perm, permt, bmats[0], bmats[1], cmats[0], cmats[1], avecs[0], avecs[1])


def _fnet_kernel(a_ref, cos_ref, nsin_ref, y_ref):
    y = _dot(cos_ref[...], a_ref[:, :W]) + _dot(nsin_ref[...], a_ref[:, W:])
    y_ref[...] = y.astype(y_ref.dtype)


def _fnet_alias_kernel(a_ref, cos_ref, nsin_ref, prev_ref, y_ref):
    _fnet_kernel(a_ref, cos_ref, nsin_ref, y_ref)


def _fnet(afn, cos_l, nsin_l, cos_c, nsin_c, l_lat, l_ctx):
    nb, ttot, _ = afn.shape
    lat = pl.pallas_call(
        _fnet_kernel,
        grid=(nb,),
        in_specs=[pl.BlockSpec((None, l_lat, 2 * W), lambda b: (b, 0, 0)),
                  _resident(cos_l.shape), _resident(nsin_l.shape)],
        out_specs=pl.BlockSpec((None, l_lat, W), lambda b: (b, 0, 0)),
        out_shape=jax.ShapeDtypeStruct((nb, ttot, W), BF16),
        compiler_params=_cparams("parallel"),
        name="fnet_lat",
    )(afn, cos_l, nsin_l)
    blk = l_lat // l_ctx
    return pl.pallas_call(
        _fnet_alias_kernel,
        grid=(nb,),
        in_specs=[pl.BlockSpec((None, l_ctx, 2 * W), lambda b: (b, blk, 0)),
                  _resident(cos_c.shape), _resident(nsin_c.shape),
                  pl.BlockSpec(memory_space=pl.ANY)],
        out_specs=pl.BlockSpec((None, l_ctx, W), lambda b: (b, blk, 0)),
        out_shape=jax.ShapeDtypeStruct((nb, ttot, W), BF16),
        input_output_aliases={3: 0},
        compiler_params=_cparams("parallel"),
        name="fnet_ctx",
    )(afn, cos_c, nsin_c, lat)


def _sgu_kernel(uv_ref, w_ref, bias_ref, lng_ref, lnb_ref, avg_ref, y_ref):
    lane_group = lax.broadcasted_iota(jnp.int32, (SG_CHUNK, W), 1) // SG_GROUP
    rows = [slice(n * SG_CHUNK, (n + 1) * SG_CHUNK) for n in range(uv_ref.shape[0] // SG_CHUNK)]
    v = [_gelu(uv_ref[r, W:].astype(F32)) for r in rows]
    dv = [t - _dot_sel(t, avg_ref[...], 2) for t in v]
    var = [_dot_sel(t * t, avg_ref[...], 2) for t in dv]
    vn = [(d_ * lax.rsqrt(s_ + EPS) * lng_ref[...] + lnb_ref[...]).astype(BF16) for d_, s_ in zip(dv, var)]
    sv = [_dot(w_ref[0], t) for t in vn]
    for g in range(1, SG_GROUPS):
        sv = [jnp.where(lane_group == g, _dot(w_ref[g], t), s_) for t, s_ in zip(vn, sv)]
    for r, s_ in zip(rows, sv):
        y_ref[r, :] = (_gelu(uv_ref[r, :W].astype(F32)) * (s_ + bias_ref[...])).astype(y_ref.dtype)


def _sgu(sg, w, bias, lng, lnb, avg, rows):
    nb, ttot, _ = sg.shape
    return pl.pallas_call(
        _sgu_kernel,
        grid=(nb, ttot // rows),
        in_specs=[pl.BlockSpec((None, rows, 2 * W), lambda b, n: (b, n, 0)),
                  _resident(w.shape), _resident(bias.shape), _resident(lng.shape),
                  _resident(lnb.shape), _resident(avg.shape)],
        out_specs=pl.BlockSpec((None, rows, W), lambda b, n: (b, n, 0)),
        out_shape=jax.ShapeDtypeStruct((nb, ttot, W), BF16),
        compiler_params=_cparams("parallel", "parallel"),
        name="sgu",
    )(sg, w, bias, lng, lnb, avg)


def _gdn_masks(backward):
    c, heads = GDN_CHUNK, GDN_HEADS
    i = np.arange(c)[:, None]
    j = (np.arange(W) % c)[None, :]
    incl = (i <= j) if backward else (i >= j)
    strict = (i < j) if backward else (i > j)
    j64 = np.arange(c)[None, :]
    incl64 = (i <= j64) if backward else (i >= j64)
    same = lambda n: (i // n) == (j // n)
    levels = [same(GDN_BASE)]
    n = GDN_BASE
    while n < c:
        levels.append(same(2 * n) & ~same(n))
        n *= 2
    f = lambda m: jnp.asarray(m, F32)
    return (f(np.stack([incl, strict, i == j])), jnp.asarray(incl64, BF16), f(np.stack(levels)))


def _gdn_chains(chains, bd_ref, bd16_ref):
    c, heads = GDN_CHUNK, GDN_HEADS
    each = lambda f, *cols: [f(*args) for args in zip(*cols)]

    def expand_heads(t):
        t = t.astype(BF16)
        return jnp.concatenate([t] * heads, axis=0) * bd16_ref[:, :t.shape[1]]

    kvq, gb, expand, m64, incl64, lvls, s_refs = zip(*chains)
    kvq = [t.astype(F32) for t in kvq]
    k = [t[:, :W] for t in kvq]
    v = [t[:, W:2 * W] for t in kvq]
    q = [t[:, 2 * W:] for t in kvq]
    ge = each(lambda g, e: _dot_sel(g, e[...], 2), gb, expand)
    g_l = [t[:, :W] for t in ge]
    beta_l = [t[:, W:] for t in ge]
    incl = [m[0] for m in m64]
    strict = [m[1] for m in m64]
    diag = [m[2] for m in m64]

    def cumulative(g, i64):
        acc = None
        for piece in _split(g, 3):
            t = _dot(i64[...], piece)
            acc = t if acc is None else acc + t
        return acc

    gc = each(cumulative, g_l, incl64)
    dif = each(lambda g, dg: g - jnp.sum(g * dg, axis=0, keepdims=True), gc, diag)
    g_tot = [jnp.sum(t, axis=0, keepdims=True) for t in g_l]
    kb = each(lambda a_, b_: a_ * b_, k, beta_l)
    k_st = [expand_heads(t) for t in k]
    kq = each(lambda a_, b_, st: lax.dot_general(
        jnp.concatenate([a_, b_], axis=0).astype(BF16), st, (((1,), (1,)), ((), ())),
        preferred_element_type=F32), kb, q, k_st)
    rel = each(lambda d_, i: jnp.exp(d_ * i) * i, dif, incl)
    a = each(lambda t, r, st: t[:c] * r * st, kq, rel, strict)
    qk = each(lambda t, r: t[c:] * r, kq, rel)
    pw = each(lambda t, lv: t * lv[0], a, lvls)
    t_inv = each(lambda dg, t: dg - t, diag, pw)
    pw_bd = [expand_heads(t) for t in pw]
    for _ in range(int(math.log2(GDN_BASE)) - 1):
        pw = each(lambda t, bd_: _dot(t.astype(BF16), bd_), pw, pw_bd)
        pw_bd = [expand_heads(t) for t in pw]
        t_inv = each(lambda t, bd_: t + _dot(t.astype(BF16), bd_), t_inv, pw_bd)
    for lvl in range(1, lvls[0].shape[0]):
        z_bd = each(lambda t, lv: expand_heads(t * lv[lvl]), a, lvls)
        tz = each(lambda t, z_: _dot(t.astype(BF16), z_), t_inv, z_bd)
        t_inv = each(lambda t, tz_: t - _dot(tz_.astype(BF16), expand_heads(t)), t_inv, tz)
    e_gc = [jnp.exp(t) for t in gc]
    rhs = each(lambda v_, b_, kb_, e: expand_heads(jnp.concatenate([v_ * b_, kb_ * e], axis=1)),
               v, beta_l, kb, e_gc)
    w_all = each(lambda t, r: _dot(t.astype(BF16), r), t_inv, rhs)
    s = [r[...] for r in s_refs]
    sb = [t.astype(BF16) for t in s]
    ws = each(lambda w_, q_, e, sb_: _dot(jnp.concatenate([w_[:, W:], q_ * e], axis=0).astype(BF16), sb_),
              w_all, q, e_gc, sb)
    u = each(lambda w_, t: w_[:, :W] - t[:c], w_all, ws)
    o = each(lambda t, qk_, u_: t[c:] + _dot(qk_.astype(BF16), expand_heads(u_)), ws, qk, u)
    k_dec = each(lambda k_, gt, g: k_ * jnp.exp(gt - g), k, g_tot, gc)
    upd = each(lambda kd, u_: lax.dot_general(kd.astype(BF16), u_.astype(BF16), (((0,), (0,)), ((), ())),
                                              preferred_element_type=F32), k_dec, u)
    for r, s_, gt, up in zip(s_refs, s, g_tot, upd):
        r[...] = s_ * jnp.exp(gt) + up * bd_ref[...]
    return o


def _gdn_chunk_kernel(kvqf_ref, kvqb_ref, gbf_ref, gbb_ref, ef_ref, eb_ref, m64f_ref, m64b_ref,
                      i64f_ref, i64b_ref, lvlf_ref, lvlb_ref, bd_ref, bd16_ref, of_ref, ob_ref, s_ref):
    @pl.when(pl.program_id(1) == 0)
    def _():
        s_ref[...] = jnp.zeros_like(s_ref)

    chains = []
    for bi in range(kvqf_ref.shape[0]):
        chains.append((kvqf_ref[bi], gbf_ref[bi], ef_ref, m64f_ref, i64f_ref, lvlf_ref, s_ref.at[bi, 0]))
        chains.append((kvqb_ref[bi], gbb_ref[bi], eb_ref, m64b_ref, i64b_ref, lvlb_ref, s_ref.at[bi, 1]))
    outs = _gdn_chains(chains, bd_ref, bd16_ref)
    for bi in range(kvqf_ref.shape[0]):
        of_ref[bi] = outs[2 * bi].astype(of_ref.dtype)
        ob_ref[bi] = outs[2 * bi + 1].astype(ob_ref.dtype)


def _gdn_chunk(kvq, gb, expand, masks, bd, bd16, l_lat, per_step):
    c = GDN_CHUNK
    nb, ttot, _ = kvq.shape
    n_lat = l_lat // c
    n_chunks = ttot // c
    fwd = lambda b, i: (b, (i + n_lat) % n_chunks, 0)
    bwd = lambda b, i: (b, n_chunks - 1 - i, 0)
    consts = [expand[0], expand[1], masks[0][0], masks[1][0], masks[0][1], masks[1][1],
              masks[0][2], masks[1][2], bd, bd16]
    return pl.pallas_call(
        _gdn_chunk_kernel,
        grid=(nb // per_step, n_chunks),
        in_specs=[pl.BlockSpec((per_step, c, 3 * W), fwd), pl.BlockSpec((per_step, c, 3 * W), bwd),
                  pl.BlockSpec((per_step, c, LANES), fwd), pl.BlockSpec((per_step, c, LANES), bwd)]
                 + [_resident(t.shape) for t in consts],
        out_specs=[pl.BlockSpec((per_step, c, W), fwd), pl.BlockSpec((per_step, c, W), bwd)],
        out_shape=[jax.ShapeDtypeStruct((nb, ttot, W), BF16)] * 2,
        scratch_shapes=[pltpu.VMEM((per_step, 2, W, W), F32)],
        compiler_params=_cparams("parallel", "arbitrary"),
        name="gdn_chunk",
    )(kvq, kvq, gb, gb, *consts)


def _merge_kernel(x_ref, h_ref, g1_ref, yf_ref, yb_ref, u5_ref, dskip_ref, gluw_ref,
                  glub_ref, of_ref, ob_ref, z_ref, gain_ref, avg_ref, yfn_ref, ysg_ref, wgate_ref,
                  wbr_ref, wout_ref, o_ref):
    d = x_ref.shape[2]
    subs = range(x_ref.shape[0])
    y5 = [_gelu(yf_ref[s] + yb_ref[s] + dskip_ref[...] * u5_ref[s]) for s in subs]
    glu = [_dot(t.astype(BF16), gluw_ref[...]) for t in y5]
    o = [of_ref[s].astype(F32) + ob_ref[s].astype(F32) for s in subs]
    ms = [_dot_sel(t * t, avg_ref[...], 2) for t in o]
    y5 = [t * _sigmoid(g_ + glub_ref[...]) for t, g_ in zip(y5, glu)]
    yg = [o[s] * lax.rsqrt(ms[s] + EPS) * gain_ref[...] * _silu(z_ref[s].astype(F32)) for s in subs]
    ys = [(y5[s], yfn_ref[s], yg[s], ysg_ref[s]) for s in subs]
    acc = [None for _ in subs]
    for j in range(N_BRANCH):
        for s in subs:
            th = jnp.tanh(_dot(h_ref[s], wgate_ref[:, j * d:(j + 1) * d]))
            b_half = _dot(ys[s][j].astype(BF16), wbr_ref[j])
            t = th * b_half + b_half
            acc[s] = t if acc[s] is None else acc[s] + t
    out = [_dot(t.astype(BF16), wout_ref[...]) for t in acc]
    for s in subs:
        o_ref[s] = x_ref[s] + g1_ref[s] * out[s]


def _merge(x3, mod, acts, consts, n_lat_tiles, n_tiles):
    nb, ttot, d = x3.shape
    h, yf, yb, u5, of, ob, z, yfn, ysg = acts
    dskip, gluw, glub, gain, avg, wgate, wbr, wout = consts
    return pl.pallas_call(
        _merge_kernel,
        grid=(nb // PAIR, n_tiles),
        in_specs=[_tile_spec(d), _tile_spec(d), _mod_spec(nb, d, 2, n_lat_tiles),
                  _tile_spec(W), _tile_spec(W), _tile_spec(W),
                  _resident(dskip.shape), _resident(gluw.shape), _resident(glub.shape),
                  _tile_spec(W), _tile_spec(W), _tile_spec(W),
                  _resident(gain.shape), _resident(avg.shape),
                  _tile_spec(W), _tile_spec(W), _resident(wgate.shape),
                  _resident(wbr.shape), _resident(wout.shape)],
        out_specs=_tile_spec(d),
        out_shape=jax.ShapeDtypeStruct((nb, n_tiles * ROW_TILE, d), F32),
        compiler_params=_cparams("parallel", "parallel"),
        name="merge",
    )(x3, h, mod, yf, yb, u5, dskip, gluw, glub, of, ob, z, gain, avg, yfn, ysg, wgate, wbr, wout)


def _ffn_kernel(x_ref, sh_ref, sc_ref, g2_ref, gain_ref, w1_ref, w2_ref, nf_ref, o_ref, *, final):
    dff = w2_ref.shape[0]
    subs = range(x_ref.shape[0])
    h = [_modulated_norm(x_ref[s], gain_ref[...], sc_ref[s], sh_ref[s]).astype(BF16) for s in subs]
    t = [_dot(h_, w1_ref[...]) for h_ in h]
    act = [(_silu(t_[:, :dff]) * t_[:, dff:]).astype(BF16) for t_ in t]
    out = [_dot(a_, w2_ref[...]) for a_ in act]
    for s in subs:
        y = x_ref[s] + g2_ref[s] * out[s]
        if final:
            y = y * lax.rsqrt(jnp.mean(y * y, axis=-1, keepdims=True) + EPS) * nf_ref[...]
        o_ref[s] = y


def _ffn(x3, mod, gain, w1, w2, norm_f, n_lat_tiles, final):
    nb, ttot, d = x3.shape
    return pl.pallas_call(
        functools.partial(_ffn_kernel, final=final),
        grid=(nb // PAIR, ttot // ROW_TILE),
        in_specs=[_tile_spec(d), _mod_spec(nb, d, 3, n_lat_tiles), _mod_spec(nb, d, 4, n_lat_tiles),
                  _mod_spec(nb, d, 5, n_lat_tiles),
                  _resident(gain.shape), _resident(w1.shape), _resident(w2.shape),
                  _resident(norm_f.shape)],
        out_specs=_tile_spec(d),
        out_shape=jax.ShapeDtypeStruct(x3.shape, F32),
        compiler_params=_cparams("parallel", "parallel"),
        name="ffn",
    )(x3, mod, mod, mod, gain, w1, w2, norm_f)


def _dft_tables(n):
    idx = np.arange(n, dtype=np.int64)
    ang = 2.0 * np.pi * ((idx[:, None] * idx[None, :]) % n).astype(np.float64) / n
    return np.cos(ang), np.sin(ang)


def _s5_tables(lam_re, lam_im, log_dt, b_re, b_im, c_re, c_im):
    g, p, cg = S5_GROUPS, S5_STATE, S5_GROUP
    lam = lax.complex(lam_re.astype(F32), lam_im.astype(F32))
    a_bar = jnp.exp(lam * jnp.exp(log_dt.astype(F32))[:, None])
    b_bar = ((a_bar - 1.0) / lam)[..., None] * lax.complex(b_re.astype(F32), b_im.astype(F32))
    eye = jnp.eye(g, dtype=F32)
    bm_re = jnp.einsum('gpc,gh->gchp', b_bar.real, eye).reshape(g * cg, g * p)
    bm_im = jnp.einsum('gpc,gh->gchp', b_bar.imag, eye).reshape(g * cg, g * p)
    cm_re = jnp.einsum('gcp,gh->gphc', c_re.astype(F32), eye).reshape(g * p, g * cg)
    cm_im = jnp.einsum('gcp,gh->gphc', c_im.astype(F32), eye).reshape(g * p, g * cg)
    ng, h = N_S5 // S5_LANE_GROUP, S5_LANE_GROUP
    bmat = jnp.concatenate([bm_re.reshape(W, ng, h), bm_im.reshape(W, ng, h)], axis=2)
    bmat = jnp.transpose(bmat, (1, 0, 2)).astype(BF16)
    cmat = jnp.concatenate([cm_re.reshape(ng, h, W), -cm_im.reshape(ng, h, W)], axis=1).astype(BF16)
    avec = jnp.concatenate([a_bar.real.reshape(ng, 1, h), a_bar.imag.reshape(ng, 1, h)], axis=2)
    return bmat, cmat, avec


def _layer_consts(l, p):
    d = p['w_in'].shape[1]
    w_in = p['w_in'][l]
    sizes = (W, W, W, 2 * GDN_HEADS, 2 * GDN_HEADS, W, W, W, W, W, N_BRANCH * d)
    offs = np.concatenate([[0], np.cumsum(sizes)])
    col = lambda j: w_in[:, offs[j]:offs[j + 1]]
    u5, k, v, a, bt, q, z, ufn, usg, vsg, gate = [col(j) for j in range(len(sizes))]
    cc, sc = _dft_tables(FN_GROUP)
    fw = p['fn_w'][l].astype(F32)
    wc = jnp.einsum('cd,gde->gce', jnp.asarray(cc, F32), fw, precision=HIGHEST)
    ws = jnp.einsum('cd,gde->gce', jnp.asarray(sc, F32), fw, precision=HIGHEST)
    ufn_g = ufn.reshape(d, FN_GROUPS, FN_GROUP)
    fold_c = jnp.einsum('kgc,gce->kge', ufn_g, wc, precision=HIGHEST).reshape(d, W)
    fold_s = jnp.einsum('kgc,gce->kge', ufn_g, ws, precision=HIGHEST).reshape(d, W)
    w_ab = jnp.concatenate([a, bt, jnp.zeros((d, LANES - 4 * GDN_HEADS), F32)], axis=1)
    ws_in = [u5, jnp.concatenate([k, v, q], axis=1), w_ab, z,
             jnp.concatenate([fold_c, fold_s], axis=1), jnp.concatenate([usg, vsg], axis=1)]
    ws_in = [w.astype(BF16) for w in ws_in]
    w_gate = (0.5 * gate).astype(BF16)

    s5 = [_s5_tables(p['s5_lam_re'][l, dr], p['s5_lam_im'][l, dr], p['s5_log_dt'][l, dr],
                     p['s5_b_re'][l, dr], p['s5_b_im'][l, dr], p['s5_c_re'][l, dr],
                     p['s5_c_im'][l, dr]) for dr in range(2)]

    conv = p['gdn_conv'][l].astype(F32)
    cw = jnp.transpose(conv, (2, 0, 1)).reshape(conv.shape[2], 3 * W)
    pad = jnp.zeros((1, LANES - 2 * GDN_HEADS), F32)
    alog = jnp.concatenate([p['gdn_a_log'][l].astype(F32).reshape(1, -1), pad], axis=1)
    dtb = jnp.concatenate([p['gdn_dt_bias'][l].astype(F32).reshape(1, -1), pad], axis=1)

    sg_bias = jnp.repeat(p['sg_b'][l].astype(F32).T, SG_GROUP, axis=1)
    merge_consts = (p['s5_d'][l].astype(F32).reshape(1, W), p['s5_glu_w'][l].astype(BF16),
                    p['s5_glu_b'][l].astype(F32).reshape(1, W),
                    jnp.tile(p['gdn_norm'][l].astype(F32), GDN_HEADS).reshape(1, W))
    return dict(
        ws_in=ws_in, w_gate=w_gate, s5=s5, cw=cw, alog=alog, dtb=dtb,
        sg_w=p['sg_w'][l].astype(BF16), sg_bias=sg_bias,
        sg_lng=p['sg_ln_g'][l].astype(F32).reshape(1, W), sg_lnb=p['sg_ln_b'][l].astype(F32).reshape(1, W),
        merge=merge_consts, wbr_half=(0.5 * p['w_branch'][l]).astype(BF16),
        wout=p['w_out'][l].astype(BF16),
        w1=p['ffn_w1'][l].astype(BF16), w2=p['ffn_w2'][l].astype(BF16),
        norm1=p['norm1'][l].astype(F32).reshape(1, d), norm2=p['norm2'][l].astype(F32).reshape(1, d))


def kernel(x, c, ctx, c_ctx, ada_w, ada_b, norm1, norm2, w_in, s5_lam_re, s5_lam_im, s5_log_dt, s5_b_re, s5_b_im, s5_c_re, s5_c_im, s5_d, s5_glu_w, s5_glu_b, fn_w, gdn_conv, gdn_a_log, gdn_dt_bias, gdn_norm, sg_ln_g, sg_ln_b, sg_w, sg_b, w_branch, w_out, ffn_w1, ffn_w2, norm_f):
    p = dict(w_in=w_in, s5_lam_re=s5_lam_re, s5_lam_im=s5_lam_im, s5_log_dt=s5_log_dt,
             s5_b_re=s5_b_re, s5_b_im=s5_b_im, s5_c_re=s5_c_re, s5_c_im=s5_c_im, s5_d=s5_d,
             s5_glu_w=s5_glu_w, s5_glu_b=s5_glu_b, fn_w=fn_w, gdn_conv=gdn_conv,
             gdn_a_log=gdn_a_log, gdn_dt_bias=gdn_dt_bias, gdn_norm=gdn_norm, sg_ln_g=sg_ln_g,
             sg_ln_b=sg_ln_b, sg_w=sg_w, sg_b=sg_b, w_branch=w_branch, w_out=w_out,
             ffn_w1=ffn_w1, ffn_w2=ffn_w2, norm1=norm1, norm2=norm2)
    nb, l_lat, d = x.shape
    l_ctx = ctx.shape[1]
    depth = ada_w.shape[0]
    ttot = l_lat + l_ctx
    assert nb % SUBLANES == 0 and nb % PAIR == 0 and d == N_BRANCH * W and l_lat % l_ctx == 0
    assert l_lat % ROW_TILE == 0 and l_ctx % ROW_TILE == 0 and ROW_TILE % SG_CHUNK == 0
    n_lat_tiles = l_lat // ROW_TILE
    n_tiles = ttot // ROW_TILE
    sg_rows = max(r for r in range(SG_CHUNK, 7 * SG_CHUNK, SG_CHUNK) if ttot % r == 0)

    x3 = jnp.concatenate([x, ctx], axis=1).astype(F32)

    ada_rows = -(-(nb + PAIR) // SUBLANES) * SUBLANES
    cc = jnp.zeros((ada_rows, d), F32).at[:nb].set(c.astype(F32)).at[nb:nb + PAIR].set(c_ctx.astype(F32))
    mods = _ada(cc, ada_w.astype(F32), ada_b.astype(F32))[:, :nb + PAIR].reshape(depth, nb + PAIR, 1, 6 * d)

    cos_l, sin_l = _dft_tables(l_lat)
    cos_c, sin_c = _dft_tables(l_ctx)
    sc_l = 1.0 / math.sqrt(l_lat * FN_GROUP)
    sc_c = 1.0 / math.sqrt(l_ctx * FN_GROUP)
    cos_l, nsin_l = jnp.asarray(cos_l * sc_l, BF16), jnp.asarray(-sin_l * sc_l, BF16)
    cos_c, nsin_c = jnp.asarray(cos_c * sc_c, BF16), jnp.asarray(-sin_c * sc_c, BF16)
    nf = norm_f.astype(F32).reshape(1, d)

    lane_head = np.arange(W) // GDN_HEAD_DIM
    seg = (lane_head[:, None] == lane_head[None, :]).astype(np.float32)
    seg_ones = jnp.asarray(seg, BF16)
    seg_avg = jnp.asarray(seg / GDN_HEAD_DIM, BF16)
    expand = []
    for dr in range(2):
        e = np.zeros((LANES, 2 * W), np.float32)
        for h in range(GDN_HEADS):
            e[dr * GDN_HEADS + h, :W] = lane_head == h
            e[2 * GDN_HEADS + dr * GDN_HEADS + h, W:] = lane_head == h
        expand.append(jnp.asarray(e, BF16))
    gdn_masks = [_gdn_masks(False), _gdn_masks(True)]
    bd = jnp.asarray(seg, F32)
    bd16 = jnp.asarray(np.concatenate([seg, seg], axis=1), BF16)
    src = (np.arange(nb)[None, :] * SUBLANES + np.arange(SUBLANES)[:, None]).reshape(-1)
    perm_np = np.zeros((nb * SUBLANES, nb * SUBLANES), np.float32)
    perm_np[np.arange(nb * SUBLANES), src] = 1.0
    perm, permt = jnp.asarray(perm_np, BF16), jnp.asarray(perm_np.T, BF16)

    for l in range(depth):
        k = _layer_consts(l, p)
        mod = mods[l]
        last = l == depth - 1
        u5, kvq_p, gb, z, afn, sg, h = _kin(x3, mod, k['norm1'], k['ws_in'],
                                            (k['cw'], seg_ones, k['alog'], k['dtb']), n_lat_tiles)
        yf, yb = _s5(u5, perm, permt, [t[0] for t in k['s5']], [t[1] for t in k['s5']],
                     [t[2] for t in k['s5']], l_lat // S5_STEPS)
        yfn = _fnet(afn, cos_l, nsin_l, cos_c, nsin_c, l_lat, l_ctx)
        ysg = _sgu(sg, k['sg_w'], k['sg_bias'], k['sg_lng'], k['sg_lnb'], seg_avg, sg_rows)
        of, ob = _gdn_chunk(kvq_p, gb, expand, gdn_masks, bd, bd16, l_lat, 8)
        acts = (h, yf, yb, u5, of, ob, z, yfn, ysg)
        consts = k['merge'] + (seg_avg, k['w_gate'], k['wbr_half'], k['wout'])
        x3 = _merge(x3, mod, acts, consts, n_lat_tiles, n_lat_tiles if last else n_tiles)
        x3 = _ffn(x3, mod, k['norm2'], k['w1'], k['w2'], nf, n_lat_tiles, final=last)

    return x3.astype(x.dtype)
```

```python
import functools
import math

import numpy as np
import jax
import jax.numpy as jnp
from jax import lax
from jax.experimental import pallas as pl
from jax.experimental.pallas import tpu as pltpu

F32 = jnp.float32
BF16 = jnp.bfloat16
HIGHEST = lax.Precision.HIGHEST

EPS = 1e-6
W = 256
N_BRANCH = 4
S5_GROUP = 16
S5_GROUPS = W // S5_GROUP
S5_STATE = 64
N_S5 = S5_GROUPS * S5_STATE
FN_GROUPS = 4
FN_GROUP = W // FN_GROUPS
GDN_HEADS = 4
GDN_HEAD_DIM = W // GDN_HEADS
GDN_CHUNK = 64
GDN_BASE = 8
SG_GROUPS = 4
SG_GROUP = W // SG_GROUPS
SG_CHUNK = 128
LANES = 128
SUBLANES = 8
ROW_TILE = 256
PAIR = 2
KIN_GROUP = 4
S5_STEPS = 32
S5_LANE_GROUP = 256
VMEM_LIMIT = 56 * 1024 * 1024


def _cparams(*sem):
    return pltpu.CompilerParams(dimension_semantics=sem, vmem_limit_bytes=VMEM_LIMIT)


def _resident(shape):
    nd = len(shape)
    return pl.BlockSpec(shape, lambda *_: (0,) * nd, pipeline_mode=pl.Buffered(1))


def _dot(a, b):
    return jnp.dot(a, b, preferred_element_type=F32)


def _split(x, parts):
    out = []
    for _ in range(parts - 1):
        hi = x.astype(BF16)
        out.append(hi)
        x = x - hi.astype(F32)
    out.append(x.astype(BF16))
    return out


def _dot_sel(x, sel, parts):
    acc = None
    for piece in _split(x, parts):
        t = _dot(piece, sel)
        acc = t if acc is None else acc + t
    return acc


def _gelu(x):
    return 0.5 * x * (1.0 + jnp.tanh(math.sqrt(2.0 / math.pi) * (x + 0.044715 * (x * x * x))))


def _sigmoid(x):
    return 0.5 * jnp.tanh(0.5 * x) + 0.5


def _silu(x):
    return x * _sigmoid(x)


def _modulated_norm(x, gain, scale, shift):
    y = x * lax.rsqrt(jnp.mean(x * x, axis=-1, keepdims=True) + EPS) * gain
    return y * (1.0 + scale) + shift


def _ada_kernel(c_ref, w_ref, b_ref, o_ref):
    c = c_ref[...]
    o_ref[0] = jnp.dot(_silu(c), w_ref[0], preferred_element_type=F32, precision=HIGHEST) + b_ref[0]


def _ada(cc, ada_w, ada_b):
    depth, d, n = ada_w.shape
    rows = cc.shape[0]
    return pl.pallas_call(
        _ada_kernel,
        grid=(depth, n // d),
        in_specs=[pl.BlockSpec((rows, d), lambda l, j: (0, 0)),
                  pl.BlockSpec((1, d, d), lambda l, j: (l, 0, j)),
                  pl.BlockSpec((1, 1, d), lambda l, j: (l, 0, j))],
        out_specs=pl.BlockSpec((1, rows, d), lambda l, j: (l, 0, j)),
        out_shape=jax.ShapeDtypeStruct((depth, rows, n), F32),
        compiler_params=_cparams("parallel", "parallel"),
        name="ada",
    )(cc, ada_w, ada_b.reshape(depth, 1, n))


def _tile_spec(n, group=PAIR):
    return pl.BlockSpec((group, ROW_TILE, n), lambda b, j: (b, j, 0))


def _mod_spec(nb, d, k, n_lat_tiles, group=PAIR):
    return pl.BlockSpec((group, 1, d), lambda b, j: (jnp.where(j < n_lat_tiles, b, nb // group), 0, k))


def _gdn_features(kvq, before, after, cw_ref, ones_ref):
    rows, n = kvq.shape
    row = lax.broadcasted_iota(jnp.int32, (rows, n), 0)
    xp = jnp.where(row == 0, before, pltpu.roll(kvq, 1, 0))
    xn = jnp.where(row == rows - 1, after, pltpu.roll(kvq, rows - 1, 0))
    y = _silu(xp * cw_ref[0:1, :] + kvq * cw_ref[1:2, :] + xn * cw_ref[2:3, :])
    k = y[:, :W]
    q = y[:, 2 * W:]
    k = k * lax.rsqrt(_dot_sel(k * k, ones_ref[...], 2) + EPS)
    q = q * lax.rsqrt(_dot_sel(q * q, ones_ref[...], 2) + EPS) * GDN_HEAD_DIM ** -0.5
    return k, y[:, W:2 * W], q


def _gdn_gates(ab, alog_ref, dtb_ref):
    z = ab + dtb_ref[...]
    softplus = jnp.maximum(z, 0.0) + jnp.log(1.0 + jnp.exp(-jnp.abs(z)))
    g = -jnp.exp(alog_ref[...]) * softplus
    lane = lax.broadcasted_iota(jnp.int32, ab.shape, 1)
    return jnp.where(lane < 2 * GDN_HEADS, g, _sigmoid(ab))


def _kin_kernel(x_ref, xp_ref, xn_ref, sh_ref, sc_ref, g_ref, w5, wkvq, wab, wz, wfn, wsg,
                cw_ref, ones_ref, alog_ref, dtb_ref, o5, okvq, ogb, oz, ofn, osg, oh,
                *, n_lat_tiles, n_tiles):
    j = pl.program_id(1)
    subs = range(x_ref.shape[0])
    halo = xp_ref.shape[1]
    hb = [_modulated_norm(x_ref[s], g_ref[...], sc_ref[s], sh_ref[s]).astype(BF16) for s in subs]
    edge = [jnp.concatenate([xp_ref[s], xn_ref[s]], axis=0) for s in subs]
    eb = [_modulated_norm(edge[s], g_ref[...], sc_ref[s], sh_ref[s]).astype(BF16) for s in subs]
    has_prev = jnp.logical_and(j != 0, j != n_lat_tiles).astype(F32)
    has_next = jnp.logical_and(j != n_lat_tiles - 1, j != n_tiles - 1).astype(F32)
    for s in subs:
        oh[s] = hb[s]
    for w_ref, o_ref in ((w5, o5), (wz, oz), (wfn, ofn), (wsg, osg)):
        for s in subs:
            o_ref[s] = _dot(hb[s], w_ref[...]).astype(o_ref.dtype)
    rows = x_ref.shape[1]
    kvq = [_dot(jnp.concatenate([hb[s], eb[s]], axis=0), wkvq[...]) for s in subs]
    for s in subs:
        k, v, q = _gdn_features(kvq[s][:rows], kvq[s][rows + halo - 1:rows + halo, :] * has_prev,
                                kvq[s][rows + halo:rows + halo + 1, :] * has_next, cw_ref, ones_ref)
        okvq[s, :, :W] = k.astype(okvq.dtype)
        okvq[s, :, W:2 * W] = v.astype(okvq.dtype)
        okvq[s, :, 2 * W:] = q.astype(okvq.dtype)
    for s in subs:
        ogb[s] = _gdn_gates(_dot(hb[s], wab[...]), alog_ref, dtb_ref)


def _kin(x3, mod, gain, ws, gdn_consts, n_lat_tiles):
    nb, ttot, d = x3.shape
    n_tiles = ttot // ROW_TILE
    widths = [W, 3 * W, LANES, W, 2 * W, 2 * W, d]
    dts = [F32, BF16, F32, BF16, BF16, BF16, BF16]
    per = ROW_TILE // SUBLANES
    last = ttot // SUBLANES - 1
    grp = KIN_GROUP
    return pl.pallas_call(
        functools.partial(_kin_kernel, n_lat_tiles=n_lat_tiles, n_tiles=n_tiles),
        grid=(nb // grp, n_tiles),
        in_specs=[_tile_spec(d, grp),
                  pl.BlockSpec((grp, SUBLANES, d), lambda b, j: (b, jnp.maximum(j * per - 1, 0), 0)),
                  pl.BlockSpec((grp, SUBLANES, d), lambda b, j: (b, jnp.minimum((j + 1) * per, last), 0)),
                  _mod_spec(nb, d, 0, n_lat_tiles, grp), _mod_spec(nb, d, 1, n_lat_tiles, grp),
                  _resident((1, d))] + [_resident(w.shape) for w in ws]
                 + [_resident(t.shape) for t in gdn_consts],
        out_specs=[_tile_spec(n, grp) for n in widths],
        out_shape=[jax.ShapeDtypeStruct((nb, ttot, n), dt) for n, dt in zip(widths, dts)],
        compiler_params=_cparams("parallel", "parallel"),
        name="kin",
    )(x3, x3, x3, mod, mod, gain, *ws, *gdn_consts)


def _s5_kernel(uf_ref, ub_ref, perm_ref, permt_ref, bmf, bmb, cmf, cmb, af_ref, ab_ref,
               yf_ref, yb_ref, st_ref, bu_ref, s_ref):
    i = pl.program_id(0)
    nb, tt, _ = uf_ref.shape
    sub = SUBLANES
    n_grp, _, gw = bmf.shape
    half = gw // 2

    @pl.when(i == 0)
    def _():
        st_ref[...] = jnp.zeros_like(st_ref)

    dirs = ((uf_ref, bmf, cmf, af_ref, yf_ref, range(tt)),
            (ub_ref, bmb, cmb, ab_ref, yb_ref, range(tt - 1, -1, -1)))

    def regroup(dr):
        u_ref = dirs[dr][0]
        parts = []
        for k in range(tt // sub):
            blk = u_ref[:, k * sub:(k + 1) * sub, :].reshape(nb * sub, W).astype(BF16)
            parts.append(_dot(perm_ref[...], blk).astype(BF16))
        return jnp.concatenate(parts, axis=0)

    u_tb = [regroup(0), regroup(1)]
    y_tb = [None, None]

    def project(dr, g):
        bu_ref[dr, g] = _dot(u_tb[dr], dirs[dr][1][g])

    def recur(dr, g):
        a_ref, order = dirs[dr][3], dirs[dr][5]
        for c in range(half // LANES):
            re = slice(c * LANES, (c + 1) * LANES)
            im = slice(half + c * LANES, half + (c + 1) * LANES)
            a_re = a_ref[g, :, re]
            a_im = a_ref[g, :, im]
            s_re = st_ref[dr, g, :, re]
            s_im = st_ref[dr, g, :, im]
            for t in order:
                r = slice(t * nb, (t + 1) * nb)
                n_re = a_re * s_re - a_im * s_im + bu_ref[dr, g, r, re]
                n_im = a_re * s_im + a_im * s_re + bu_ref[dr, g, r, im]
                s_re, s_im = n_re, n_im
                s_ref[dr, g, r, re] = s_re.astype(BF16)
                s_ref[dr, g, r, im] = s_im.astype(BF16)
            st_ref[dr, g, :, re] = s_re
            st_ref[dr, g, :, im] = s_im

    def read_out(dr, g):
        t = _dot(s_ref[dr, g], dirs[dr][2][g])
        y_tb[dr] = t if y_tb[dr] is None else y_tb[dr] + t

    for dr in range(2):
        project(dr, 0)
    for g in range(n_grp):
        for dr in range(2):
            if g + 1 < n_grp:
                project(dr, g + 1)
            recur(dr, g)
        for dr in range(2):
            read_out(dr, g)
    for dr in range(2):
        y_ref = dirs[dr][4]
        for k in range(tt // sub):
            hi, lo = _split(y_tb[dr][k * sub * nb:(k + 1) * sub * nb], 2)
            y_bt = _dot(permt_ref[...], hi) + _dot(permt_ref[...], lo)
            y_ref[:, k * sub:(k + 1) * sub, :] = y_bt.reshape(nb, sub, W)


def _s5(u5, perm, permt, bmats, cmats, avecs, n_lat_tiles):
    nb, ttot, _ = u5.shape
    tt = S5_STEPS
    n_tiles = ttot // tt
    fwd = lambda i: (0, (i + n_lat_tiles) % n_tiles, 0)
    bwd = lambda i: (0, n_tiles - 1 - i, 0)
    rows = tt * nb
    n_grp, _, gw = bmats[0].shape
    return pl.pallas_call(
        _s5_kernel,
        grid=(n_tiles,),
        in_specs=[pl.BlockSpec((nb, tt, W), fwd), pl.BlockSpec((nb, tt, W), bwd),
                  _resident(perm.shape), _resident(permt.shape),
                  _resident(bmats[0].shape), _resident(bmats[1].shape),
                  _resident(cmats[0].shape), _resident(cmats[1].shape),
                  _resident(avecs[0].shape), _resident(avecs[1].shape)],
        out_specs=[pl.BlockSpec((nb, tt, W), fwd), pl.BlockSpec((nb, tt, W), bwd)],
        out_shape=[jax.ShapeDtypeStruct((nb, ttot, W), F32)] * 2,
        scratch_shapes=[pltpu.VMEM((2, n_grp, nb, gw), F32),
                        pltpu.VMEM((2, n_grp, rows, gw), F32),
                        pltpu.VMEM((2, n_grp, rows, gw), BF16)],
        compiler_params=_cparams("arbitrary"),
        name="s5",
    )(u5, u5, perm, permt, bmats[0], bmats[1], cmats[0], cmats[1], avecs[0], avecs[1])


def _fnet_kernel(a_ref, cos_ref, nsin_ref, y_ref):
    y = _dot(cos_ref[...], a_ref[:, :W]) + _dot(nsin_ref[...], a_ref[:, W:])
    y_ref[...] = y.astype(y_ref.dtype)


def _fnet_alias_kernel(a_ref, cos_ref, nsin_ref, prev_ref, y_ref):
    _fnet_kernel(a_ref, cos_ref, nsin_ref, y_ref)


def _fnet(afn, cos_l, nsin_l, cos_c, nsin_c, l_lat, l_ctx):
    nb, ttot, _ = afn.shape
    lat = pl.pallas_call(
        _fnet_kernel,
        grid=(nb,),
        in_specs=[pl.BlockSpec((None, l_lat, 2 * W), lambda b: (b, 0, 0)),
                  _resident(cos_l.shape), _resident(nsin_l.shape)],
        out_specs=pl.BlockSpec((None, l_lat, W), lambda b: (b, 0, 0)),
        out_shape=jax.ShapeDtypeStruct((nb, ttot, W), BF16),
        compiler_params=_cparams("parallel"),
        name="fnet_lat",
    )(afn, cos_l, nsin_l)
    blk = l_lat // l_ctx
    return pl.pallas_call(
        _fnet_alias_kernel,
        grid=(nb,),
        in_specs=[pl.BlockSpec((None, l_ctx, 2 * W), lambda b: (b, blk, 0)),
                  _resident(cos_c.shape), _resident(nsin_c.shape),
                  pl.BlockSpec(memory_space=pl.ANY)],
        out_specs=pl.BlockSpec((None, l_ctx, W), lambda b: (b, blk, 0)),
        out_shape=jax.ShapeDtypeStruct((nb, ttot, W), BF16),
        input_output_aliases={3: 0},
        compiler_params=_cparams("parallel"),
        name="fnet_ctx",
    )(afn, cos_c, nsin_c, lat)


def _sgu_kernel(uv_ref, w_ref, bias_ref, lng_ref, lnb_ref, avg_ref, y_ref):
    lane_group = lax.broadcasted_iota(jnp.int32, (SG_CHUNK, W), 1) // SG_GROUP
    rows = [slice(n * SG_CHUNK, (n + 1) * SG_CHUNK) for n in range(uv_ref.shape[0] // SG_CHUNK)]
    v = [_gelu(uv_ref[r, W:].astype(F32)) for r in rows]
    dv = [t - _dot_sel(t, avg_ref[...], 2) for t in v]
    var = [_dot_sel(t * t, avg_ref[...], 2) for t in dv]
    vn = [(d_ * lax.rsqrt(s_ + EPS) * lng_ref[...] + lnb_ref[...]).astype(BF16) for d_, s_ in zip(dv, var)]
    sv = [_dot(w_ref[0], t) for t in vn]
    for g in range(1, SG_GROUPS):
        sv = [jnp.where(lane_group == g, _dot(w_ref[g], t), s_) for t, s_ in zip(vn, sv)]
    for r, s_ in zip(rows, sv):
        y_ref[r, :] = (_gelu(uv_ref[r, :W].astype(F32)) * (s_ + bias_ref[...])).astype(y_ref.dtype)


def _sgu(sg, w, bias, lng, lnb, avg, rows):
    nb, ttot, _ = sg.shape
    return pl.pallas_call(
        _sgu_kernel,
        grid=(nb, ttot // rows),
        in_specs=[pl.BlockSpec((None, rows, 2 * W), lambda b, n: (b, n, 0)),
                  _resident(w.shape), _resident(bias.shape), _resident(lng.shape),
                  _resident(lnb.shape), _resident(avg.shape)],
        out_specs=pl.BlockSpec((None, rows, W), lambda b, n: (b, n, 0)),
        out_shape=jax.ShapeDtypeStruct((nb, ttot, W), BF16),
        compiler_params=_cparams("parallel", "parallel"),
        name="sgu",
    )(sg, w, bias, lng, lnb, avg)


def _gdn_masks(backward):
    c, heads = GDN_CHUNK, GDN_HEADS
    i = np.arange(c)[:, None]
    j = (np.arange(W) % c)[None, :]
    incl = (i <= j) if backward else (i >= j)
    strict = (i < j) if backward else (i > j)
    j64 = np.arange(c)[None, :]
    incl64 = (i <= j64) if backward else (i >= j64)
    same = lambda n: (i // n) == (j // n)
    levels = [same(GDN_BASE)]
    n = GDN_BASE
    while n < c:
        levels.append(same(2 * n) & ~same(n))
        n *= 2
    f = lambda m: jnp.asarray(m, F32)
    return (f(np.stack([incl, strict, i == j])), jnp.asarray(incl64, BF16), f(np.stack(levels)))


def _gdn_chains(chains, bd_ref, bd16_ref):
    c, heads = GDN_CHUNK, GDN_HEADS
    each = lambda f, *cols: [f(*args) for args in zip(*cols)]

    def expand_heads(t):
        t = t.astype(BF16)
        return jnp.concatenate([t] * heads, axis=0) * bd16_ref[:, :t.shape[1]]

    kvq, gb, expand, m64, incl64, lvls, s_refs = zip(*chains)
    kvq = [t.astype(F32) for t in kvq]
    k = [t[:, :W] for t in kvq]
    v = [t[:, W:2 * W] for t in kvq]
    q = [t[:, 2 * W:] for t in kvq]
    ge = each(lambda g, e: _dot_sel(g, e[...], 2), gb, expand)
    g_l = [t[:, :W] for t in ge]
    beta_l = [t[:, W:] for t in ge]
    incl = [m[0] for m in m64]
    strict = [m[1] for m in m64]
    diag = [m[2] for m in m64]

    def cumulative(g, i64):
        acc = None
        for piece in _split(g, 3):
            t = _dot(i64[...], piece)
            acc = t if acc is None else acc + t
        return acc

    gc = each(cumulative, g_l, incl64)
    dif = each(lambda g, dg: g - jnp.sum(g * dg, axis=0, keepdims=True), gc, diag)
    g_tot = [jnp.sum(t, axis=0, keepdims=True) for t in g_l]
    kb = each(lambda a_, b_: a_ * b_, k, beta_l)
    k_st = [expand_heads(t) for t in k]
    kq = each(lambda a_, b_, st: lax.dot_general(
        jnp.concatenate([a_, b_], axis=0).astype(BF16), st, (((1,), (1,)), ((), ())),
        preferred_element_type=F32), kb, q, k_st)
    rel = each(lambda d_, i: jnp.exp(d_ * i) * i, dif, incl)
    a = each(lambda t, r, st: t[:c] * r * st, kq, rel, strict)
    qk = each(lambda t, r: t[c:] * r, kq, rel)
    pw = each(lambda t, lv: t * lv[0], a, lvls)
    t_inv = each(lambda dg, t: dg - t, diag, pw)
    pw_bd = [expand_heads(t) for t in pw]
    for _ in range(int(math.log2(GDN_BASE)) - 1):
        pw = each(lambda t, bd_: _dot(t.astype(BF16), bd_), pw, pw_bd)
        pw_bd = [expand_heads(t) for t in pw]
        t_inv = each(lambda t, bd_: t + _dot(t.astype(BF16), bd_), t_inv, pw_bd)
    for lvl in range(1, lvls[0].shape[0]):
        z_bd = each(lambda t, lv: expand_heads(t * lv[lvl]), a, lvls)
        tz = each(lambda t, z_: _dot(t.astype(BF16), z_), t_inv, z_bd)
        t_inv = each(lambda t, tz_: t - _dot(tz_.astype(BF16), expand_heads(t)), t_inv, tz)
    e_gc = [jnp.exp(t) for t in gc]
    rhs = each(lambda v_, b_, kb_, e: expand_heads(jnp.concatenate([v_ * b_, kb_ * e], axis=1)),
               v, beta_l, kb, e_gc)
    w_all = each(lambda t, r: _dot(t.astype(BF16), r), t_inv, rhs)
    s = [r[...] for r in s_refs]
    sb = [t.astype(BF16) for t in s]
    ws = each(lambda w_, q_, e, sb_: _dot(jnp.concatenate([w_[:, W:], q_ * e], axis=0).astype(BF16), sb_),
              w_all, q, e_gc, sb)
    u = each(lambda w_, t: w_[:, :W] - t[:c], w_all, ws)
    o = each(lambda t, qk_, u_: t[c:] + _dot(qk_.astype(BF16), expand_heads(u_)), ws, qk, u)
    k_dec = each(lambda k_, gt, g: k_ * jnp.exp(gt - g), k, g_tot, gc)
    upd = each(lambda kd, u_: lax.dot_general(kd.astype(BF16), u_.astype(BF16), (((0,), (0,)), ((), ())),
                                              preferred_element_type=F32), k_dec, u)
    for r, s_, gt, up in zip(s_refs, s, g_tot, upd):
        r[...] = s_ * jnp.exp(gt) + up * bd_ref[...]
    return o


def _gdn_chunk_kernel(kvqf_ref, kvqb_ref, gbf_ref, gbb_ref, ef_ref, eb_ref, m64f_ref, m64b_ref,
                      i64f_ref, i64b_ref, lvlf_ref, lvlb_ref, bd_ref, bd16_ref, of_ref, ob_ref, s_ref):
    @pl.when(pl.program_id(1) == 0)
    def _():
        s_ref[...] = jnp.zeros_like(s_ref)

    chains = []
    for bi in range(kvqf_ref.shape[0]):
        chains.append((kvqf_ref[bi], gbf_ref[bi], ef_ref, m64f_ref, i64f_ref, lvlf_ref, s_ref.at[bi, 0]))
        chains.append((kvqb_ref[bi], gbb_ref[bi], eb_ref, m64b_ref, i64b_ref, lvlb_ref, s_ref.at[bi, 1]))
    outs = _gdn_chains(chains, bd_ref, bd16_ref)
    for bi in range(kvqf_ref.shape[0]):
        of_ref[bi] = outs[2 * bi].astype(of_ref.dtype)
        ob_ref[bi] = outs[2 * bi + 1].astype(ob_ref.dtype)


def _gdn_chunk(kvq, gb, expand, masks, bd, bd16, l_lat, per_step):
    c = GDN_CHUNK
    nb, ttot, _ = kvq.shape
    n_lat = l_lat // c
    n_chunks = ttot // c
    fwd = lambda b, i: (b, (i + n_lat) % n_chunks, 0)
    bwd = lambda b, i: (b, n_chunks - 1 - i, 0)
    consts = [expand[0], expand[1], masks[0][0], masks[1][0], masks[0][1], masks[1][1],
              masks[0][2], masks[1][2], bd, bd16]
    return pl.pallas_call(
        _gdn_chunk_kernel,
        grid=(nb // per_step, n_chunks),
        in_specs=[pl.BlockSpec((per_step, c, 3 * W), fwd), pl.BlockSpec((per_step, c, 3 * W), bwd),
                  pl.BlockSpec((per_step, c, LANES), fwd), pl.BlockSpec((per_step, c, LANES), bwd)]
                 + [_resident(t.shape) for t in consts],
        out_specs=[pl.BlockSpec((per_step, c, W), fwd), pl.BlockSpec((per_step, c, W), bwd)],
        out_shape=[jax.ShapeDtypeStruct((nb, ttot, W), BF16)] * 2,
        scratch_shapes=[pltpu.VMEM((per_step, 2, W, W), F32)],
        compiler_params=_cparams("parallel", "arbitrary"),
        name="gdn_chunk",
    )(kvq, kvq, gb, gb, *consts)


def _merge_kernel(x_ref, h_ref, g1_ref, yf_ref, yb_ref, u5_ref, dskip_ref, gluw_ref,
                  glub_ref, of_ref, ob_ref, z_ref, gain_ref, avg_ref, yfn_ref, ysg_ref, wgate_ref,
                  wbr_ref, wout_ref, o_ref):
    d = x_ref.shape[2]
    subs = range(x_ref.shape[0])
    y5 = [_gelu(yf_ref[s] + yb_ref[s] + dskip_ref[...] * u5_ref[s]) for s in subs]
    glu = [_dot(t.astype(BF16), gluw_ref[...]) for t in y5]
    o = [of_ref[s].astype(F32) + ob_ref[s].astype(F32) for s in subs]
    ms = [_dot_sel(t * t, avg_ref[...], 2) for t in o]
    y5 = [t * _sigmoid(g_ + glub_ref[...]) for t, g_ in zip(y5, glu)]
    yg = [o[s] * lax.rsqrt(ms[s] + EPS) * gain_ref[...] * _silu(z_ref[s].astype(F32)) for s in subs]
    ys = [(y5[s], yfn_ref[s], yg[s], ysg_ref[s]) for s in subs]
    acc = [None for _ in subs]
    for j in range(N_BRANCH):
        for s in subs:
            th = jnp.tanh(_dot(h_ref[s], wgate_ref[:, j * d:(j + 1) * d]))
            b_half = _dot(ys[s][j].astype(BF16), wbr_ref[j])
            t = th * b_half + b_half
            acc[s] = t if acc[s] is None else acc[s] + t
    out = [_dot(t.astype(BF16), wout_ref[...]) for t in acc]
    for s in subs:
        o_ref[s] = x_ref[s] + g1_ref[s] * out[s]


def _merge(x3, mod, acts, consts, n_lat_tiles, n_tiles):
    nb, ttot, d = x3.shape
    h, yf, yb, u5, of, ob, z, yfn, ysg = acts
    dskip, gluw, glub, gain, avg, wgate, wbr, wout = consts
    return pl.pallas_call(
        _merge_kernel,
        grid=(nb // PAIR, n_tiles),
        in_specs=[_tile_spec(d), _tile_spec(d), _mod_spec(nb, d, 2, n_lat_tiles),
                  _tile_spec(W), _tile_spec(W), _tile_spec(W),
                  _resident(dskip.shape), _resident(gluw.shape), _resident(glub.shape),
                  _tile_spec(W), _tile_spec(W), _tile_spec(W),
                  _resident(gain.shape), _resident(avg.shape),
                  _tile_spec(W), _tile_spec(W), _resident(wgate.shape),
                  _resident(wbr.shape), _resident(wout.shape)],
        out_specs=_tile_spec(d),
        out_shape=jax.ShapeDtypeStruct((nb, n_tiles * ROW_TILE, d), F32),
        compiler_params=_cparams("parallel", "parallel"),
        name="merge",
    )(x3, h, mod, yf, yb, u5, dskip, gluw, glub, of, ob, z, gain, avg, yfn, ysg, wgate, wbr, wout)


def _ffn_kernel(x_ref, sh_ref, sc_ref, g2_ref, gain_ref, w1_ref, w2_ref, nf_ref, o_ref, *, final):
    dff = w2_ref.shape[0]
    subs = range(x_ref.shape[0])
    h = [_modulated_norm(x_ref[s], gain_ref[...], sc_ref[s], sh_ref[s]).astype(BF16) for s in subs]
    t = [_dot(h_, w1_ref[...]) for h_ in h]
    act = [(_silu(t_[:, :dff]) * t_[:, dff:]).astype(BF16) for t_ in t]
    out = [_dot(a_, w2_ref[...]) for a_ in act]
    for s in subs:
        y = x_ref[s] + g2_ref[s] * out[s]
        if final:
            y = y * lax.rsqrt(jnp.mean(y * y, axis=-1, keepdims=True) + EPS) * nf_ref[...]
        o_ref[s] = y


def _ffn(x3, mod, gain, w1, w2, norm_f, n_lat_tiles, final):
    nb, ttot, d = x3.shape
    return pl.pallas_call(
        functools.partial(_ffn_kernel, final=final),
        grid=(nb // PAIR, ttot // ROW_TILE),
        in_specs=[_tile_spec(d), _mod_spec(nb, d, 3, n_lat_tiles), _mod_spec(nb, d, 4, n_lat_tiles),
                  _mod_spec(nb, d, 5, n_lat_tiles),
                  _resident(gain.shape), _resident(w1.shape), _resident(w2.shape),
                  _resident(norm_f.shape)],
        out_specs=_tile_spec(d),
        out_shape=jax.ShapeDtypeStruct(x3.shape, F32),
        compiler_params=_cparams("parallel", "parallel"),
        name="ffn",
    )(x3, mod, mod, mod, gain, w1, w2, norm_f)


def _dft_tables(n):
    idx = np.arange(n, dtype=np.int64)
    ang = 2.0 * np.pi * ((idx[:, None] * idx[None, :]) % n).astype(np.float64) / n
    return np.cos(ang), np.sin(ang)


def _s5_tables(lam_re, lam_im, log_dt, b_re, b_im, c_re, c_im):
    g, p, cg = S5_GROUPS, S5_STATE, S5_GROUP
    lam = lax.complex(lam_re.astype(F32), lam_im.astype(F32))
    a_bar = jnp.exp(lam * jnp.exp(log_dt.astype(F32))[:, None])
    b_bar = ((a_bar - 1.0) / lam)[..., None] * lax.complex(b_re.astype(F32), b_im.astype(F32))
    eye = jnp.eye(g, dtype=F32)
    bm_re = jnp.einsum('gpc,gh->gchp', b_bar.real, eye).reshape(g * cg, g * p)
    bm_im = jnp.einsum('gpc,gh->gchp', b_bar.imag, eye).reshape(g * cg, g * p)
    cm_re = jnp.einsum('gcp,gh->gphc', c_re.astype(F32), eye).reshape(g * p, g * cg)
    cm_im = jnp.einsum('gcp,gh->gphc', c_im.astype(F32), eye).reshape(g * p, g * cg)
    ng, h = N_S5 // S5_LANE_GROUP, S5_LANE_GROUP
    bmat = jnp.concatenate([bm_re.reshape(W, ng, h), bm_im.reshape(W, ng, h)], axis=2)
    bmat = jnp.transpose(bmat, (1, 0, 2)).astype(BF16)
    cmat = jnp.concatenate([cm_re.reshape(ng, h, W), -cm_im.reshape(ng, h, W)], axis=1).astype(BF16)
    avec = jnp.concatenate([a_bar.real.reshape(ng, 1, h), a_bar.imag.reshape(ng, 1, h)], axis=2)
    return bmat, cmat, avec


def _layer_consts(l, p):
    d = p['w_in'].shape[1]
    w_in = p['w_in'][l]
    sizes = (W, W, W, 2 * GDN_HEADS, 2 * GDN_HEADS, W, W, W, W, W, N_BRANCH * d)
    offs = np.concatenate([[0], np.cumsum(sizes)])
    col = lambda j: w_in[:, offs[j]:offs[j + 1]]
    u5, k, v, a, bt, q, z, ufn, usg, vsg, gate = [col(j) for j in range(len(sizes))]
    cc, sc = _dft_tables(FN_GROUP)
    fw = p['fn_w'][l].astype(F32)
    wc = jnp.einsum('cd,gde->gce', jnp.asarray(cc, F32), fw, precision=HIGHEST)
    ws = jnp.einsum('cd,gde->gce', jnp.asarray(sc, F32), fw, precision=HIGHEST)
    ufn_g = ufn.reshape(d, FN_GROUPS, FN_GROUP)
    fold_c = jnp.einsum('kgc,gce->kge', ufn_g, wc, precision=HIGHEST).reshape(d, W)
    fold_s = jnp.einsum('kgc,gce->kge', ufn_g, ws, precision=HIGHEST).reshape(d, W)
    w_ab = jnp.concatenate([a, bt, jnp.zeros((d, LANES - 4 * GDN_HEADS), F32)], axis=1)
    ws_in = [u5, jnp.concatenate([k, v, q], axis=1), w_ab, z,
             jnp.concatenate([fold_c, fold_s], axis=1), jnp.concatenate([usg, vsg], axis=1)]
    ws_in = [w.astype(BF16) for w in ws_in]
    w_gate = (0.5 * gate).astype(BF16)

    s5 = [_s5_tables(p['s5_lam_re'][l, dr], p['s5_lam_im'][l, dr], p['s5_log_dt'][l, dr],
                     p['s5_b_re'][l, dr], p['s5_b_im'][l, dr], p['s5_c_re'][l, dr],
                     p['s5_c_im'][l, dr]) for dr in range(2)]

    conv = p['gdn_conv'][l].astype(F32)
    cw = jnp.transpose(conv, (2, 0, 1)).reshape(conv.shape[2], 3 * W)
    pad = jnp.zeros((1, LANES - 2 * GDN_HEADS), F32)
    alog = jnp.concatenate([p['gdn_a_log'][l].astype(F32).reshape(1, -1), pad], axis=1)
    dtb = jnp.concatenate([p['gdn_dt_bias'][l].astype(F32).reshape(1, -1), pad], axis=1)

    sg_bias = jnp.repeat(p['sg_b'][l].astype(F32).T, SG_GROUP, axis=1)
    merge_consts = (p['s5_d'][l].astype(F32).reshape(1, W), p['s5_glu_w'][l].astype(BF16),
                    p['s5_glu_b'][l].astype(F32).reshape(1, W),
                    jnp.tile(p['gdn_norm'][l].astype(F32), GDN_HEADS).reshape(1, W))
    return dict(
        ws_in=ws_in, w_gate=w_gate, s5=s5, cw=cw, alog=alog, dtb=dtb,
        sg_w=p['sg_w'][l].astype(BF16), sg_bias=sg_bias,
        sg_lng=p['sg_ln_g'][l].astype(F32).reshape(1, W), sg_lnb=p['sg_ln_b'][l].astype(F32).reshape(1, W),
        merge=merge_consts, wbr_half=(0.5 * p['w_branch'][l]).astype(BF16),
        wout=p['w_out'][l].astype(BF16),
        w1=p['ffn_w1'][l].astype(BF16), w2=p['ffn_w2'][l].astype(BF16),
        norm1=p['norm1'][l].astype(F32).reshape(1, d), norm2=p['norm2'][l].astype(F32).reshape(1, d))


def kernel(x, c, ctx, c_ctx, ada_w, ada_b, norm1, norm2, w_in, s5_lam_re, s5_lam_im, s5_log_dt, s5_b_re, s5_b_im, s5_c_re, s5_c_im, s5_d, s5_glu_w, s5_glu_b, fn_w, gdn_conv, gdn_a_log, gdn_dt_bias, gdn_norm, sg_ln_g, sg_ln_b, sg_w, sg_b, w_branch, w_out, ffn_w1, ffn_w2, norm_f):
    p = dict(w_in=w_in, s5_lam_re=s5_lam_re, s5_lam_im=s5_lam_im, s5_log_dt=s5_log_dt,
             s5_b_re=s5_b_re, s5_b_im=s5_b_im, s5_c_re=s5_c_re, s5_c_im=s5_c_im, s5_d=s5_d,
             s5_glu_w=s5_glu_w, s5_glu_b=s5_glu_b, fn_w=fn_w, gdn_conv=gdn_conv,
             gdn_a_log=gdn_a_log, gdn_dt_bias=gdn_dt_bias, gdn_norm=gdn_norm, sg_ln_g=sg_ln_g,
             sg_ln_b=sg_ln_b, sg_w=sg_w, sg_b=sg_b, w_branch=w_branch, w_out=w_out,
             ffn_w1=ffn_w1, ffn_w2=ffn_w2, norm1=norm1, norm2=norm2)
    nb, l_lat, d = x.shape
    l_ctx = ctx.shape[1]
    depth = ada_w.shape[0]
    ttot = l_lat + l_ctx
    assert nb % SUBLANES == 0 and nb % PAIR == 0 and nb % KIN_GROUP == 0
    assert d == N_BRANCH * W and l_lat % l_ctx == 0
    assert l_lat % ROW_TILE == 0 and l_ctx % ROW_TILE == 0 and ROW_TILE % SG_CHUNK == 0
    n_lat_tiles = l_lat // ROW_TILE
    n_tiles = ttot // ROW_TILE
    sg_rows = max(r for r in range(SG_CHUNK, 7 * SG_CHUNK, SG_CHUNK) if ttot % r == 0)

    x3 = jnp.concatenate([x, ctx], axis=1).astype(F32)

    n_cond = nb + max(PAIR, KIN_GROUP)
    ada_rows = -(-n_cond // SUBLANES) * SUBLANES
    cc = jnp.zeros((ada_rows, d), F32).at[:nb].set(c.astype(F32)).at[nb:n_cond].set(c_ctx.astype(F32))
    mods = _ada(cc, ada_w.astype(F32), ada_b.astype(F32))[:, :n_cond].reshape(depth, n_cond, 1, 6 * d)

    cos_l, sin_l = _dft_tables(l_lat)
    cos_c, sin_c = _dft_tables(l_ctx)
    sc_l = 1.0 / math.sqrt(l_lat * FN_GROUP)
    sc_c = 1.0 / math.sqrt(l_ctx * FN_GROUP)
    cos_l, nsin_l = jnp.asarray(cos_l * sc_l, BF16), jnp.asarray(-sin_l * sc_l, BF16)
    cos_c, nsin_c = jnp.asarray(cos_c * sc_c, BF16), jnp.asarray(-sin_c * sc_c, BF16)
    nf = norm_f.astype(F32).reshape(1, d)

    lane_head = np.arange(W) // GDN_HEAD_DIM
    seg = (lane_head[:, None] == lane_head[None, :]).astype(np.float32)
    seg_ones = jnp.asarray(seg, BF16)
    seg_avg = jnp.asarray(seg / GDN_HEAD_DIM, BF16)
    expand = []
    for dr in range(2):
        e = np.zeros((LANES, 2 * W), np.float32)
        for h in range(GDN_HEADS):
            e[dr * GDN_HEADS + h, :W] = lane_head == h
            e[2 * GDN_HEADS + dr * GDN_HEADS + h, W:] = lane_head == h
        expand.append(jnp.asarray(e, BF16))
    gdn_masks = [_gdn_masks(False), _gdn_masks(True)]
    bd = jnp.asarray(seg, F32)
    bd16 = jnp.asarray(np.concatenate([seg, seg], axis=1), BF16)
    src = (np.arange(nb)[None, :] * SUBLANES + np.arange(SUBLANES)[:, None]).reshape(-1)
    perm_np = np.zeros((nb * SUBLANES, nb * SUBLANES), np.float32)
    perm_np[np.arange(nb * SUBLANES), src] = 1.0
    perm, permt = jnp.asarray(perm_np, BF16), jnp.asarray(perm_np.T, BF16)

    for l in range(depth):
        k = _layer_consts(l, p)
        mod = mods[l]
        last = l == depth - 1
        u5, kvq_p, gb, z, afn, sg, h = _kin(x3, mod, k['norm1'], k['ws_in'],
                                            (k['cw'], seg_ones, k['alog'], k['dtb']), n_lat_tiles)
        yf, yb = _s5(u5, perm, permt, [t[0] for t in k['s5']], [t[1] for t in k['s5']],
                     [t[2] for t in k['s5']], l_lat // S5_STEPS)
        yfn = _fnet(afn, cos_l, nsin_l, cos_c, nsin_c, l_lat, l_ctx)
        ysg = _sgu(sg, k['sg_w'], k['sg_bias'], k['sg_lng'], k['sg_lnb'], seg_avg, sg_rows)
        of, ob = _gdn_chunk(kvq_p, gb, expand, gdn_masks, bd, bd16, l_lat, 8)
        acts = (h, yf, yb, u5, of, ob, z, yfn, ysg)
        consts = k['merge'] + (seg_avg, k['w_gate'], k['wbr_half'], k['wout'])
        x3 = _merge(x3, mod, acts, consts, n_lat_tiles, n_lat_tiles if last else n_tiles)
        x3 = _ffn(x3, mod, k['norm2'], k['w1'], k['w2'], nf, n_lat_tiles, final=last)

    return x3.astype(x.dtype)
```

```python
import functools
import math

import numpy as np
import jax
import jax.numpy as jnp
from jax import lax
from jax.experimental import pallas as pl
from jax.experimental.pallas import tpu as pltpu

F32 = jnp.float32
BF16 = jnp.bfloat16
HIGHEST = lax.Precision.HIGHEST

EPS = 1e-6
W = 256
N_BRANCH = 4
S5_GROUP = 16
S5_GROUPS = W // S5_GROUP
S5_STATE = 64
N_S5 = S5_GROUPS * S5_STATE
FN_GROUPS = 4
FN_GROUP = W // FN_GROUPS
GDN_HEADS = 4
GDN_HEAD_DIM = W // GDN_HEADS
GDN_CHUNK = 64
GDN_BASE = 8
SG_GROUPS = 4
SG_GROUP = W // SG_GROUPS
SG_CHUNK = 128
LANES = 128
SUBLANES = 8
ROW_TILE = 256
PAIR = 2
KIN_GROUP = 4
S5_STEPS = 32
S5_LANE_GROUP = 256
VMEM_LIMIT = 56 * 1024 * 1024


def _cparams(*sem):
    return pltpu.CompilerParams(dimension_semantics=sem, vmem_limit_bytes=VMEM_LIMIT)


def _resident(shape):
    nd = len(shape)
    return pl.BlockSpec(shape, lambda *_: (0,) * nd, pipeline_mode=pl.Buffered(1))


def _dot(a, b):
    return jnp.dot(a, b, preferred_element_type=F32)


def _split(x, parts):
    out = []
    for _ in range(parts - 1):
        hi = x.astype(BF16)
        out.append(hi)
        x = x - hi.astype(F32)
    out.append(x.astype(BF16))
    return out


def _dot_sel(x, sel, parts):
    acc = None
    for piece in _split(x, parts):
        t = _dot(piece, sel)
        acc = t if acc is None else acc + t
    return acc


def _gelu(x):
    return 0.5 * x * (1.0 + jnp.tanh(math.sqrt(2.0 / math.pi) * (x + 0.044715 * (x * x * x))))


def _sigmoid(x):
    return 0.5 * jnp.tanh(0.5 * x) + 0.5


def _silu(x):
    return x * _sigmoid(x)


def _modulated_norm(x, gain, scale, shift):
    y = x * lax.rsqrt(jnp.mean(x * x, axis=-1, keepdims=True) + EPS) * gain
    return y * (1.0 + scale) + shift


def _ada_kernel(c_ref, w_ref, b_ref, o_ref):
    c = c_ref[...]
    o_ref[0] = jnp.dot(_silu(c), w_ref[0], preferred_element_type=F32, precision=HIGHEST) + b_ref[0]


def _ada(cc, ada_w, ada_b):
    depth, d, n = ada_w.shape
    rows = cc.shape[0]
    return pl.pallas_call(
        _ada_kernel,
        grid=(depth, n // d),
        in_specs=[pl.BlockSpec((rows, d), lambda l, j: (0, 0)),
                  pl.BlockSpec((1, d, d), lambda l, j: (l, 0, j)),
                  pl.BlockSpec((1, 1, d), lambda l, j: (l, 0, j))],
        out_specs=pl.BlockSpec((1, rows, d), lambda l, j: (l, 0, j)),
        out_shape=jax.ShapeDtypeStruct((depth, rows, n), F32),
        compiler_params=_cparams("parallel", "parallel"),
        name="ada",
    )(cc, ada_w, ada_b.reshape(depth, 1, n))


def _tile_spec(n, group=PAIR):
    return pl.BlockSpec((group, ROW_TILE, n), lambda b, j: (b, j, 0))


def _mod_spec(nb, d, k, n_lat_tiles, group=PAIR):
    return pl.BlockSpec((group, 1, d), lambda b, j: (jnp.where(j < n_lat_tiles, b, nb // group), 0, k))


def _gdn_features(kvq, before, after, cw_ref, ones_ref):
    rows, n = kvq.shape
    row = lax.broadcasted_iota(jnp.int32, (rows, n), 0)
    xp = jnp.where(row == 0, before, pltpu.roll(kvq, 1, 0))
    xn = jnp.where(row == rows - 1, after, pltpu.roll(kvq, rows - 1, 0))
    y = _silu(xp * cw_ref[0:1, :] + kvq * cw_ref[1:2, :] + xn * cw_ref[2:3, :])
    k = y[:, :W]
    q = y[:, 2 * W:]
    k = k * lax.rsqrt(_dot_sel(k * k, ones_ref[...], 2) + EPS)
    q = q * lax.rsqrt(_dot_sel(q * q, ones_ref[...], 2) + EPS) * GDN_HEAD_DIM ** -0.5
    return k, y[:, W:2 * W], q


def _gdn_gates(ab, alog_ref, dtb_ref):
    z = ab + dtb_ref[...]
    softplus = jnp.maximum(z, 0.0) + jnp.log(1.0 + jnp.exp(-jnp.abs(z)))
    g = -jnp.exp(alog_ref[...]) * softplus
    lane = lax.broadcasted_iota(jnp.int32, ab.shape, 1)
    return jnp.where(lane < 2 * GDN_HEADS, g, _sigmoid(ab))


def _kin_kernel(x_ref, xp_ref, xn_ref, sh_ref, sc_ref, g_ref, w5, wkvq, wab, wz, wfn, wsg,
                cw_ref, ones_ref, alog_ref, dtb_ref, o5, okvq, ogb, oz, ofn, osg, oh,
                *, n_lat_tiles, n_tiles):
    j = pl.program_id(1)
    subs = range(x_ref.shape[0])
    halo = xp_ref.shape[1]
    hb = [_modulated_norm(x_ref[s], g_ref[...], sc_ref[s], sh_ref[s]).astype(BF16) for s in subs]
    edge = [jnp.concatenate([xp_ref[s], xn_ref[s]], axis=0) for s in subs]
    eb = [_modulated_norm(edge[s], g_ref[...], sc_ref[s], sh_ref[s]).astype(BF16) for s in subs]
    has_prev = jnp.logical_and(j != 0, j != n_lat_tiles).astype(F32)
    has_next = jnp.logical_and(j != n_lat_tiles - 1, j != n_tiles - 1).astype(F32)
    for s in subs:
        oh[s] = hb[s]
    for w_ref, o_ref in ((w5, o5), (wz, oz), (wfn, ofn), (wsg, osg)):
        for s in subs:
            o_ref[s] = _dot(hb[s], w_ref[...]).astype(o_ref.dtype)
    rows = x_ref.shape[1]
    kvq = [_dot(jnp.concatenate([hb[s], eb[s]], axis=0), wkvq[...]) for s in subs]
    for s in subs:
        k, v, q = _gdn_features(kvq[s][:rows], kvq[s][rows + halo - 1:rows + halo, :] * has_prev,
                                kvq[s][rows + halo:rows + halo + 1, :] * has_next, cw_ref, ones_ref)
        okvq[s, :, :W] = k.astype(okvq.dtype)
        okvq[s, :, W:2 * W] = v.astype(okvq.dtype)
        okvq[s, :, 2 * W:] = q.astype(okvq.dtype)
    for s in subs:
        ogb[s] = _gdn_gates(_dot(hb[s], wab[...]), alog_ref, dtb_ref)


def _kin(x3, mod, gain, ws, gdn_consts, n_lat_tiles):
    nb, ttot, d = x3.shape
    n_tiles = ttot // ROW_TILE
    widths = [W, 3 * W, LANES, W, 2 * W, 2 * W, d]
    dts = [F32, BF16, F32, BF16, BF16, BF16, BF16]
    per = ROW_TILE // SUBLANES
    last = ttot // SUBLANES - 1
    grp = KIN_GROUP
    return pl.pallas_call(
        functools.partial(_kin_kernel, n_lat_tiles=n_lat_tiles, n_tiles=n_tiles),
        grid=(nb // grp, n_tiles),
        in_specs=[_tile_spec(d, grp),
                  pl.BlockSpec((grp, SUBLANES, d), lambda b, j: (b, jnp.maximum(j * per - 1, 0), 0)),
                  pl.BlockSpec((grp, SUBLANES, d), lambda b, j: (b, jnp.minimum((j + 1) * per, last), 0)),
                  _mod_spec(nb, d, 0, n_lat_tiles, grp), _mod_spec(nb, d, 1, n_lat_tiles, grp),
                  _resident((1, d))] + [_resident(w.shape) for w in ws]
                 + [_resident(t.shape) for t in gdn_consts],
        out_specs=[_tile_spec(n, grp) for n in widths],
        out_shape=[jax.ShapeDtypeStruct((nb, ttot, n), dt) for n, dt in zip(widths, dts)],
        compiler_params=_cparams("parallel", "parallel"),
        name="kin",
    )(x3, x3, x3, mod, mod, gain, *ws, *gdn_consts)


def _s5_kernel(uf_ref, ub_ref, perm_ref, permt_ref, bmf, bmb, cmf, cmb, af_ref, ab_ref,
               yf_ref, yb_ref, st_ref, bu_ref, s_ref):
    i = pl.program_id(0)
    nb, tt, _ = uf_ref.shape
    sub = SUBLANES
    n_grp, _, gw = bmf.shape
    half = gw // 2

    @pl.when(i == 0)
    def _():
        st_ref[...] = jnp.zeros_like(st_ref)

    dirs = ((uf_ref, bmf, cmf, af_ref, yf_ref, range(tt)),
            (ub_ref, bmb, cmb, ab_ref, yb_ref, range(tt - 1, -1, -1)))

    def regroup(dr):
        u_ref = dirs[dr][0]
        parts = []
        for k in range(tt // sub):
            blk = u_ref[:, k * sub:(k + 1) * sub, :].reshape(nb * sub, W).astype(BF16)
            parts.append(_dot(perm_ref[...], blk).astype(BF16))
        return jnp.concatenate(parts, axis=0)

    u_tb = [regroup(0), regroup(1)]
    y_tb = [None, None]

    def project(dr, g):
        bu_ref[dr, g] = _dot(u_tb[dr], dirs[dr][1][g])

    def recur(dr, g):
        a_ref, order = dirs[dr][3], dirs[dr][5]
        for c in range(half // LANES):
            re = slice(c * LANES, (c + 1) * LANES)
            im = slice(half + c * LANES, half + (c + 1) * LANES)
            a_re = a_ref[g, :, re]
            a_im = a_ref[g, :, im]
            s_re = st_ref[dr, g, :, re]
            s_im = st_ref[dr, g, :, im]
            for t in order:
                r = slice(t * nb, (t + 1) * nb)
                n_re = a_re * s_re - a_im * s_im + bu_ref[dr, g, r, re]
                n_im = a_re * s_im + a_im * s_re + bu_ref[dr, g, r, im]
                s_re, s_im = n_re, n_im
                s_ref[dr, g, r, re] = s_re.astype(BF16)
                s_ref[dr, g, r, im] = s_im.astype(BF16)
            st_ref[dr, g, :, re] = s_re
            st_ref[dr, g, :, im] = s_im

    def read_out(dr, g):
        t = _dot(s_ref[dr, g], dirs[dr][2][g])
        y_tb[dr] = t if y_tb[dr] is None else y_tb[dr] + t

    for dr in range(2):
        project(dr, 0)
    for g in range(n_grp):
        for dr in range(2):
            if g + 1 < n_grp:
                project(dr, g + 1)
            recur(dr, g)
        for dr in range(2):
            read_out(dr, g)
    for dr in range(2):
        y_ref = dirs[dr][4]
        for k in range(tt // sub):
            y_bt = _dot(permt_ref[...], y_tb[dr][k * sub * nb:(k + 1) * sub * nb].astype(BF16))
            y_ref[:, k * sub:(k + 1) * sub, :] = y_bt.reshape(nb, sub, W)


def _s5(u5, perm, permt, bmats, cmats, avecs, n_lat_tiles):
    nb, ttot, _ = u5.shape
    tt = S5_STEPS
    n_tiles = ttot // tt
    fwd = lambda i: (0, (i + n_lat_tiles) % n_tiles, 0)
    bwd = lambda i: (0, n_tiles - 1 - i, 0)
    rows = tt * nb
    n_grp, _, gw = bmats[0].shape
    return pl.pallas_call(
        _s5_kernel,
        grid=(n_tiles,),
        in_specs=[pl.BlockSpec((nb, tt, W), fwd), pl.BlockSpec((nb, tt, W), bwd),
                  _resident(perm.shape), _resident(permt.shape),
                  _resident(bmats[0].shape), _resident(bmats[1].shape),
                  _resident(cmats[0].shape), _resident(cmats[1].shape),
                  _resident(avecs[0].shape), _resident(avecs[1].shape)],
        out_specs=[pl.BlockSpec((nb, tt, W), fwd), pl.BlockSpec((nb, tt, W), bwd)],
        out_shape=[jax.ShapeDtypeStruct((nb, ttot, W), F32)] * 2,
        scratch_shapes=[pltpu.VMEM((2, n_grp, nb, gw), F32),
                        pltpu.VMEM((2, n_grp, rows, gw), F32),
                        pltpu.VMEM((2, n_grp, rows, gw), BF16)],
        compiler_params=_cparams("arbitrary"),
        name="s5",
    )(u5, u5, perm, permt, bmats[0], bmats[1], cmats[0], cmats[1], avecs[0], avecs[1])


def _fnet_kernel(a_ref, cos_ref, nsin_ref, y_ref):
    y = _dot(cos_ref[...], a_ref[:, :W]) + _dot(nsin_ref[...], a_ref[:, W:])
    y_ref[...] = y.astype(y_ref.dtype)


def _fnet_alias_kernel(a_ref, cos_ref, nsin_ref, prev_ref, y_ref):
    _fnet_kernel(a_ref, cos_ref, nsin_ref, y_ref)


def _fnet(afn, cos_l, nsin_l, cos_c, nsin_c, l_lat, l_ctx):
    nb, ttot, _ = afn.shape
    lat = pl.pallas_call(
        _fnet_kernel,
        grid=(nb,),
        in_specs=[pl.BlockSpec((None, l_lat, 2 * W), lambda b: (b, 0, 0)),
                  _resident(cos_l.shape), _resident(nsin_l.shape)],
        out_specs=pl.BlockSpec((None, l_lat, W), lambda b: (b, 0, 0)),
        out_shape=jax.ShapeDtypeStruct((nb, ttot, W), BF16),
        compiler_params=_cparams("parallel"),
        name="fnet_lat",
    )(afn, cos_l, nsin_l)
    blk = l_lat // l_ctx
    return pl.pallas_call(
        _fnet_alias_kernel,
        grid=(nb,),
        in_specs=[pl.BlockSpec((None, l_ctx, 2 * W), lambda b: (b, blk, 0)),
                  _resident(cos_c.shape), _resident(nsin_c.shape),
                  pl.BlockSpec(memory_space=pl.ANY)],
        out_specs=pl.BlockSpec((None, l_ctx, W), lambda b: (b, blk, 0)),
        out_shape=jax.ShapeDtypeStruct((nb, ttot, W), BF16),
        input_output_aliases={3: 0},
        compiler_params=_cparams("parallel"),
        name="fnet_ctx",
    )(afn, cos_c, nsin_c, lat)


def _sgu_kernel(uv_ref, w_ref, bias_ref, lng_ref, lnb_ref, avg_ref, y_ref):
    lane_group = lax.broadcasted_iota(jnp.int32, (SG_CHUNK, W), 1) // SG_GROUP
    rows = [slice(n * SG_CHUNK, (n + 1) * SG_CHUNK) for n in range(uv_ref.shape[0] // SG_CHUNK)]
    v = [_gelu(uv_ref[r, W:].astype(F32)) for r in rows]
    dv = [t - _dot_sel(t, avg_ref[...], 2) for t in v]
    var = [_dot_sel(t * t, avg_ref[...], 2) for t in dv]
    vn = [(d_ * lax.rsqrt(s_ + EPS) * lng_ref[...] + lnb_ref[...]).astype(BF16) for d_, s_ in zip(dv, var)]
    sv = [_dot(w_ref[0], t) for t in vn]
    for g in range(1, SG_GROUPS):
        sv = [jnp.where(lane_group == g, _dot(w_ref[g], t), s_) for t, s_ in zip(vn, sv)]
    for r, s_ in zip(rows, sv):
        y_ref[r, :] = (_gelu(uv_ref[r, :W].astype(F32)) * (s_ + bias_ref[...])).astype(y_ref.dtype)


def _sgu(sg, w, bias, lng, lnb, avg, rows):
    nb, ttot, _ = sg.shape
    return pl.pallas_call(
        _sgu_kernel,
        grid=(nb, ttot // rows),
        in_specs=[pl.BlockSpec((None, rows, 2 * W), lambda b, n: (b, n, 0)),
                  _resident(w.shape), _resident(bias.shape), _resident(lng.shape),
                  _resident(lnb.shape), _resident(avg.shape)],
        out_specs=pl.BlockSpec((None, rows, W), lambda b, n: (b, n, 0)),
        out_shape=jax.ShapeDtypeStruct((nb, ttot, W), BF16),
        compiler_params=_cparams("parallel", "parallel"),
        name="sgu",
    )(sg, w, bias, lng, lnb, avg)


def _gdn_masks(backward):
    c, heads = GDN_CHUNK, GDN_HEADS
    i = np.arange(c)[:, None]
    j = (np.arange(W) % c)[None, :]
    incl = (i <= j) if backward else (i >= j)
    strict = (i < j) if backward else (i > j)
    j64 = np.arange(c)[None, :]
    incl64 = (i <= j64) if backward else (i >= j64)
    same = lambda n: (i // n) == (j // n)
    levels = [same(GDN_BASE)]
    n = GDN_BASE
    while n < c:
        levels.append(same(2 * n) & ~same(n))
        n *= 2
    f = lambda m: jnp.asarray(m, F32)
    return (f(np.stack([incl, strict, i == j])), jnp.asarray(incl64, BF16), f(np.stack(levels)))


def _gdn_chains(chains, bd_ref, bd16_ref):
    c, heads = GDN_CHUNK, GDN_HEADS
    each = lambda f, *cols: [f(*args) for args in zip(*cols)]

    def expand_heads(t):
        t = t.astype(BF16)
        return jnp.concatenate([t] * heads, axis=0) * bd16_ref[:, :t.shape[1]]

    kvq, gb, expand, m64, incl64, lvls, s_refs = zip(*chains)
    kvq = [t.astype(F32) for t in kvq]
    k = [t[:, :W] for t in kvq]
    v = [t[:, W:2 * W] for t in kvq]
    q = [t[:, 2 * W:] for t in kvq]
    def expand_gates(g, e):
        hi, lo = _split(g, 2)
        t = _dot(jnp.concatenate([hi, lo], axis=0), e[...])
        return t[:c] + t[c:]

    ge = each(expand_gates, gb, expand)
    g_l = [t[:, :W] for t in ge]
    beta_l = [t[:, W:] for t in ge]
    incl = [m[0] for m in m64]
    strict = [m[1] for m in m64]
    diag = [m[2] for m in m64]

    def cumulative(g, i64):
        acc = None
        for piece in _split(g, 2):
            t = _dot(i64[...], piece)
            acc = t if acc is None else acc + t
        return acc

    gc = each(cumulative, g_l, incl64)
    dif = each(lambda g, dg: g - jnp.sum(g * dg, axis=0, keepdims=True), gc, diag)
    g_tot = [jnp.sum(t, axis=0, keepdims=True) for t in g_l]
    kb = each(lambda a_, b_: a_ * b_, k, beta_l)
    k_st = [expand_heads(t) for t in k]
    kq = each(lambda a_, b_, st: lax.dot_general(
        jnp.concatenate([a_, b_], axis=0).astype(BF16), st, (((1,), (1,)), ((), ())),
        preferred_element_type=F32), kb, q, k_st)
    rel = each(lambda d_, i: jnp.exp(d_ * i) * i, dif, incl)
    a = each(lambda t, r, st: t[:c] * r * st, kq, rel, strict)
    qk = each(lambda t, r: t[c:] * r, kq, rel)
    pw = each(lambda t, lv: t * lv[0], a, lvls)
    t_inv = each(lambda dg, t: dg - t, diag, pw)
    pw_bd = [expand_heads(t) for t in pw]
    for _ in range(int(math.log2(GDN_BASE)) - 1):
        pw = each(lambda t, bd_: _dot(t.astype(BF16), bd_), pw, pw_bd)
        pw_bd = [expand_heads(t) for t in pw]
        t_inv = each(lambda t, bd_: t + _dot(t.astype(BF16), bd_), t_inv, pw_bd)
    for lvl in range(1, lvls[0].shape[0]):
        z_bd = each(lambda t, lv: expand_heads(t * lv[lvl]), a, lvls)
        tz = each(lambda t, z_: _dot(t.astype(BF16), z_), t_inv, z_bd)
        t_inv = each(lambda t, tz_: t - _dot(tz_.astype(BF16), expand_heads(t)), t_inv, tz)
    e_gc = [jnp.exp(t) for t in gc]
    s = [r[...] for r in s_refs]
    sb = [t.astype(BF16) for t in s]
    ws = each(lambda kb_, e, q_, sb_: _dot(jnp.concatenate([kb_ * e, q_ * e], axis=0).astype(BF16), sb_),
              kb, e_gc, q, sb)
    rhs = each(lambda v_, b_, t: expand_heads(v_ * b_ - t[:c]), v, beta_l, ws)
    u = each(lambda t, r: _dot(t.astype(BF16), r), t_inv, rhs)
    o = each(lambda t, qk_, u_: t[c:] + _dot(qk_.astype(BF16), expand_heads(u_)), ws, qk, u)
    k_dec = each(lambda k_, gt, g: k_ * jnp.exp(gt - g), k, g_tot, gc)
    upd = each(lambda kd, u_: lax.dot_general(kd.astype(BF16), u_.astype(BF16), (((0,), (0,)), ((), ())),
                                              preferred_element_type=F32), k_dec, u)
    for r, s_, gt, up in zip(s_refs, s, g_tot, upd):
        r[...] = s_ * jnp.exp(gt) + up * bd_ref[...]
    return o


def _gdn_chunk_kernel(kvqf_ref, kvqb_ref, gbf_ref, gbb_ref, ef_ref, eb_ref, m64f_ref, m64b_ref,
                      i64f_ref, i64b_ref, lvlf_ref, lvlb_ref, bd_ref, bd16_ref, of_ref, ob_ref, s_ref):
    @pl.when(pl.program_id(1) == 0)
    def _():
        s_ref[...] = jnp.zeros_like(s_ref)

    chains = []
    for bi in range(kvqf_ref.shape[0]):
        chains.append((kvqf_ref[bi], gbf_ref[bi], ef_ref, m64f_ref, i64f_ref, lvlf_ref, s_ref.at[bi, 0]))
        chains.append((kvqb_ref[bi], gbb_ref[bi], eb_ref, m64b_ref, i64b_ref, lvlb_ref, s_ref.at[bi, 1]))
    outs = _gdn_chains(chains, bd_ref, bd16_ref)
    for bi in range(kvqf_ref.shape[0]):
        of_ref[bi] = outs[2 * bi].astype(of_ref.dtype)
        ob_ref[bi] = outs[2 * bi + 1].astype(ob_ref.dtype)


def _gdn_chunk(kvq, gb, expand, masks, bd, bd16, l_lat, per_step):
    c = GDN_CHUNK
    nb, ttot, _ = kvq.shape
    n_lat = l_lat // c
    n_chunks = ttot // c
    fwd = lambda b, i: (b, (i + n_lat) % n_chunks, 0)
    bwd = lambda b, i: (b, n_chunks - 1 - i, 0)
    consts = [expand[0], expand[1], masks[0][0], masks[1][0], masks[0][1], masks[1][1],
              masks[0][2], masks[1][2], bd, bd16]
    return pl.pallas_call(
        _gdn_chunk_kernel,
        grid=(nb // per_step, n_chunks),
        in_specs=[pl.BlockSpec((per_step, c, 3 * W), fwd), pl.BlockSpec((per_step, c, 3 * W), bwd),
                  pl.BlockSpec((per_step, c, LANES), fwd), pl.BlockSpec((per_step, c, LANES), bwd)]
                 + [_resident(t.shape) for t in consts],
        out_specs=[pl.BlockSpec((per_step, c, W), fwd), pl.BlockSpec((per_step, c, W), bwd)],
        out_shape=[jax.ShapeDtypeStruct((nb, ttot, W), BF16)] * 2,
        scratch_shapes=[pltpu.VMEM((per_step, 2, W, W), F32)],
        compiler_params=_cparams("parallel", "arbitrary"),
        name="gdn_chunk",
    )(kvq, kvq, gb, gb, *consts)


def _merge_kernel(x_ref, h_ref, g1_ref, yf_ref, yb_ref, u5_ref, dskip_ref, gluw_ref,
                  glub_ref, of_ref, ob_ref, z_ref, gain_ref, avg_ref, yfn_ref, ysg_ref, wgate_ref,
                  wbr_ref, wout_ref, o_ref):
    d = x_ref.shape[2]
    subs = range(x_ref.shape[0])
    y5 = [_gelu(yf_ref[s] + yb_ref[s] + dskip_ref[...] * u5_ref[s]) for s in subs]
    glu = [_dot(t.astype(BF16), gluw_ref[...]) for t in y5]
    o = [of_ref[s].astype(F32) + ob_ref[s].astype(F32) for s in subs]
    ms = [_dot_sel(t * t, avg_ref[...], 2) for t in o]
    y5 = [t * _sigmoid(g_ + glub_ref[...]) for t, g_ in zip(y5, glu)]
    yg = [o[s] * lax.rsqrt(ms[s] + EPS) * gain_ref[...] * _silu(z_ref[s].astype(F32)) for s in subs]
    ys = [(y5[s], yfn_ref[s], yg[s], ysg_ref[s]) for s in subs]
    acc = [None for _ in subs]
    for j in range(N_BRANCH):
        for s in subs:
            th = jnp.tanh(_dot(h_ref[s], wgate_ref[:, j * d:(j + 1) * d]))
            b_half = _dot(ys[s][j].astype(BF16), wbr_ref[j])
            t = th * b_half + b_half
            acc[s] = t if acc[s] is None else acc[s] + t
    out = [_dot(t.astype(BF16), wout_ref[...]) for t in acc]
    for s in subs:
        o_ref[s] = x_ref[s] + g1_ref[s] * out[s]


def _merge(x3, mod, acts, consts, n_lat_tiles, n_tiles):
    nb, ttot, d = x3.shape
    h, yf, yb, u5, of, ob, z, yfn, ysg = acts
    dskip, gluw, glub, gain, avg, wgate, wbr, wout = consts
    return pl.pallas_call(
        _merge_kernel,
        grid=(nb // PAIR, n_tiles),
        in_specs=[_tile_spec(d), _tile_spec(d), _mod_spec(nb, d, 2, n_lat_tiles),
                  _tile_spec(W), _tile_spec(W), _tile_spec(W),
                  _resident(dskip.shape), _resident(gluw.shape), _resident(glub.shape),
                  _tile_spec(W), _tile_spec(W), _tile_spec(W),
                  _resident(gain.shape), _resident(avg.shape),
                  _tile_spec(W), _tile_spec(W), _resident(wgate.shape),
                  _resident(wbr.shape), _resident(wout.shape)],
        out_specs=_tile_spec(d),
        out_shape=jax.ShapeDtypeStruct((nb, n_tiles * ROW_TILE, d), F32),
        compiler_params=_cparams("parallel", "parallel"),
        name="merge",
    )(x3, h, mod, yf, yb, u5, dskip, gluw, glub, of, ob, z, gain, avg, yfn, ysg, wgate, wbr, wout)


def _ffn_kernel(x_ref, sh_ref, sc_ref, g2_ref, gain_ref, w1_ref, w2_ref, nf_ref, o_ref, *, final):
    dff = w2_ref.shape[0]
    subs = range(x_ref.shape[0])
    h = [_modulated_norm(x_ref[s], gain_ref[...], sc_ref[s], sh_ref[s]).astype(BF16) for s in subs]
    t = [_dot(h_, w1_ref[...]) for h_ in h]
    act = [(_silu(t_[:, :dff]) * t_[:, dff:]).astype(BF16) for t_ in t]
    out = [_dot(a_, w2_ref[...]) for a_ in act]
    for s in subs:
        y = x_ref[s] + g2_ref[s] * out[s]
        if final:
            y = y * lax.rsqrt(jnp.mean(y * y, axis=-1, keepdims=True) + EPS) * nf_ref[...]
        o_ref[s] = y


def _ffn(x3, mod, gain, w1, w2, norm_f, n_lat_tiles, final):
    nb, ttot, d = x3.shape
    return pl.pallas_call(
        functools.partial(_ffn_kernel, final=final),
        grid=(nb // PAIR, ttot // ROW_TILE),
        in_specs=[_tile_spec(d), _mod_spec(nb, d, 3, n_lat_tiles), _mod_spec(nb, d, 4, n_lat_tiles),
                  _mod_spec(nb, d, 5, n_lat_tiles),
                  _resident(gain.shape), _resident(w1.shape), _resident(w2.shape),
                  _resident(norm_f.shape)],
        out_specs=_tile_spec(d),
        out_shape=jax.ShapeDtypeStruct(x3.shape, F32),
        compiler_params=_cparams("parallel", "parallel"),
        name="ffn",
    )(x3, mod, mod, mod, gain, w1, w2, norm_f)


def _dft_tables(n):
    idx = np.arange(n, dtype=np.int64)
    ang = 2.0 * np.pi * ((idx[:, None] * idx[None, :]) % n).astype(np.float64) / n
    return np.cos(ang), np.sin(ang)


def _s5_tables(lam_re, lam_im, log_dt, b_re, b_im, c_re, c_im):
    g, p, cg = S5_GROUPS, S5_STATE, S5_GROUP
    lam = lax.complex(lam_re.astype(F32), lam_im.astype(F32))
    a_bar = jnp.exp(lam * jnp.exp(log_dt.astype(F32))[:, None])
    b_bar = ((a_bar - 1.0) / lam)[..., None] * lax.complex(b_re.astype(F32), b_im.astype(F32))
    eye = jnp.eye(g, dtype=F32)
    bm_re = jnp.einsum('gpc,gh->gchp', b_bar.real, eye).reshape(g * cg, g * p)
    bm_im = jnp.einsum('gpc,gh->gchp', b_bar.imag, eye).reshape(g * cg, g * p)
    cm_re = jnp.einsum('gcp,gh->gphc', c_re.astype(F32), eye).reshape(g * p, g * cg)
    cm_im = jnp.einsum('gcp,gh->gphc', c_im.astype(F32), eye).reshape(g * p, g * cg)
    ng, h = N_S5 // S5_LANE_GROUP, S5_LANE_GROUP
    bmat = jnp.concatenate([bm_re.reshape(W, ng, h), bm_im.reshape(W, ng, h)], axis=2)
    bmat = jnp.transpose(bmat, (1, 0, 2)).astype(BF16)
    cmat = jnp.concatenate([cm_re.reshape(ng, h, W), -cm_im.reshape(ng, h, W)], axis=1).astype(BF16)
    avec = jnp.concatenate([a_bar.real.reshape(ng, 1, h), a_bar.imag.reshape(ng, 1, h)], axis=2)
    return bmat, cmat, avec


def _layer_consts(l, p):
    d = p['w_in'].shape[1]
    w_in = p['w_in'][l]
    sizes = (W, W, W, 2 * GDN_HEADS, 2 * GDN_HEADS, W, W, W, W, W, N_BRANCH * d)
    offs = np.concatenate([[0], np.cumsum(sizes)])
    col = lambda j: w_in[:, offs[j]:offs[j + 1]]
    u5, k, v, a, bt, q, z, ufn, usg, vsg, gate = [col(j) for j in range(len(sizes))]
    cc, sc = _dft_tables(FN_GROUP)
    fw = p['fn_w'][l].astype(F32)
    wc = jnp.einsum('cd,gde->gce', jnp.asarray(cc, F32), fw, precision=HIGHEST)
    ws = jnp.einsum('cd,gde->gce', jnp.asarray(sc, F32), fw, precision=HIGHEST)
    ufn_g = ufn.reshape(d, FN_GROUPS, FN_GROUP)
    fold_c = jnp.einsum('kgc,gce->kge', ufn_g, wc, precision=HIGHEST).reshape(d, W)
    fold_s = jnp.einsum('kgc,gce->kge', ufn_g, ws, precision=HIGHEST).reshape(d, W)
    w_ab = jnp.concatenate([a, bt, jnp.zeros((d, LANES - 4 * GDN_HEADS), F32)], axis=1)
    ws_in = [u5, jnp.concatenate([k, v, q], axis=1), w_ab, z,
             jnp.concatenate([fold_c, fold_s], axis=1), jnp.concatenate([usg, vsg], axis=1)]
    ws_in = [w.astype(BF16) for w in ws_in]
    w_gate = (0.5 * gate).astype(BF16)

    s5 = [_s5_tables(p['s5_lam_re'][l, dr], p['s5_lam_im'][l, dr], p['s5_log_dt'][l, dr],
                     p['s5_b_re'][l, dr], p['s5_b_im'][l, dr], p['s5_c_re'][l, dr],
                     p['s5_c_im'][l, dr]) for dr in range(2)]

    conv = p['gdn_conv'][l].astype(F32)
    cw = jnp.transpose(conv, (2, 0, 1)).reshape(conv.shape[2], 3 * W)
    pad = jnp.zeros((1, LANES - 2 * GDN_HEADS), F32)
    alog = jnp.concatenate([p['gdn_a_log'][l].astype(F32).reshape(1, -1), pad], axis=1)
    dtb = jnp.concatenate([p['gdn_dt_bias'][l].astype(F32).reshape(1, -1), pad], axis=1)

    sg_bias = jnp.repeat(p['sg_b'][l].astype(F32).T, SG_GROUP, axis=1)
    merge_consts = (p['s5_d'][l].astype(F32).reshape(1, W), p['s5_glu_w'][l].astype(BF16),
                    p['s5_glu_b'][l].astype(F32).reshape(1, W),
                    jnp.tile(p['gdn_norm'][l].astype(F32), GDN_HEADS).reshape(1, W))
    return dict(
        ws_in=ws_in, w_gate=w_gate, s5=s5, cw=cw, alog=alog, dtb=dtb,
        sg_w=p['sg_w'][l].astype(BF16), sg_bias=sg_bias,
        sg_lng=p['sg_ln_g'][l].astype(F32).reshape(1, W), sg_lnb=p['sg_ln_b'][l].astype(F32).reshape(1, W),
        merge=merge_consts, wbr_half=(0.5 * p['w_branch'][l]).astype(BF16),
        wout=p['w_out'][l].astype(BF16),
        w1=p['ffn_w1'][l].astype(BF16), w2=p['ffn_w2'][l].astype(BF16),
        norm1=p['norm1'][l].astype(F32).reshape(1, d), norm2=p['norm2'][l].astype(F32).reshape(1, d))


def kernel(x, c, ctx, c_ctx, ada_w, ada_b, norm1, norm2, w_in, s5_lam_re, s5_lam_im, s5_log_dt, s5_b_re, s5_b_im, s5_c_re, s5_c_im, s5_d, s5_glu_w, s5_glu_b, fn_w, gdn_conv, gdn_a_log, gdn_dt_bias, gdn_norm, sg_ln_g, sg_ln_b, sg_w, sg_b, w_branch, w_out, ffn_w1, ffn_w2, norm_f):
    p = dict(w_in=w_in, s5_lam_re=s5_lam_re, s5_lam_im=s5_lam_im, s5_log_dt=s5_log_dt,
             s5_b_re=s5_b_re, s5_b_im=s5_b_im, s5_c_re=s5_c_re, s5_c_im=s5_c_im, s5_d=s5_d,
             s5_glu_w=s5_glu_w, s5_glu_b=s5_glu_b, fn_w=fn_w, gdn_conv=gdn_conv,
             gdn_a_log=gdn_a_log, gdn_dt_bias=gdn_dt_bias, gdn_norm=gdn_norm, sg_ln_g=sg_ln_g,
             sg_ln_b=sg_ln_b, sg_w=sg_w, sg_b=sg_b, w_branch=w_branch, w_out=w_out,
             ffn_w1=ffn_w1, ffn_w2=ffn_w2, norm1=norm1, norm2=norm2)
    nb, l_lat, d = x.shape
    l_ctx = ctx.shape[1]
    depth = ada_w.shape[0]
    ttot = l_lat + l_ctx
    assert nb % SUBLANES == 0 and nb % PAIR == 0 and nb % KIN_GROUP == 0
    assert d == N_BRANCH * W and l_lat % l_ctx == 0
    assert l_lat % ROW_TILE == 0 and l_ctx % ROW_TILE == 0 and ROW_TILE % SG_CHUNK == 0
    n_lat_tiles = l_lat // ROW_TILE
    n_tiles = ttot // ROW_TILE
    sg_rows = max(r for r in range(SG_CHUNK, 7 * SG_CHUNK, SG_CHUNK) if ttot % r == 0)

    x3 = jnp.concatenate([x, ctx], axis=1).astype(F32)

    n_cond = nb + max(PAIR, KIN_GROUP)
    ada_rows = -(-n_cond // SUBLANES) * SUBLANES
    cc = jnp.zeros((ada_rows, d), F32).at[:nb].set(c.astype(F32)).at[nb:n_cond].set(c_ctx.astype(F32))
    mods = _ada(cc, ada_w.astype(F32), ada_b.astype(F32))[:, :n_cond].reshape(depth, n_cond, 1, 6 * d)

    cos_l, sin_l = _dft_tables(l_lat)
    cos_c, sin_c = _dft_tables(l_ctx)
    sc_l = 1.0 / math.sqrt(l_lat * FN_GROUP)
    sc_c = 1.0 / math.sqrt(l_ctx * FN_GROUP)
    cos_l, nsin_l = jnp.asarray(cos_l * sc_l, BF16), jnp.asarray(-sin_l * sc_l, BF16)
    cos_c, nsin_c = jnp.asarray(cos_c * sc_c, BF16), jnp.asarray(-sin_c * sc_c, BF16)
    nf = norm_f.astype(F32).reshape(1, d)

    lane_head = np.arange(W) // GDN_HEAD_DIM
    seg = (lane_head[:, None] == lane_head[None, :]).astype(np.float32)
    seg_ones = jnp.asarray(seg, BF16)
    seg_avg = jnp.asarray(seg / GDN_HEAD_DIM, BF16)
    expand = []
    for dr in range(2):
        e = np.zeros((LANES, 2 * W), np.float32)
        for h in range(GDN_HEADS):
            e[dr * GDN_HEADS + h, :W] = lane_head == h
            e[2 * GDN_HEADS + dr * GDN_HEADS + h, W:] = lane_head == h
        expand.append(jnp.asarray(e, BF16))
    gdn_masks = [_gdn_masks(False), _gdn_masks(True)]
    bd = jnp.asarray(seg, F32)
    bd16 = jnp.asarray(np.concatenate([seg, seg], axis=1), BF16)
    src = (np.arange(nb)[None, :] * SUBLANES + np.arange(SUBLANES)[:, None]).reshape(-1)
    perm_np = np.zeros((nb * SUBLANES, nb * SUBLANES), np.float32)
    perm_np[np.arange(nb * SUBLANES), src] = 1.0
    perm, permt = jnp.asarray(perm_np, BF16), jnp.asarray(perm_np.T, BF16)

    for l in range(depth):
        k = _layer_consts(l, p)
        mod = mods[l]
        last = l == depth - 1
        u5, kvq_p, gb, z, afn, sg, h = _kin(x3, mod, k['norm1'], k['ws_in'],
                                            (k['cw'], seg_ones, k['alog'], k['dtb']), n_lat_tiles)
        yf, yb = _s5(u5, perm, permt, [t[0] for t in k['s5']], [t[1] for t in k['s5']],
                     [t[2] for t in k['s5']], l_lat // S5_STEPS)
        yfn = _fnet(afn, cos_l, nsin_l, cos_c, nsin_c, l_lat, l_ctx)
        ysg = _sgu(sg, k['sg_w'], k['sg_bias'], k['sg_lng'], k['sg_lnb'], seg_avg, sg_rows)
        of, ob = _gdn_chunk(kvq_p, gb, expand, gdn_masks, bd, bd16, l_lat, 8)
        acts = (h, yf, yb, u5, of, ob, z, yfn, ysg)
        consts = k['merge'] + (seg_avg, k['w_gate'], k['wbr_half'], k['wout'])
        x3 = _merge(x3, mod, acts, consts, n_lat_tiles, n_lat_tiles if last else n_tiles)
        x3 = _ffn(x3, mod, k['norm2'], k['w1'], k['w2'], nf, n_lat_tiles, final=last)

    return x3.astype(x.dtype)
```

```python
import functools
import math

import numpy as np
import jax
import jax.numpy as jnp
from jax import lax
from jax.experimental import pallas as pl
from jax.experimental.pallas import tpu as pltpu

F32 = jnp.float32
BF16 = jnp.bfloat16
HIGHEST = lax.Precision.HIGHEST

EPS = 1e-6
W = 256
N_BRANCH = 4
S5_GROUP = 16
S5_GROUPS = W // S5_GROUP
S5_STATE = 64
N_S5 = S5_GROUPS * S5_STATE
FN_GROUPS = 4
FN_GROUP = W // FN_GROUPS
GDN_HEADS = 4
GDN_HEAD_DIM = W // GDN_HEADS
GDN_CHUNK = 64
GDN_BASE = 8
SG_GROUPS = 4
SG_GROUP = W // SG_GROUPS
SG_CHUNK = 128
LANES = 128
SUBLANES = 8
ROW_TILE = 256
PAIR = 2
KIN_GROUP = 4
S5_STEPS = 32
S5_LANE_GROUP = 256
VMEM_LIMIT = 56 * 1024 * 1024


def _cparams(*sem):
    return pltpu.CompilerParams(dimension_semantics=sem, vmem_limit_bytes=VMEM_LIMIT)


def _resident(shape):
    nd = len(shape)
    return pl.BlockSpec(shape, lambda *_: (0,) * nd, pipeline_mode=pl.Buffered(1))


def _dot(a, b):
    return jnp.dot(a, b, preferred_element_type=F32)


def _split(x, parts):
    out = []
    for _ in range(parts - 1):
        hi = x.astype(BF16)
        out.append(hi)
        x = x - hi.astype(F32)
    out.append(x.astype(BF16))
    return out


def _dot_sel(x, sel, parts):
    acc = None
    for piece in _split(x, parts):
        t = _dot(piece, sel)
        acc = t if acc is None else acc + t
    return acc


def _gelu(x):
    return 0.5 * x * (1.0 + jnp.tanh(math.sqrt(2.0 / math.pi) * (x + 0.044715 * (x * x * x))))


def _sigmoid(x):
    return 0.5 * jnp.tanh(0.5 * x) + 0.5


def _silu(x):
    return x * _sigmoid(x)


def _modulated_norm(x, gain, scale, shift):
    y = x * lax.rsqrt(jnp.mean(x * x, axis=-1, keepdims=True) + EPS) * gain
    return y * (1.0 + scale) + shift


def _ada_kernel(c_ref, w_ref, b_ref, o_ref):
    c = c_ref[...]
    o_ref[0] = jnp.dot(_silu(c), w_ref[0], preferred_element_type=F32, precision=HIGHEST) + b_ref[0]


def _ada(cc, ada_w, ada_b):
    depth, d, n = ada_w.shape
    rows = cc.shape[0]
    return pl.pallas_call(
        _ada_kernel,
        grid=(depth, n // d),
        in_specs=[pl.BlockSpec((rows, d), lambda l, j: (0, 0)),
                  pl.BlockSpec((1, d, d), lambda l, j: (l, 0, j)),
                  pl.BlockSpec((1, 1, d), lambda l, j: (l, 0, j))],
        out_specs=pl.BlockSpec((1, rows, d), lambda l, j: (l, 0, j)),
        out_shape=jax.ShapeDtypeStruct((depth, rows, n), F32),
        compiler_params=_cparams("parallel", "parallel"),
        name="ada",
    )(cc, ada_w, ada_b.reshape(depth, 1, n))


def _tile_spec(n, group=PAIR):
    return pl.BlockSpec((group, ROW_TILE, n), lambda b, j: (b, j, 0))


def _mod_spec(nb, d, k, n_lat_tiles, group=PAIR):
    return pl.BlockSpec((group, 1, d), lambda b, j: (jnp.where(j < n_lat_tiles, b, nb // group), 0, k))


def _gdn_features(kvq, before, after, cw_ref, ones_ref):
    rows, n = kvq.shape
    row = lax.broadcasted_iota(jnp.int32, (rows, n), 0)
    xp = jnp.where(row == 0, before, pltpu.roll(kvq, 1, 0))
    xn = jnp.where(row == rows - 1, after, pltpu.roll(kvq, rows - 1, 0))
    y = _silu(xp * cw_ref[0:1, :] + kvq * cw_ref[1:2, :] + xn * cw_ref[2:3, :])
    k = y[:, :W]
    q = y[:, 2 * W:]
    k = k * lax.rsqrt(_dot_sel(k * k, ones_ref[...], 2) + EPS)
    q = q * lax.rsqrt(_dot_sel(q * q, ones_ref[...], 2) + EPS) * GDN_HEAD_DIM ** -0.5
    return k, y[:, W:2 * W], q


def _gdn_gates(ab, alog_ref, dtb_ref):
    z = ab + dtb_ref[...]
    softplus = jnp.maximum(z, 0.0) + jnp.log(1.0 + jnp.exp(-jnp.abs(z)))
    g = -jnp.exp(alog_ref[...]) * softplus
    lane = lax.broadcasted_iota(jnp.int32, ab.shape, 1)
    return jnp.where(lane < 2 * GDN_HEADS, g, _sigmoid(ab))


def _stream_specs(xs, d, group, n_lat_tiles):
    if len(xs) == 1:
        return [_tile_spec(d, group)]
    return [pl.BlockSpec((group, ROW_TILE, d), lambda b, j: (b, jnp.minimum(j, n_lat_tiles - 1), 0)),
            pl.BlockSpec((group, ROW_TILE, d), lambda b, j: (b, jnp.maximum(j - n_lat_tiles, 0), 0))]


def _stream_tile(x_ref, c_ref, s, n_lat_tiles):
    if c_ref is None:
        return x_ref[s]
    return jnp.where(pl.program_id(1) >= n_lat_tiles, c_ref[s], x_ref[s])


def _kin_kernel(*refs, n_lat_tiles, n_tiles, split):
    x_ref, c_ref = (refs[0], refs[1]) if split else (refs[0], None)
    (xp_ref, xn_ref, sh_ref, sc_ref, g_ref, w5, wkvq, wab, wz, wfn, wsg, cw_ref, ones_ref, alog_ref,
     dtb_ref, o5, okvq, ogb, oz, ofn, osg, oh) = refs[2 if split else 1:]
    j = pl.program_id(1)
    subs = range(x_ref.shape[0])
    halo = xp_ref.shape[1]
    hb = [_modulated_norm(_stream_tile(x_ref, c_ref, s, n_lat_tiles), g_ref[...], sc_ref[s],
                          sh_ref[s]).astype(BF16) for s in subs]
    edge = [jnp.concatenate([xp_ref[s], xn_ref[s]], axis=0) for s in subs]
    eb = [_modulated_norm(edge[s], g_ref[...], sc_ref[s], sh_ref[s]).astype(BF16) for s in subs]
    has_prev = jnp.logical_and(j != 0, j != n_lat_tiles).astype(F32)
    has_next = jnp.logical_and(j != n_lat_tiles - 1, j != n_tiles - 1).astype(F32)
    for s in subs:
        oh[s] = hb[s]
    for w_ref, o_ref in ((w5, o5), (wz, oz), (wfn, ofn), (wsg, osg)):
        for s in subs:
            o_ref[s] = _dot(hb[s], w_ref[...]).astype(o_ref.dtype)
    rows = x_ref.shape[1]
    kvq = [_dot(jnp.concatenate([hb[s], eb[s]], axis=0), wkvq[...]) for s in subs]
    for s in subs:
        k, v, q = _gdn_features(kvq[s][:rows], kvq[s][rows + halo - 1:rows + halo, :] * has_prev,
                                kvq[s][rows + halo:rows + halo + 1, :] * has_next, cw_ref, ones_ref)
        okvq[s, :, :W] = k.astype(okvq.dtype)
        okvq[s, :, W:2 * W] = v.astype(okvq.dtype)
        okvq[s, :, 2 * W:] = q.astype(okvq.dtype)
    for s in subs:
        ogb[s] = _gdn_gates(_dot(hb[s], wab[...]), alog_ref, dtb_ref)


def _kin(xs, mod, gain, ws, gdn_consts, n_lat_tiles, n_tiles):
    nb, _, d = xs[0].shape
    ttot = n_tiles * ROW_TILE
    widths = [W, 3 * W, LANES, W, 2 * W, 2 * W, d]
    dts = [F32, BF16, F32, BF16, BF16, BF16, BF16]
    per = ROW_TILE // SUBLANES
    last = xs[0].shape[1] // SUBLANES - 1
    grp = KIN_GROUP
    return pl.pallas_call(
        functools.partial(_kin_kernel, n_lat_tiles=n_lat_tiles, n_tiles=n_tiles, split=len(xs) == 2),
        grid=(nb // grp, n_tiles),
        in_specs=_stream_specs(xs, d, grp, n_lat_tiles)
                 + [pl.BlockSpec((grp, SUBLANES, d), lambda b, j: (b, jnp.clip(j * per - 1, 0, last), 0)),
                    pl.BlockSpec((grp, SUBLANES, d), lambda b, j: (b, jnp.clip((j + 1) * per, 0, last), 0)),
                    _mod_spec(nb, d, 0, n_lat_tiles, grp), _mod_spec(nb, d, 1, n_lat_tiles, grp),
                    _resident((1, d))] + [_resident(w.shape) for w in ws]
                 + [_resident(t.shape) for t in gdn_consts],
        out_specs=[_tile_spec(n, grp) for n in widths],
        out_shape=[jax.ShapeDtypeStruct((nb, ttot, n), dt) for n, dt in zip(widths, dts)],
        compiler_params=_cparams("parallel", "parallel"),
        name="kin",
    )(*xs, xs[0], xs[0], mod, mod, gain, *ws, *gdn_consts)


def _s5_kernel(uf_ref, ub_ref, perm_ref, permt_ref, bmf, bmb, cmf, cmb, af_ref, ab_ref,
               yf_ref, yb_ref, st_ref, bu_ref, s_ref):
    i = pl.program_id(0)
    nb, tt, _ = uf_ref.shape
    sub = SUBLANES
    n_grp, _, gw = bmf.shape
    half = gw // 2

    @pl.when(i == 0)
    def _():
        st_ref[...] = jnp.zeros_like(st_ref)

    dirs = ((uf_ref, bmf, cmf, af_ref, yf_ref, range(tt)),
            (ub_ref, bmb, cmb, ab_ref, yb_ref, range(tt - 1, -1, -1)))

    def regroup(dr):
        u_ref = dirs[dr][0]
        parts = []
        for k in range(tt // sub):
            blk = u_ref[:, k * sub:(k + 1) * sub, :].reshape(nb * sub, W).astype(BF16)
            parts.append(_dot(perm_ref[...], blk).astype(BF16))
        return jnp.concatenate(parts, axis=0)

    u_tb = [regroup(0), regroup(1)]
    y_tb = [None, None]

    def project(dr, g):
        bu_ref[dr, g] = _dot(u_tb[dr], dirs[dr][1][g])

    def recur(dr, g):
        a_ref, order = dirs[dr][3], dirs[dr][5]
        for c in range(half // LANES):
            re = slice(c * LANES, (c + 1) * LANES)
            im = slice(half + c * LANES, half + (c + 1) * LANES)
            a_re = a_ref[g, :, re]
            a_im = a_ref[g, :, im]
            s_re = st_ref[dr, g, :, re]
            s_im = st_ref[dr, g, :, im]
            for t in order:
                r = slice(t * nb, (t + 1) * nb)
                n_re = a_re * s_re - a_im * s_im + bu_ref[dr, g, r, re]
                n_im = a_re * s_im + a_im * s_re + bu_ref[dr, g, r, im]
                s_re, s_im = n_re, n_im
                s_ref[dr, g, r, re] = s_re.astype(BF16)
                s_ref[dr, g, r, im] = s_im.astype(BF16)
            st_ref[dr, g, :, re] = s_re
            st_ref[dr, g, :, im] = s_im

    def read_out(dr, g):
        t = _dot(s_ref[dr, g], dirs[dr][2][g])
        y_tb[dr] = t if y_tb[dr] is None else y_tb[dr] + t

    for dr in range(2):
        project(dr, 0)
    for g in range(n_grp):
        for dr in range(2):
            if g + 1 < n_grp:
                project(dr, g + 1)
            recur(dr, g)
        for dr in range(2):
            read_out(dr, g)
    for dr in range(2):
        y_ref = dirs[dr][4]
        for k in range(tt // sub):
            y_bt = _dot(permt_ref[...], y_tb[dr][k * sub * nb:(k + 1) * sub * nb].astype(BF16))
            y_ref[:, k * sub:(k + 1) * sub, :] = y_bt.reshape(nb, sub, W)


def _s5(u5, perm, permt, bmats, cmats, avecs, n_lat_tiles):
    nb, ttot, _ = u5.shape
    tt = S5_STEPS
    n_tiles = ttot // tt
    fwd = lambda i: (0, (i + n_lat_tiles) % n_tiles, 0)
    bwd = lambda i: (0, n_tiles - 1 - i, 0)
    rows = tt * nb
    n_grp, _, gw = bmats[0].shape
    return pl.pallas_call(
        _s5_kernel,
        grid=(n_tiles,),
        in_specs=[pl.BlockSpec((nb, tt, W), fwd), pl.BlockSpec((nb, tt, W), bwd),
                  _resident(perm.shape), _resident(permt.shape),
                  _resident(bmats[0].shape), _resident(bmats[1].shape),
                  _resident(cmats[0].shape), _resident(cmats[1].shape),
                  _resident(avecs[0].shape), _resident(avecs[1].shape)],
        out_specs=[pl.BlockSpec((nb, tt, W), fwd), pl.BlockSpec((nb, tt, W), bwd)],
        out_shape=[jax.ShapeDtypeStruct((nb, ttot, W), F32)] * 2,
        scratch_shapes=[pltpu.VMEM((2, n_grp, nb, gw), F32),
                        pltpu.VMEM((2, n_grp, rows, gw), F32),
                        pltpu.VMEM((2, n_grp, rows, gw), BF16)],
        compiler_params=_cparams("arbitrary"),
        name="s5",
    )(u5, u5, perm, permt, bmats[0], bmats[1], cmats[0], cmats[1], avecs[0], avecs[1])


def _fnet_kernel(a_ref, cos_ref, nsin_ref, y_ref):
    y = _dot(cos_ref[...], a_ref[:, :W]) + _dot(nsin_ref[...], a_ref[:, W:])
    y_ref[...] = y.astype(y_ref.dtype)


def _fnet_alias_kernel(a_ref, cos_ref, nsin_ref, prev_ref, y_ref):
    _fnet_kernel(a_ref, cos_ref, nsin_ref, y_ref)


def _fnet(afn, cos_l, nsin_l, cos_c, nsin_c, l_lat, l_ctx):
    nb, ttot, _ = afn.shape
    lat = pl.pallas_call(
        _fnet_kernel,
        grid=(nb,),
        in_specs=[pl.BlockSpec((None, l_lat, 2 * W), lambda b: (b, 0, 0)),
                  _resident(cos_l.shape), _resident(nsin_l.shape)],
        out_specs=pl.BlockSpec((None, l_lat, W), lambda b: (b, 0, 0)),
        out_shape=jax.ShapeDtypeStruct((nb, ttot, W), BF16),
        compiler_params=_cparams("parallel"),
        name="fnet_lat",
    )(afn, cos_l, nsin_l)
    blk = l_lat // l_ctx
    return pl.pallas_call(
        _fnet_alias_kernel,
        grid=(nb,),
        in_specs=[pl.BlockSpec((None, l_ctx, 2 * W), lambda b: (b, blk, 0)),
                  _resident(cos_c.shape), _resident(nsin_c.shape),
                  pl.BlockSpec(memory_space=pl.ANY)],
        out_specs=pl.BlockSpec((None, l_ctx, W), lambda b: (b, blk, 0)),
        out_shape=jax.ShapeDtypeStruct((nb, ttot, W), BF16),
        input_output_aliases={3: 0},
        compiler_params=_cparams("parallel"),
        name="fnet_ctx",
    )(afn, cos_c, nsin_c, lat)


def _sgu_kernel(uv_ref, w_ref, bias_ref, lng_ref, lnb_ref, avg_ref, y_ref):
    lane_group = lax.broadcasted_iota(jnp.int32, (SG_CHUNK, W), 1) // SG_GROUP
    rows = [slice(n * SG_CHUNK, (n + 1) * SG_CHUNK) for n in range(uv_ref.shape[0] // SG_CHUNK)]
    v = [_gelu(uv_ref[r, W:].astype(F32)) for r in rows]
    dv = [t - _dot_sel(t, avg_ref[...], 2) for t in v]
    var = [_dot_sel(t * t, avg_ref[...], 2) for t in dv]
    vn = [(d_ * lax.rsqrt(s_ + EPS) * lng_ref[...] + lnb_ref[...]).astype(BF16) for d_, s_ in zip(dv, var)]
    sv = [_dot(w_ref[0], t) for t in vn]
    for g in range(1, SG_GROUPS):
        sv = [jnp.where(lane_group == g, _dot(w_ref[g], t), s_) for t, s_ in zip(vn, sv)]
    for r, s_ in zip(rows, sv):
        y_ref[r, :] = (_gelu(uv_ref[r, :W].astype(F32)) * (s_ + bias_ref[...])).astype(y_ref.dtype)


def _sgu(sg, w, bias, lng, lnb, avg, rows):
    nb, ttot, _ = sg.shape
    return pl.pallas_call(
        _sgu_kernel,
        grid=(nb, ttot // rows),
        in_specs=[pl.BlockSpec((None, rows, 2 * W), lambda b, n: (b, n, 0)),
                  _resident(w.shape), _resident(bias.shape), _resident(lng.shape),
                  _resident(lnb.shape), _resident(avg.shape)],
        out_specs=pl.BlockSpec((None, rows, W), lambda b, n: (b, n, 0)),
        out_shape=jax.ShapeDtypeStruct((nb, ttot, W), BF16),
        compiler_params=_cparams("parallel", "parallel"),
        name="sgu",
    )(sg, w, bias, lng, lnb, avg)


def _gdn_masks(backward):
    c, heads = GDN_CHUNK, GDN_HEADS
    i = np.arange(c)[:, None]
    j = (np.arange(W) % c)[None, :]
    incl = (i <= j) if backward else (i >= j)
    strict = (i < j) if backward else (i > j)
    j64 = np.arange(c)[None, :]
    incl64 = (i <= j64) if backward else (i >= j64)
    same = lambda n: (i // n) == (j // n)
    levels = [same(GDN_BASE)]
    n = GDN_BASE
    while n < c:
        levels.append(same(2 * n) & ~same(n))
        n *= 2
    f = lambda m: jnp.asarray(m, F32)
    return (f(np.stack([incl, strict, i == j])), jnp.asarray(incl64, BF16), f(np.stack(levels)))


def _gdn_chains(chains, bd_ref, bd16_ref):
    c, heads = GDN_CHUNK, GDN_HEADS
    each = lambda f, *cols: [f(*args) for args in zip(*cols)]

    def expand_heads(t):
        t = t.astype(BF16)
        return jnp.concatenate([t] * heads, axis=0) * bd16_ref[:, :t.shape[1]]

    kvq, gb, expand, m64, incl64, lvls, s_refs = zip(*chains)
    kvq = [t.astype(F32) for t in kvq]
    k = [t[:, :W] for t in kvq]
    v = [t[:, W:2 * W] for t in kvq]
    q = [t[:, 2 * W:] for t in kvq]
    def expand_gates(g, e):
        hi, lo = _split(g, 2)
        t = _dot(jnp.concatenate([hi, lo], axis=0), e[...])
        return t[:c] + t[c:]

    ge = each(expand_gates, gb, expand)
    g_l = [t[:, :W] for t in ge]
    beta_l = [t[:, W:] for t in ge]
    incl = [m[0] for m in m64]
    strict = [m[1] for m in m64]
    diag = [m[2] for m in m64]

    def cumulative(g, i64):
        acc = None
        for piece in _split(g, 2):
            t = _dot(i64[...], piece)
            acc = t if acc is None else acc + t
        return acc

    gc = each(cumulative, g_l, incl64)
    dif = each(lambda g, dg: g - jnp.sum(g * dg, axis=0, keepdims=True), gc, diag)
    g_tot = [jnp.sum(t, axis=0, keepdims=True) for t in g_l]
    kb = each(lambda a_, b_: a_ * b_, k, beta_l)
    k_st = [expand_heads(t) for t in k]
    kq = each(lambda a_, b_, st: lax.dot_general(
        jnp.concatenate([a_, b_], axis=0).astype(BF16), st, (((1,), (1,)), ((), ())),
        preferred_element_type=F32), kb, q, k_st)
    rel = each(lambda d_, i: jnp.exp(d_ * i) * i, dif, incl)
    a = each(lambda t, r, st: t[:c] * r * st, kq, rel, strict)
    qk = each(lambda t, r: t[c:] * r, kq, rel)
    pw = each(lambda t, lv: t * lv[0], a, lvls)
    t_inv = each(lambda dg, t: dg - t, diag, pw)
    pw_bd = [expand_heads(t) for t in pw]
    for _ in range(int(math.log2(GDN_BASE)) - 1):
        pw = each(lambda t, bd_: _dot(t.astype(BF16), bd_), pw, pw_bd)
        pw_bd = [expand_heads(t) for t in pw]
        t_inv = each(lambda t, bd_: t + _dot(t.astype(BF16), bd_), t_inv, pw_bd)
    for lvl in range(1, lvls[0].shape[0]):
        z_bd = each(lambda t, lv: expand_heads(t * lv[lvl]), a, lvls)
        tz = each(lambda t, z_: _dot(t.astype(BF16), z_), t_inv, z_bd)
        t_inv = each(lambda t, tz_: t - _dot(tz_.astype(BF16), expand_heads(t)), t_inv, tz)
    e_gc = [jnp.exp(t) for t in gc]
    s = [r[...] for r in s_refs]
    sb = [t.astype(BF16) for t in s]
    ws = each(lambda kb_, e, q_, sb_: _dot(jnp.concatenate([kb_ * e, q_ * e], axis=0).astype(BF16), sb_),
              kb, e_gc, q, sb)
    rhs = each(lambda v_, b_, t: expand_heads(v_ * b_ - t[:c]), v, beta_l, ws)
    u = each(lambda t, r: _dot(t.astype(BF16), r), t_inv, rhs)
    o = each(lambda t, qk_, u_: t[c:] + _dot(qk_.astype(BF16), expand_heads(u_)), ws, qk, u)
    k_dec = each(lambda k_, gt, g: k_ * jnp.exp(gt - g), k, g_tot, gc)
    upd = each(lambda kd, u_: lax.dot_general(kd.astype(BF16), u_.astype(BF16), (((0,), (0,)), ((), ())),
                                              preferred_element_type=F32), k_dec, u)
    for r, s_, gt, up in zip(s_refs, s, g_tot, upd):
        r[...] = s_ * jnp.exp(gt) + up * bd_ref[...]
    return o


def _gdn_chunk_kernel(kvqf_ref, kvqb_ref, gbf_ref, gbb_ref, ef_ref, eb_ref, m64f_ref, m64b_ref,
                      i64f_ref, i64b_ref, lvlf_ref, lvlb_ref, bd_ref, bd16_ref, of_ref, ob_ref, s_ref):
    @pl.when(pl.program_id(1) == 0)
    def _():
        s_ref[...] = jnp.zeros_like(s_ref)

    chains = []
    for bi in range(kvqf_ref.shape[0]):
        chains.append((kvqf_ref[bi], gbf_ref[bi], ef_ref, m64f_ref, i64f_ref, lvlf_ref, s_ref.at[bi, 0]))
        chains.append((kvqb_ref[bi], gbb_ref[bi], eb_ref, m64b_ref, i64b_ref, lvlb_ref, s_ref.at[bi, 1]))
    outs = _gdn_chains(chains, bd_ref, bd16_ref)
    for bi in range(kvqf_ref.shape[0]):
        of_ref[bi] = outs[2 * bi].astype(of_ref.dtype)
        ob_ref[bi] = outs[2 * bi + 1].astype(ob_ref.dtype)


def _gdn_chunk(kvq, gb, expand, masks, bd, bd16, l_lat, per_step):
    c = GDN_CHUNK
    nb, ttot, _ = kvq.shape
    n_lat = l_lat // c
    n_chunks = ttot // c
    fwd = lambda b, i: (b, (i + n_lat) % n_chunks, 0)
    bwd = lambda b, i: (b, n_chunks - 1 - i, 0)
    consts = [expand[0], expand[1], masks[0][0], masks[1][0], masks[0][1], masks[1][1],
              masks[0][2], masks[1][2], bd, bd16]
    return pl.pallas_call(
        _gdn_chunk_kernel,
        grid=(nb // per_step, n_chunks),
        in_specs=[pl.BlockSpec((per_step, c, 3 * W), fwd), pl.BlockSpec((per_step, c, 3 * W), bwd),
                  pl.BlockSpec((per_step, c, LANES), fwd), pl.BlockSpec((per_step, c, LANES), bwd)]
                 + [_resident(t.shape) for t in consts],
        out_specs=[pl.BlockSpec((per_step, c, W), fwd), pl.BlockSpec((per_step, c, W), bwd)],
        out_shape=[jax.ShapeDtypeStruct((nb, ttot, W), BF16)] * 2,
        scratch_shapes=[pltpu.VMEM((per_step, 2, W, W), F32)],
        compiler_params=_cparams("parallel", "arbitrary"),
        name="gdn_chunk",
    )(kvq, kvq, gb, gb, *consts)


def _merge_kernel(*refs, n_lat_tiles, split):
    x_ref, c_ref = (refs[0], refs[1]) if split else (refs[0], None)
    (h_ref, g1_ref, yf_ref, yb_ref, u5_ref, dskip_ref, gluw_ref, glub_ref, of_ref, ob_ref, z_ref,
     gain_ref, avg_ref, yfn_ref, ysg_ref, wgate_ref, wbr_ref, wout_ref, o_ref) = refs[2 if split else 1:]
    d = x_ref.shape[2]
    subs = range(x_ref.shape[0])
    y5 = [_gelu(yf_ref[s] + yb_ref[s] + dskip_ref[...] * u5_ref[s]) for s in subs]
    glu = [_dot(t.astype(BF16), gluw_ref[...]) for t in y5]
    o = [of_ref[s].astype(F32) + ob_ref[s].astype(F32) for s in subs]
    ms = [_dot_sel(t * t, avg_ref[...], 2) for t in o]
    y5 = [t * _sigmoid(g_ + glub_ref[...]) for t, g_ in zip(y5, glu)]
    yg = [o[s] * lax.rsqrt(ms[s] + EPS) * gain_ref[...] * _silu(z_ref[s].astype(F32)) for s in subs]
    ys = [(y5[s], yfn_ref[s], yg[s], ysg_ref[s]) for s in subs]
    acc = [None for _ in subs]
    for j in range(N_BRANCH):
        for s in subs:
            th = jnp.tanh(_dot(h_ref[s], wgate_ref[:, j * d:(j + 1) * d]))
            b_half = _dot(ys[s][j].astype(BF16), wbr_ref[j])
            t = th * b_half + b_half
            acc[s] = t if acc[s] is None else acc[s] + t
    out = [_dot(t.astype(BF16), wout_ref[...]) for t in acc]
    for s in subs:
        o_ref[s] = _stream_tile(x_ref, c_ref, s, n_lat_tiles) + g1_ref[s] * out[s]


def _merge(xs, mod, acts, consts, n_lat_tiles, n_tiles):
    nb, _, d = xs[0].shape
    h, yf, yb, u5, of, ob, z, yfn, ysg = acts
    dskip, gluw, glub, gain, avg, wgate, wbr, wout = consts
    return pl.pallas_call(
        functools.partial(_merge_kernel, n_lat_tiles=n_lat_tiles, split=len(xs) == 2),
        grid=(nb // PAIR, n_tiles),
        in_specs=_stream_specs(xs, d, PAIR, n_lat_tiles)
                 + [_tile_spec(d), _mod_spec(nb, d, 2, n_lat_tiles),
                  _tile_spec(W), _tile_spec(W), _tile_spec(W),
                  _resident(dskip.shape), _resident(gluw.shape), _resident(glub.shape),
                  _tile_spec(W), _tile_spec(W), _tile_spec(W),
                  _resident(gain.shape), _resident(avg.shape),
                  _tile_spec(W), _tile_spec(W), _resident(wgate.shape),
                  _resident(wbr.shape), _resident(wout.shape)],
        out_specs=_tile_spec(d),
        out_shape=jax.ShapeDtypeStruct((nb, n_tiles * ROW_TILE, d), F32),
        compiler_params=_cparams("parallel", "parallel"),
        name="merge",
    )(*xs, h, mod, yf, yb, u5, dskip, gluw, glub, of, ob, z, gain, avg, yfn, ysg, wgate, wbr, wout)


def _ffn_kernel(x_ref, sh_ref, sc_ref, g2_ref, gain_ref, w1_ref, w2_ref, nf_ref, o_ref, *, final):
    dff = w2_ref.shape[0]
    subs = range(x_ref.shape[0])
    h = [_modulated_norm(x_ref[s], gain_ref[...], sc_ref[s], sh_ref[s]).astype(BF16) for s in subs]
    t = [_dot(h_, w1_ref[...]) for h_ in h]
    act = [(_silu(t_[:, :dff]) * t_[:, dff:]).astype(BF16) for t_ in t]
    out = [_dot(a_, w2_ref[...]) for a_ in act]
    for s in subs:
        y = x_ref[s] + g2_ref[s] * out[s]
        if final:
            y = y * lax.rsqrt(jnp.mean(y * y, axis=-1, keepdims=True) + EPS) * nf_ref[...]
        o_ref[s] = y


def _ffn(x3, mod, gain, w1, w2, norm_f, n_lat_tiles, final):
    nb, ttot, d = x3.shape
    return pl.pallas_call(
        functools.partial(_ffn_kernel, final=final),
        grid=(nb // PAIR, ttot // ROW_TILE),
        in_specs=[_tile_spec(d), _mod_spec(nb, d, 3, n_lat_tiles), _mod_spec(nb, d, 4, n_lat_tiles),
                  _mod_spec(nb, d, 5, n_lat_tiles),
                  _resident(gain.shape), _resident(w1.shape), _resident(w2.shape),
                  _resident(norm_f.shape)],
        out_specs=_tile_spec(d),
        out_shape=jax.ShapeDtypeStruct(x3.shape, F32),
        compiler_params=_cparams("parallel", "parallel"),
        name="ffn",
    )(x3, mod, mod, mod, gain, w1, w2, norm_f)


def _dft_tables(n):
    idx = np.arange(n, dtype=np.int64)
    ang = 2.0 * np.pi * ((idx[:, None] * idx[None, :]) % n).astype(np.float64) / n
    return np.cos(ang), np.sin(ang)


def _s5_tables(lam_re, lam_im, log_dt, b_re, b_im, c_re, c_im):
    g, p, cg = S5_GROUPS, S5_STATE, S5_GROUP
    lam = lax.complex(lam_re.astype(F32), lam_im.astype(F32))
    a_bar = jnp.exp(lam * jnp.exp(log_dt.astype(F32))[:, None])
    b_bar = ((a_bar - 1.0) / lam)[..., None] * lax.complex(b_re.astype(F32), b_im.astype(F32))
    eye = jnp.eye(g, dtype=F32)
    bm_re = jnp.einsum('gpc,gh->gchp', b_bar.real, eye).reshape(g * cg, g * p)
    bm_im = jnp.einsum('gpc,gh->gchp', b_bar.imag, eye).reshape(g * cg, g * p)
    cm_re = jnp.einsum('gcp,gh->gphc', c_re.astype(F32), eye).reshape(g * p, g * cg)
    cm_im = jnp.einsum('gcp,gh->gphc', c_im.astype(F32), eye).reshape(g * p, g * cg)
    ng, h = N_S5 // S5_LANE_GROUP, S5_LANE_GROUP
    bmat = jnp.concatenate([bm_re.reshape(W, ng, h), bm_im.reshape(W, ng, h)], axis=2)
    bmat = jnp.transpose(bmat, (1, 0, 2)).astype(BF16)
    cmat = jnp.concatenate([cm_re.reshape(ng, h, W), -cm_im.reshape(ng, h, W)], axis=1).astype(BF16)
    avec = jnp.concatenate([a_bar.real.reshape(ng, 1, h), a_bar.imag.reshape(ng, 1, h)], axis=2)
    return bmat, cmat, avec


def _layer_consts(l, p):
    d = p['w_in'].shape[1]
    w_in = p['w_in'][l]
    sizes = (W, W, W, 2 * GDN_HEADS, 2 * GDN_HEADS, W, W, W, W, W, N_BRANCH * d)
    offs = np.concatenate([[0], np.cumsum(sizes)])
    col = lambda j: w_in[:, offs[j]:offs[j + 1]]
    u5, k, v, a, bt, q, z, ufn, usg, vsg, gate = [col(j) for j in range(len(sizes))]
    cc, sc = _dft_tables(FN_GROUP)
    fw = p['fn_w'][l].astype(F32)
    wc = jnp.einsum('cd,gde->gce', jnp.asarray(cc, F32), fw, precision=HIGHEST)
    ws = jnp.einsum('cd,gde->gce', jnp.asarray(sc, F32), fw, precision=HIGHEST)
    ufn_g = ufn.reshape(d, FN_GROUPS, FN_GROUP)
    fold_c = jnp.einsum('kgc,gce->kge', ufn_g, wc, precision=HIGHEST).reshape(d, W)
    fold_s = jnp.einsum('kgc,gce->kge', ufn_g, ws, precision=HIGHEST).reshape(d, W)
    w_ab = jnp.concatenate([a, bt, jnp.zeros((d, LANES - 4 * GDN_HEADS), F32)], axis=1)
    ws_in = [u5, jnp.concatenate([k, v, q], axis=1), w_ab, z,
             jnp.concatenate([fold_c, fold_s], axis=1), jnp.concatenate([usg, vsg], axis=1)]
    ws_in = [w.astype(BF16) for w in ws_in]
    w_gate = (0.5 * gate).astype(BF16)

    s5 = [_s5_tables(p['s5_lam_re'][l, dr], p['s5_lam_im'][l, dr], p['s5_log_dt'][l, dr],
                     p['s5_b_re'][l, dr], p['s5_b_im'][l, dr], p['s5_c_re'][l, dr],
                     p['s5_c_im'][l, dr]) for dr in range(2)]

    conv = p['gdn_conv'][l].astype(F32)
    cw = jnp.transpose(conv, (2, 0, 1)).reshape(conv.shape[2], 3 * W)
    pad = jnp.zeros((1, LANES - 2 * GDN_HEADS), F32)
    alog = jnp.concatenate([p['gdn_a_log'][l].astype(F32).reshape(1, -1), pad], axis=1)
    dtb = jnp.concatenate([p['gdn_dt_bias'][l].astype(F32).reshape(1, -1), pad], axis=1)

    sg_bias = jnp.repeat(p['sg_b'][l].astype(F32).T, SG_GROUP, axis=1)
    merge_consts = (p['s5_d'][l].astype(F32).reshape(1, W), p['s5_glu_w'][l].astype(BF16),
                    p['s5_glu_b'][l].astype(F32).reshape(1, W),
                    jnp.tile(p['gdn_norm'][l].astype(F32), GDN_HEADS).reshape(1, W))
    return dict(
        ws_in=ws_in, w_gate=w_gate, s5=s5, cw=cw, alog=alog, dtb=dtb,
        sg_w=p['sg_w'][l].astype(BF16), sg_bias=sg_bias,
        sg_lng=p['sg_ln_g'][l].astype(F32).reshape(1, W), sg_lnb=p['sg_ln_b'][l].astype(F32).reshape(1, W),
        merge=merge_consts, wbr_half=(0.5 * p['w_branch'][l]).astype(BF16),
        wout=p['w_out'][l].astype(BF16),
        w1=p['ffn_w1'][l].astype(BF16), w2=p['ffn_w2'][l].astype(BF16),
        norm1=p['norm1'][l].astype(F32).reshape(1, d), norm2=p['norm2'][l].astype(F32).reshape(1, d))


def kernel(x, c, ctx, c_ctx, ada_w, ada_b, norm1, norm2, w_in, s5_lam_re, s5_lam_im, s5_log_dt, s5_b_re, s5_b_im, s5_c_re, s5_c_im, s5_d, s5_glu_w, s5_glu_b, fn_w, gdn_conv, gdn_a_log, gdn_dt_bias, gdn_norm, sg_ln_g, sg_ln_b, sg_w, sg_b, w_branch, w_out, ffn_w1, ffn_w2, norm_f):
    p = dict(w_in=w_in, s5_lam_re=s5_lam_re, s5_lam_im=s5_lam_im, s5_log_dt=s5_log_dt,
             s5_b_re=s5_b_re, s5_b_im=s5_b_im, s5_c_re=s5_c_re, s5_c_im=s5_c_im, s5_d=s5_d,
             s5_glu_w=s5_glu_w, s5_glu_b=s5_glu_b, fn_w=fn_w, gdn_conv=gdn_conv,
             gdn_a_log=gdn_a_log, gdn_dt_bias=gdn_dt_bias, gdn_norm=gdn_norm, sg_ln_g=sg_ln_g,
             sg_ln_b=sg_ln_b, sg_w=sg_w, sg_b=sg_b, w_branch=w_branch, w_out=w_out,
             ffn_w1=ffn_w1, ffn_w2=ffn_w2, norm1=norm1, norm2=norm2)
    nb, l_lat, d = x.shape
    l_ctx = ctx.shape[1]
    depth = ada_w.shape[0]
    ttot = l_lat + l_ctx
    assert nb % SUBLANES == 0 and nb % PAIR == 0 and nb % KIN_GROUP == 0
    assert d == N_BRANCH * W and l_lat % l_ctx == 0
    assert l_lat % ROW_TILE == 0 and l_ctx % ROW_TILE == 0 and ROW_TILE % SG_CHUNK == 0
    n_lat_tiles = l_lat // ROW_TILE
    n_tiles = ttot // ROW_TILE
    sg_rows = max(r for r in range(SG_CHUNK, 7 * SG_CHUNK, SG_CHUNK) if ttot % r == 0)

    if l_ctx == ROW_TILE:
        xs = (x.astype(F32), ctx.astype(F32))
    else:
        xs = (jnp.concatenate([x, ctx], axis=1).astype(F32),)

    n_cond = nb + max(PAIR, KIN_GROUP)
    ada_rows = -(-n_cond // SUBLANES) * SUBLANES
    cc = jnp.zeros((ada_rows, d), F32).at[:nb].set(c.astype(F32)).at[nb:n_cond].set(c_ctx.astype(F32))
    mods = _ada(cc, ada_w.astype(F32), ada_b.astype(F32))[:, :n_cond].reshape(depth, n_cond, 1, 6 * d)

    cos_l, sin_l = _dft_tables(l_lat)
    cos_c, sin_c = _dft_tables(l_ctx)
    sc_l = 1.0 / math.sqrt(l_lat * FN_GROUP)
    sc_c = 1.0 / math.sqrt(l_ctx * FN_GROUP)
    cos_l, nsin_l = jnp.asarray(cos_l * sc_l, BF16), jnp.asarray(-sin_l * sc_l, BF16)
    cos_c, nsin_c = jnp.asarray(cos_c * sc_c, BF16), jnp.asarray(-sin_c * sc_c, BF16)
    nf = norm_f.astype(F32).reshape(1, d)

    lane_head = np.arange(W) // GDN_HEAD_DIM
    seg = (lane_head[:, None] == lane_head[None, :]).astype(np.float32)
    seg_ones = jnp.asarray(seg, BF16)
    seg_avg = jnp.asarray(seg / GDN_HEAD_DIM, BF16)
    expand = []
    for dr in range(2):
        e = np.zeros((LANES, 2 * W), np.float32)
        for h in range(GDN_HEADS):
            e[dr * GDN_HEADS + h, :W] = lane_head == h
            e[2 * GDN_HEADS + dr * GDN_HEADS + h, W:] = lane_head == h
        expand.append(jnp.asarray(e, BF16))
    gdn_masks = [_gdn_masks(False), _gdn_masks(True)]
    bd = jnp.asarray(seg, F32)
    bd16 = jnp.asarray(np.concatenate([seg, seg], axis=1), BF16)
    src = (np.arange(nb)[None, :] * SUBLANES + np.arange(SUBLANES)[:, None]).reshape(-1)
    perm_np = np.zeros((nb * SUBLANES, nb * SUBLANES), np.float32)
    perm_np[np.arange(nb * SUBLANES), src] = 1.0
    perm, permt = jnp.asarray(perm_np, BF16), jnp.asarray(perm_np.T, BF16)

    for l in range(depth):
        k = _layer_consts(l, p)
        mod = mods[l]
        last = l == depth - 1
        u5, kvq_p, gb, z, afn, sg, h = _kin(xs, mod, k['norm1'], k['ws_in'],
                                            (k['cw'], seg_ones, k['alog'], k['dtb']), n_lat_tiles, n_tiles)
        yf, yb = _s5(u5, perm, permt, [t[0] for t in k['s5']], [t[1] for t in k['s5']],
                     [t[2] for t in k['s5']], l_lat // S5_STEPS)
        yfn = _fnet(afn, cos_l, nsin_l, cos_c, nsin_c, l_lat, l_ctx)
        ysg = _sgu(sg, k['sg_w'], k['sg_bias'], k['sg_lng'], k['sg_lnb'], seg_avg, sg_rows)
        of, ob = _gdn_chunk(kvq_p, gb, expand, gdn_masks, bd, bd16, l_lat, 8)
        acts = (h, yf, yb, u5, of, ob, z, yfn, ysg)
        consts = k['merge'] + (seg_avg, k['w_gate'], k['wbr_half'], k['wout'])
        x3 = _merge(xs, mod, acts, consts, n_lat_tiles, n_lat_tiles if last else n_tiles)
        xs = (_ffn(x3, mod, k['norm2'], k['w1'], k['w2'], nf, n_lat_tiles, final=last),)

    return xs[0].astype(x.dtype)
```

```python
import functools
import math

import numpy as np
import jax
import jax.numpy as jnp
from jax import lax
from jax.experimental import pallas as pl
from jax.experimental.pallas import tpu as pltpu

F32 = jnp.float32
BF16 = jnp.bfloat16
HIGHEST = lax.Precision.HIGHEST

EPS = 1e-6
W = 256
N_BRANCH = 4
S5_GROUP = 16
S5_GROUPS = W // S5_GROUP
S5_STATE = 64
N_S5 = S5_GROUPS * S5_STATE
FN_GROUPS = 4
FN_GROUP = W // FN_GROUPS
GDN_HEADS = 4
GDN_HEAD_DIM = W // GDN_HEADS
GDN_CHUNK = 64
GDN_BASE = 8
SG_GROUPS = 4
SG_GROUP = W // SG_GROUPS
SG_CHUNK = 128
LANES = 128
SUBLANES = 8
ROW_TILE = 256
PAIR = 2
KIN_GROUP = 4
S5_STEPS = 32
S5_LANE_GROUP = 256
VMEM_LIMIT = 56 * 1024 * 1024


def _cparams(*sem):
    return pltpu.CompilerParams(dimension_semantics=sem, vmem_limit_bytes=VMEM_LIMIT)


def _resident(shape):
    nd = len(shape)
    return pl.BlockSpec(shape, lambda *_: (0,) * nd, pipeline_mode=pl.Buffered(1))


def _dot(a, b):
    return jnp.dot(a, b, preferred_element_type=F32)


def _split(x, parts):
    out = []
    for _ in range(parts - 1):
        hi = x.astype(BF16)
        out.append(hi)
        x = x - hi.astype(F32)
    out.append(x.astype(BF16))
    return out


def _dot_sel(x, sel, parts):
    acc = None
    for piece in _split(x, parts):
        t = _dot(piece, sel)
        acc = t if acc is None else acc + t
    return acc


def _gelu(x):
    return 0.5 * x * (1.0 + jnp.tanh(math.sqrt(2.0 / math.pi) * (x + 0.044715 * (x * x * x))))


def _sigmoid(x):
    return 0.5 * jnp.tanh(0.5 * x) + 0.5


def _silu(x):
    return x * _sigmoid(x)


def _modulated_norm(x, gain, scale, shift):
    y = x * lax.rsqrt(jnp.mean(x * x, axis=-1, keepdims=True) + EPS) * gain
    return y * (1.0 + scale) + shift


def _ada_kernel(c_ref, w_ref, b_ref, o_ref):
    c = c_ref[...]
    o_ref[0] = jnp.dot(_silu(c), w_ref[0], preferred_element_type=F32, precision=HIGHEST) + b_ref[0]


def _ada(cc, ada_w, ada_b):
    depth, d, n = ada_w.shape
    rows = cc.shape[0]
    return pl.pallas_call(
        _ada_kernel,
        grid=(depth, n // d),
        in_specs=[pl.BlockSpec((rows, d), lambda l, j: (0, 0)),
                  pl.BlockSpec((1, d, d), lambda l, j: (l, 0, j)),
                  pl.BlockSpec((1, 1, d), lambda l, j: (l, 0, j))],
        out_specs=pl.BlockSpec((1, rows, d), lambda l, j: (l, 0, j)),
        out_shape=jax.ShapeDtypeStruct((depth, rows, n), F32),
        compiler_params=_cparams("parallel", "parallel"),
        name="ada",
    )(cc, ada_w, ada_b.reshape(depth, 1, n))


def _tile_spec(n, group=PAIR):
    return pl.BlockSpec((group, ROW_TILE, n), lambda b, j: (b, j, 0))


def _mod_spec(nb, d, k, n_lat_tiles, group=PAIR):
    return pl.BlockSpec((group, 1, d), lambda b, j: (jnp.where(j < n_lat_tiles, b, nb // group), 0, k))


def _gdn_features(kvq, before, after, cw_ref, ones_ref):
    rows, n = kvq.shape
    row = lax.broadcasted_iota(jnp.int32, (rows, n), 0)
    xp = jnp.where(row == 0, before, pltpu.roll(kvq, 1, 0))
    xn = jnp.where(row == rows - 1, after, pltpu.roll(kvq, rows - 1, 0))
    y = _silu(xp * cw_ref[0:1, :] + kvq * cw_ref[1:2, :] + xn * cw_ref[2:3, :])
    k = y[:, :W]
    q = y[:, 2 * W:]
    k = k * lax.rsqrt(_dot_sel(k * k, ones_ref[...], 2) + EPS)
    q = q * lax.rsqrt(_dot_sel(q * q, ones_ref[...], 2) + EPS) * GDN_HEAD_DIM ** -0.5
    return k, y[:, W:2 * W], q


def _gdn_gates(ab, alog_ref, dtb_ref):
    z = ab + dtb_ref[...]
    softplus = jnp.maximum(z, 0.0) + jnp.log(1.0 + jnp.exp(-jnp.abs(z)))
    g = -jnp.exp(alog_ref[...]) * softplus
    lane = lax.broadcasted_iota(jnp.int32, ab.shape, 1)
    return jnp.where(lane < 2 * GDN_HEADS, g, _sigmoid(ab))


def _stream_specs(xs, d, group, n_lat_tiles):
    if len(xs) == 1:
        return [_tile_spec(d, group)]
    return [pl.BlockSpec((group, ROW_TILE, d), lambda b, j: (b, jnp.minimum(j, n_lat_tiles - 1), 0)),
            pl.BlockSpec((group, ROW_TILE, d), lambda b, j: (b, jnp.maximum(j - n_lat_tiles, 0), 0))]


def _stream_tile(x_ref, c_ref, s, n_lat_tiles):
    if c_ref is None:
        return x_ref[s]
    return jnp.where(pl.program_id(1) >= n_lat_tiles, c_ref[s], x_ref[s])


def _kin_kernel(*refs, n_lat_tiles, n_tiles, split):
    x_ref, c_ref = (refs[0], refs[1]) if split else (refs[0], None)
    (xp_ref, xn_ref, sh_ref, sc_ref, g_ref, w5, wkvq, wab, wz, wfn, wsg, cw_ref, ones_ref, alog_ref,
     dtb_ref, o5, okvq, ogb, oz, ofn, osg, oh) = refs[2 if split else 1:]
    j = pl.program_id(1)
    subs = range(x_ref.shape[0])
    halo = xp_ref.shape[1]
    hb = [_modulated_norm(_stream_tile(x_ref, c_ref, s, n_lat_tiles), g_ref[...], sc_ref[s],
                          sh_ref[s]).astype(BF16) for s in subs]
    edge = [jnp.concatenate([xp_ref[s], xn_ref[s]], axis=0) for s in subs]
    eb = [_modulated_norm(edge[s], g_ref[...], sc_ref[s], sh_ref[s]).astype(BF16) for s in subs]
    has_prev = jnp.logical_and(j != 0, j != n_lat_tiles).astype(F32)
    has_next = jnp.logical_and(j != n_lat_tiles - 1, j != n_tiles - 1).astype(F32)
    for s in subs:
        oh[s] = hb[s]
    for w_ref, o_ref in ((w5, o5), (wz, oz), (wfn, ofn), (wsg, osg)):
        for s in subs:
            o_ref[s] = _dot(hb[s], w_ref[...]).astype(o_ref.dtype)
    rows = x_ref.shape[1]
    kvq = [_dot(jnp.concatenate([hb[s], eb[s]], axis=0), wkvq[...]) for s in subs]
    for s in subs:
        k, v, q = _gdn_features(kvq[s][:rows], kvq[s][rows + halo - 1:rows + halo, :] * has_prev,
                                kvq[s][rows + halo:rows + halo + 1, :] * has_next, cw_ref, ones_ref)
        okvq[s, :, :W] = k.astype(okvq.dtype)
        okvq[s, :, W:2 * W] = v.astype(okvq.dtype)
        okvq[s, :, 2 * W:] = q.astype(okvq.dtype)
    for s in subs:
        ogb[s] = _gdn_gates(_dot(hb[s], wab[...]), alog_ref, dtb_ref)


def _kin(xs, mod, gain, ws, gdn_consts, n_lat_tiles, n_tiles):
    nb, _, d = xs[0].shape
    ttot = n_tiles * ROW_TILE
    widths = [W, 3 * W, LANES, W, 2 * W, 2 * W, d]
    dts = [F32, BF16, F32, BF16, BF16, BF16, BF16]
    per = ROW_TILE // SUBLANES
    last = xs[0].shape[1] // SUBLANES - 1
    grp = KIN_GROUP
    return pl.pallas_call(
        functools.partial(_kin_kernel, n_lat_tiles=n_lat_tiles, n_tiles=n_tiles, split=len(xs) == 2),
        grid=(nb // grp, n_tiles),
        in_specs=_stream_specs(xs, d, grp, n_lat_tiles)
                 + [pl.BlockSpec((grp, SUBLANES, d), lambda b, j: (b, jnp.clip(j * per - 1, 0, last), 0)),
                    pl.BlockSpec((grp, SUBLANES, d), lambda b, j: (b, jnp.clip((j + 1) * per, 0, last), 0)),
                    _mod_spec(nb, d, 0, n_lat_tiles, grp), _mod_spec(nb, d, 1, n_lat_tiles, grp),
                    _resident((1, d))] + [_resident(w.shape) for w in ws]
                 + [_resident(t.shape) for t in gdn_consts],
        out_specs=[_tile_spec(n, grp) for n in widths],
        out_shape=[jax.ShapeDtypeStruct((nb, ttot, n), dt) for n, dt in zip(widths, dts)],
        compiler_params=_cparams("parallel", "parallel"),
        name="kin",
    )(*xs, xs[0], xs[0], mod, mod, gain, *ws, *gdn_consts)


def _s5_kernel(uf_ref, ub_ref, perm_ref, permt_ref, bmf, bmb, cmf, cmb, af_ref, ab_ref,
               yf_ref, yb_ref, st_ref, bu_ref, s_ref):
    i = pl.program_id(0)
    nb, tt, _ = uf_ref.shape
    sub = SUBLANES
    n_grp, _, gw = bmf.shape
    half = gw // 2

    @pl.when(i == 0)
    def _():
        st_ref[...] = jnp.zeros_like(st_ref)

    dirs = ((uf_ref, bmf, cmf, af_ref, yf_ref, range(tt)),
            (ub_ref, bmb, cmb, ab_ref, yb_ref, range(tt - 1, -1, -1)))

    def regroup(dr):
        u_ref = dirs[dr][0]
        parts = []
        for k in range(tt // sub):
            blk = u_ref[:, k * sub:(k + 1) * sub, :].reshape(nb * sub, W).astype(BF16)
            parts.append(_dot(perm_ref[...], blk).astype(BF16))
        return jnp.concatenate(parts, axis=0)

    u_tb = [regroup(0), regroup(1)]
    y_tb = [None, None]

    def project(dr, g):
        bu_ref[dr, g] = _dot(u_tb[dr], dirs[dr][1][g])

    def recur(dr, g):
        a_ref, order = dirs[dr][3], dirs[dr][5]
        for c in range(half // LANES):
            re = slice(c * LANES, (c + 1) * LANES)
            im = slice(half + c * LANES, half + (c + 1) * LANES)
            a_re = a_ref[g, :, re]
            a_im = a_ref[g, :, im]
            s_re = st_ref[dr, g, :, re]
            s_im = st_ref[dr, g, :, im]
            for t in order:
                r = slice(t * nb, (t + 1) * nb)
                n_re = a_re * s_re - a_im * s_im + bu_ref[dr, g, r, re]
                n_im = a_re * s_im + a_im * s_re + bu_ref[dr, g, r, im]
                s_re, s_im = n_re, n_im
                s_ref[dr, g, r, re] = s_re.astype(BF16)
                s_ref[dr, g, r, im] = s_im.astype(BF16)
            st_ref[dr, g, :, re] = s_re
            st_ref[dr, g, :, im] = s_im

    def read_out(dr, g):
        t = _dot(s_ref[dr, g], dirs[dr][2][g])
        y_tb[dr] = t if y_tb[dr] is None else y_tb[dr] + t

    for dr in range(2):
        project(dr, 0)
    for g in range(n_grp):
        for dr in range(2):
            if g + 1 < n_grp:
                project(dr, g + 1)
            recur(dr, g)
        for dr in range(2):
            read_out(dr, g)
    for dr in range(2):
        y_ref = dirs[dr][4]
        for k in range(tt // sub):
            y_bt = _dot(permt_ref[...], y_tb[dr][k * sub * nb:(k + 1) * sub * nb].astype(BF16))
            y_ref[:, k * sub:(k + 1) * sub, :] = y_bt.reshape(nb, sub, W)


def _s5(u5, perm, permt, bmats, cmats, avecs, n_lat_tiles):
    nb, ttot, _ = u5.shape
    tt = S5_STEPS
    n_tiles = ttot // tt
    fwd = lambda i: (0, (i + n_lat_tiles) % n_tiles, 0)
    bwd = lambda i: (0, n_tiles - 1 - i, 0)
    rows = tt * nb
    n_grp, _, gw = bmats[0].shape
    return pl.pallas_call(
        _s5_kernel,
        grid=(n_tiles,),
        in_specs=[pl.BlockSpec((nb, tt, W), fwd), pl.BlockSpec((nb, tt, W), bwd),
                  _resident(perm.shape), _resident(permt.shape),
                  _resident(bmats[0].shape), _resident(bmats[1].shape),
                  _resident(cmats[0].shape), _resident(cmats[1].shape),
                  _resident(avecs[0].shape), _resident(avecs[1].shape)],
        out_specs=[pl.BlockSpec((nb, tt, W), fwd), pl.BlockSpec((nb, tt, W), bwd)],
        out_shape=[jax.ShapeDtypeStruct((nb, ttot, W), F32)] * 2,
        scratch_shapes=[pltpu.VMEM((2, n_grp, nb, gw), F32),
                        pltpu.VMEM((2, n_grp, rows, gw), F32),
                        pltpu.VMEM((2, n_grp, rows, gw), BF16)],
        compiler_params=_cparams("arbitrary"),
        name="s5",
    )(u5, u5, perm, permt, bmats[0], bmats[1], cmats[0], cmats[1], avecs[0], avecs[1])


def _fnet_kernel(a_ref, cosl_ref, nsinl_ref, cosc_ref, nsinc_ref, y_ref):
    l_lat = cosl_ref.shape[0]
    for rows, cos_ref, nsin_ref in ((slice(0, l_lat), cosl_ref, nsinl_ref),
                                    (slice(l_lat, y_ref.shape[0]), cosc_ref, nsinc_ref)):
        y = _dot(cos_ref[...], a_ref[rows, :W]) + _dot(nsin_ref[...], a_ref[rows, W:])
        y_ref[rows, :] = y.astype(y_ref.dtype)


def _fnet(afn, cos_l, nsin_l, cos_c, nsin_c):
    nb, ttot, _ = afn.shape
    return pl.pallas_call(
        _fnet_kernel,
        grid=(nb,),
        in_specs=[pl.BlockSpec((None, ttot, 2 * W), lambda b: (b, 0, 0)),
                  _resident(cos_l.shape), _resident(nsin_l.shape),
                  _resident(cos_c.shape), _resident(nsin_c.shape)],
        out_specs=pl.BlockSpec((None, ttot, W), lambda b: (b, 0, 0)),
        out_shape=jax.ShapeDtypeStruct((nb, ttot, W), BF16),
        compiler_params=_cparams("parallel"),
        name="fnet",
    )(afn, cos_l, nsin_l, cos_c, nsin_c)


def _sgu_kernel(uv_ref, w_ref, bias_ref, lng_ref, lnb_ref, avg_ref, y_ref):
    lane_group = lax.broadcasted_iota(jnp.int32, (SG_CHUNK, W), 1) // SG_GROUP
    rows = [slice(n * SG_CHUNK, (n + 1) * SG_CHUNK) for n in range(uv_ref.shape[0] // SG_CHUNK)]
    v = [_gelu(uv_ref[r, W:].astype(F32)) for r in rows]
    dv = [t - _dot_sel(t, avg_ref[...], 2) for t in v]
    var = [_dot_sel(t * t, avg_ref[...], 2) for t in dv]
    vn = [(d_ * lax.rsqrt(s_ + EPS) * lng_ref[...] + lnb_ref[...]).astype(BF16) for d_, s_ in zip(dv, var)]
    sv = [_dot(w_ref[0], t) for t in vn]
    for g in range(1, SG_GROUPS):
        sv = [jnp.where(lane_group == g, _dot(w_ref[g], t), s_) for t, s_ in zip(vn, sv)]
    for r, s_ in zip(rows, sv):
        y_ref[r, :] = (_gelu(uv_ref[r, :W].astype(F32)) * (s_ + bias_ref[...])).astype(y_ref.dtype)


def _sgu(sg, w, bias, lng, lnb, avg, rows):
    nb, ttot, _ = sg.shape
    return pl.pallas_call(
        _sgu_kernel,
        grid=(nb, ttot // rows),
        in_specs=[pl.BlockSpec((None, rows, 2 * W), lambda b, n: (b, n, 0)),
                  _resident(w.shape), _resident(bias.shape), _resident(lng.shape),
                  _resident(lnb.shape), _resident(avg.shape)],
        out_specs=pl.BlockSpec((None, rows, W), lambda b, n: (b, n, 0)),
        out_shape=jax.ShapeDtypeStruct((nb, ttot, W), BF16),
        compiler_params=_cparams("parallel", "parallel"),
        name="sgu",
    )(sg, w, bias, lng, lnb, avg)


def _gdn_masks(backward):
    c, heads = GDN_CHUNK, GDN_HEADS
    i = np.arange(c)[:, None]
    j = (np.arange(W) % c)[None, :]
    incl = (i <= j) if backward else (i >= j)
    strict = (i < j) if backward else (i > j)
    j64 = np.arange(c)[None, :]
    incl64 = (i <= j64) if backward else (i >= j64)
    same = lambda n: (i // n) == (j // n)
    levels = [same(GDN_BASE)]
    n = GDN_BASE
    while n < c:
        levels.append(same(2 * n) & ~same(n))
        n *= 2
    f = lambda m: jnp.asarray(m, F32)
    return (f(np.stack([incl, strict, i == j])), jnp.asarray(incl64, BF16), f(np.stack(levels)))


def _gdn_chains(chains, bd_ref, bd16_ref):
    c, heads = GDN_CHUNK, GDN_HEADS
    each = lambda f, *cols: [f(*args) for args in zip(*cols)]

    def expand_heads(t):
        t = t.astype(BF16)
        return jnp.concatenate([t] * heads, axis=0) * bd16_ref[:, :t.shape[1]]

    kvq, gb, expand, m64, incl64, lvls, s_refs = zip(*chains)
    kvq = [t.astype(F32) for t in kvq]
    k = [t[:, :W] for t in kvq]
    v = [t[:, W:2 * W] for t in kvq]
    q = [t[:, 2 * W:] for t in kvq]
    def expand_gates(g, e):
        hi, lo = _split(g, 2)
        t = _dot(jnp.concatenate([hi, lo], axis=0), e[...])
        return t[:c] + t[c:]

    ge = each(expand_gates, gb, expand)
    g_l = [t[:, :W] for t in ge]
    beta_l = [t[:, W:] for t in ge]
    incl = [m[0] for m in m64]
    strict = [m[1] for m in m64]
    diag = [m[2] for m in m64]

    def cumulative(g, i64):
        acc = None
        for piece in _split(g, 2):
            t = _dot(i64[...], piece)
            acc = t if acc is None else acc + t
        return acc

    gc = each(cumulative, g_l, incl64)
    dif = each(lambda g, dg: g - jnp.sum(g * dg, axis=0, keepdims=True), gc, diag)
    g_tot = [jnp.sum(t, axis=0, keepdims=True) for t in g_l]
    kb = each(lambda a_, b_: a_ * b_, k, beta_l)
    k_st = [expand_heads(t) for t in k]
    kq = each(lambda a_, b_, st: lax.dot_general(
        jnp.concatenate([a_, b_], axis=0).astype(BF16), st, (((1,), (1,)), ((), ())),
        preferred_element_type=F32), kb, q, k_st)
    rel = each(lambda d_, i: jnp.exp(d_ * i) * i, dif, incl)
    a = each(lambda t, r, st: t[:c] * r * st, kq, rel, strict)
    qk = each(lambda t, r: t[c:] * r, kq, rel)
    pw = each(lambda t, lv: t * lv[0], a, lvls)
    t_inv = each(lambda dg, t: dg - t, diag, pw)
    pw_bd = [expand_heads(t) for t in pw]
    for _ in range(int(math.log2(GDN_BASE)) - 1):
        pw = each(lambda t, bd_: _dot(t.astype(BF16), bd_), pw, pw_bd)
        pw_bd = [expand_heads(t) for t in pw]
        t_inv = each(lambda t, bd_: t + _dot(t.astype(BF16), bd_), t_inv, pw_bd)
    for lvl in range(1, lvls[0].shape[0]):
        z_bd = each(lambda t, lv: expand_heads(t * lv[lvl]), a, lvls)
        tz = each(lambda t, z_: _dot(t.astype(BF16), z_), t_inv, z_bd)
        t_inv = each(lambda t, tz_: t - _dot(tz_.astype(BF16), expand_heads(t)), t_inv, tz)
    e_gc = [jnp.exp(t) for t in gc]
    s = [r[...] for r in s_refs]
    sb = [t.astype(BF16) for t in s]
    ws = each(lambda kb_, e, q_, sb_: _dot(jnp.concatenate([kb_ * e, q_ * e], axis=0).astype(BF16), sb_),
              kb, e_gc, q, sb)
    rhs = each(lambda v_, b_, t: expand_heads(v_ * b_ - t[:c]), v, beta_l, ws)
    u = each(lambda t, r: _dot(t.astype(BF16), r), t_inv, rhs)
    o = each(lambda t, qk_, u_: t[c:] + _dot(qk_.astype(BF16), expand_heads(u_)), ws, qk, u)
    k_dec = each(lambda k_, gt, g: k_ * jnp.exp(gt - g), k, g_tot, gc)
    upd = each(lambda kd, u_: lax.dot_general(kd.astype(BF16), u_.astype(BF16), (((0,), (0,)), ((), ())),
                                              preferred_element_type=F32), k_dec, u)
    for r, s_, gt, up in zip(s_refs, s, g_tot, upd):
        r[...] = s_ * jnp.exp(gt) + up * bd_ref[...]
    return o


def _gdn_chunk_kernel(kvqf_ref, kvqb_ref, gbf_ref, gbb_ref, ef_ref, eb_ref, m64f_ref, m64b_ref,
                      i64f_ref, i64b_ref, lvlf_ref, lvlb_ref, bd_ref, bd16_ref, of_ref, ob_ref, s_ref):
    @pl.when(pl.program_id(1) == 0)
    def _():
        s_ref[...] = jnp.zeros_like(s_ref)

    chains = []
    for bi in range(kvqf_ref.shape[0]):
        chains.append((kvqf_ref[bi], gbf_ref[bi], ef_ref, m64f_ref, i64f_ref, lvlf_ref, s_ref.at[bi, 0]))
        chains.append((kvqb_ref[bi], gbb_ref[bi], eb_ref, m64b_ref, i64b_ref, lvlb_ref, s_ref.at[bi, 1]))
    outs = _gdn_chains(chains, bd_ref, bd16_ref)
    for bi in range(kvqf_ref.shape[0]):
        of_ref[bi] = outs[2 * bi].astype(of_ref.dtype)
        ob_ref[bi] = outs[2 * bi + 1].astype(ob_ref.dtype)


def _gdn_chunk(kvq, gb, expand, masks, bd, bd16, l_lat, per_step):
    c = GDN_CHUNK
    nb, ttot, _ = kvq.shape
    n_lat = l_lat // c
    n_chunks = ttot // c
    fwd = lambda b, i: (b, (i + n_lat) % n_chunks, 0)
    bwd = lambda b, i: (b, n_chunks - 1 - i, 0)
    consts = [expand[0], expand[1], masks[0][0], masks[1][0], masks[0][1], masks[1][1],
              masks[0][2], masks[1][2], bd, bd16]
    return pl.pallas_call(
        _gdn_chunk_kernel,
        grid=(nb // per_step, n_chunks),
        in_specs=[pl.BlockSpec((per_step, c, 3 * W), fwd), pl.BlockSpec((per_step, c, 3 * W), bwd),
                  pl.BlockSpec((per_step, c, LANES), fwd), pl.BlockSpec((per_step, c, LANES), bwd)]
                 + [_resident(t.shape) for t in consts],
        out_specs=[pl.BlockSpec((per_step, c, W), fwd), pl.BlockSpec((per_step, c, W), bwd)],
        out_shape=[jax.ShapeDtypeStruct((nb, ttot, W), BF16)] * 2,
        scratch_shapes=[pltpu.VMEM((per_step, 2, W, W), F32)],
        compiler_params=_cparams("parallel", "arbitrary"),
        name="gdn_chunk",
    )(kvq, kvq, gb, gb, *consts)


def _merge_kernel(*refs, n_lat_tiles, split):
    x_ref, c_ref = (refs[0], refs[1]) if split else (refs[0], None)
    (h_ref, g1_ref, yf_ref, yb_ref, u5_ref, dskip_ref, gluw_ref, glub_ref, of_ref, ob_ref, z_ref,
     gain_ref, avg_ref, yfn_ref, ysg_ref, wgate_ref, wbr_ref, wout_ref, o_ref) = refs[2 if split else 1:]
    d = x_ref.shape[2]
    subs = range(x_ref.shape[0])
    y5 = [_gelu(yf_ref[s] + yb_ref[s] + dskip_ref[...] * u5_ref[s]) for s in subs]
    glu = [_dot(t.astype(BF16), gluw_ref[...]) for t in y5]
    o = [of_ref[s].astype(F32) + ob_ref[s].astype(F32) for s in subs]
    ms = [_dot_sel(t * t, avg_ref[...], 2) for t in o]
    y5 = [t * _sigmoid(g_ + glub_ref[...]) for t, g_ in zip(y5, glu)]
    yg = [o[s] * lax.rsqrt(ms[s] + EPS) * gain_ref[...] * _silu(z_ref[s].astype(F32)) for s in subs]
    ys = [(y5[s], yfn_ref[s], yg[s], ysg_ref[s]) for s in subs]
    acc = [None for _ in subs]
    for j in range(N_BRANCH):
        for s in subs:
            th = jnp.tanh(_dot(h_ref[s], wgate_ref[:, j * d:(j + 1) * d]))
            b_half = _dot(ys[s][j].astype(BF16), wbr_ref[j])
            t = th * b_half + b_half
            acc[s] = t if acc[s] is None else acc[s] + t
    out = [_dot(t.astype(BF16), wout_ref[...]) for t in acc]
    for s in subs:
        o_ref[s] = _stream_tile(x_ref, c_ref, s, n_lat_tiles) + g1_ref[s] * out[s]


def _merge(xs, mod, acts, consts, n_lat_tiles, n_tiles):
    nb, _, d = xs[0].shape
    h, yf, yb, u5, of, ob, z, yfn, ysg = acts
    dskip, gluw, glub, gain, avg, wgate, wbr, wout = consts
    return pl.pallas_call(
        functools.partial(_merge_kernel, n_lat_tiles=n_lat_tiles, split=len(xs) == 2),
        grid=(nb // PAIR, n_tiles),
        in_specs=_stream_specs(xs, d, PAIR, n_lat_tiles)
                 + [_tile_spec(d), _mod_spec(nb, d, 2, n_lat_tiles),
                  _tile_spec(W), _tile_spec(W), _tile_spec(W),
                  _resident(dskip.shape), _resident(gluw.shape), _resident(glub.shape),
                  _tile_spec(W), _tile_spec(W), _tile_spec(W),
                  _resident(gain.shape), _resident(avg.shape),
                  _tile_spec(W), _tile_spec(W), _resident(wgate.shape),
                  _resident(wbr.shape), _resident(wout.shape)],
        out_specs=_tile_spec(d),
        out_shape=jax.ShapeDtypeStruct((nb, n_tiles * ROW_TILE, d), F32),
        compiler_params=_cparams("parallel", "parallel"),
        name="merge",
    )(*xs, h, mod, yf, yb, u5, dskip, gluw, glub, of, ob, z, gain, avg, yfn, ysg, wgate, wbr, wout)


def _ffn_kernel(x_ref, sh_ref, sc_ref, g2_ref, gain_ref, w1_ref, w2_ref, nf_ref, o_ref, *, final):
    dff = w2_ref.shape[0]
    subs = range(x_ref.shape[0])
    h = [_modulated_norm(x_ref[s], gain_ref[...], sc_ref[s], sh_ref[s]).astype(BF16) for s in subs]
    t = [_dot(h_, w1_ref[...]) for h_ in h]
    act = [(_silu(t_[:, :dff]) * t_[:, dff:]).astype(BF16) for t_ in t]
    out = [_dot(a_, w2_ref[...]) for a_ in act]
    for s in subs:
        y = x_ref[s] + g2_ref[s] * out[s]
        if final:
            y = y * lax.rsqrt(jnp.mean(y * y, axis=-1, keepdims=True) + EPS) * nf_ref[...]
        o_ref[s] = y


def _ffn(x3, mod, gain, w1, w2, norm_f, n_lat_tiles, final):
    nb, ttot, d = x3.shape
    return pl.pallas_call(
        functools.partial(_ffn_kernel, final=final),
        grid=(nb // PAIR, ttot // ROW_TILE),
        in_specs=[_tile_spec(d), _mod_spec(nb, d, 3, n_lat_tiles), _mod_spec(nb, d, 4, n_lat_tiles),
                  _mod_spec(nb, d, 5, n_lat_tiles),
                  _resident(gain.shape), _resident(w1.shape), _resident(w2.shape),
                  _resident(norm_f.shape)],
        out_specs=_tile_spec(d),
        out_shape=jax.ShapeDtypeStruct(x3.shape, F32),
        compiler_params=_cparams("parallel", "parallel"),
        name="ffn",
    )(x3, mod, mod, mod, gain, w1, w2, norm_f)


def _dft_tables(n):
    idx = np.arange(n, dtype=np.int64)
    ang = 2.0 * np.pi * ((idx[:, None] * idx[None, :]) % n).astype(np.float64) / n
    return np.cos(ang), np.sin(ang)


def _s5_tables(lam_re, lam_im, log_dt, b_re, b_im, c_re, c_im):
    g, p, cg = S5_GROUPS, S5_STATE, S5_GROUP
    lam = lax.complex(lam_re.astype(F32), lam_im.astype(F32))
    a_bar = jnp.exp(lam * jnp.exp(log_dt.astype(F32))[:, None])
    b_bar = ((a_bar - 1.0) / lam)[..., None] * lax.complex(b_re.astype(F32), b_im.astype(F32))
    eye = jnp.eye(g, dtype=F32)
    bm_re = jnp.einsum('gpc,gh->gchp', b_bar.real, eye).reshape(g * cg, g * p)
    bm_im = jnp.einsum('gpc,gh->gchp', b_bar.imag, eye).reshape(g * cg, g * p)
    cm_re = jnp.einsum('gcp,gh->gphc', c_re.astype(F32), eye).reshape(g * p, g * cg)
    cm_im = jnp.einsum('gcp,gh->gphc', c_im.astype(F32), eye).reshape(g * p, g * cg)
    ng, h = N_S5 // S5_LANE_GROUP, S5_LANE_GROUP
    bmat = jnp.concatenate([bm_re.reshape(W, ng, h), bm_im.reshape(W, ng, h)], axis=2)
    bmat = jnp.transpose(bmat, (1, 0, 2)).astype(BF16)
    cmat = jnp.concatenate([cm_re.reshape(ng, h, W), -cm_im.reshape(ng, h, W)], axis=1).astype(BF16)
    avec = jnp.concatenate([a_bar.real.reshape(ng, 1, h), a_bar.imag.reshape(ng, 1, h)], axis=2)
    return bmat, cmat, avec


def _layer_consts(l, p):
    d = p['w_in'].shape[1]
    w_in = p['w_in'][l]
    sizes = (W, W, W, 2 * GDN_HEADS, 2 * GDN_HEADS, W, W, W, W, W, N_BRANCH * d)
    offs = np.concatenate([[0], np.cumsum(sizes)])
    col = lambda j: w_in[:, offs[j]:offs[j + 1]]
    u5, k, v, a, bt, q, z, ufn, usg, vsg, gate = [col(j) for j in range(len(sizes))]
    cc, sc = _dft_tables(FN_GROUP)
    fw = p['fn_w'][l].astype(F32)
    wc = jnp.einsum('cd,gde->gce', jnp.asarray(cc, F32), fw, precision=HIGHEST)
    ws = jnp.einsum('cd,gde->gce', jnp.asarray(sc, F32), fw, precision=HIGHEST)
    ufn_g = ufn.reshape(d, FN_GROUPS, FN_GROUP)
    fold_c = jnp.einsum('kgc,gce->kge', ufn_g, wc, precision=HIGHEST).reshape(d, W)
    fold_s = jnp.einsum('kgc,gce->kge', ufn_g, ws, precision=HIGHEST).reshape(d, W)
    w_ab = jnp.concatenate([a, bt, jnp.zeros((d, LANES - 4 * GDN_HEADS), F32)], axis=1)
    ws_in = [u5, jnp.concatenate([k, v, q], axis=1), w_ab, z,
             jnp.concatenate([fold_c, fold_s], axis=1), jnp.concatenate([usg, vsg], axis=1)]
    ws_in = [w.astype(BF16) for w in ws_in]
    w_gate = (0.5 * gate).astype(BF16)

    s5 = [_s5_tables(p['s5_lam_re'][l, dr], p['s5_lam_im'][l, dr], p['s5_log_dt'][l, dr],
                     p['s5_b_re'][l, dr], p['s5_b_im'][l, dr], p['s5_c_re'][l, dr],
                     p['s5_c_im'][l, dr]) for dr in range(2)]

    conv = p['gdn_conv'][l].astype(F32)
    cw = jnp.transpose(conv, (2, 0, 1)).reshape(conv.shape[2], 3 * W)
    pad = jnp.zeros((1, LANES - 2 * GDN_HEADS), F32)
    alog = jnp.concatenate([p['gdn_a_log'][l].astype(F32).reshape(1, -1), pad], axis=1)
    dtb = jnp.concatenate([p['gdn_dt_bias'][l].astype(F32).reshape(1, -1), pad], axis=1)

    sg_bias = jnp.repeat(p['sg_b'][l].astype(F32).T, SG_GROUP, axis=1)
    merge_consts = (p['s5_d'][l].astype(F32).reshape(1, W), p['s5_glu_w'][l].astype(BF16),
                    p['s5_glu_b'][l].astype(F32).reshape(1, W),
                    jnp.tile(p['gdn_norm'][l].astype(F32), GDN_HEADS).reshape(1, W))
    return dict(
        ws_in=ws_in, w_gate=w_gate, s5=s5, cw=cw, alog=alog, dtb=dtb,
        sg_w=p['sg_w'][l].astype(BF16), sg_bias=sg_bias,
        sg_lng=p['sg_ln_g'][l].astype(F32).reshape(1, W), sg_lnb=p['sg_ln_b'][l].astype(F32).reshape(1, W),
        merge=merge_consts, wbr_half=(0.5 * p['w_branch'][l]).astype(BF16),
        wout=p['w_out'][l].astype(BF16),
        w1=p['ffn_w1'][l].astype(BF16), w2=p['ffn_w2'][l].astype(BF16),
        norm1=p['norm1'][l].astype(F32).reshape(1, d), norm2=p['norm2'][l].astype(F32).reshape(1, d))


def kernel(x, c, ctx, c_ctx, ada_w, ada_b, norm1, norm2, w_in, s5_lam_re, s5_lam_im, s5_log_dt, s5_b_re, s5_b_im, s5_c_re, s5_c_im, s5_d, s5_glu_w, s5_glu_b, fn_w, gdn_conv, gdn_a_log, gdn_dt_bias, gdn_norm, sg_ln_g, sg_ln_b, sg_w, sg_b, w_branch, w_out, ffn_w1, ffn_w2, norm_f):
    p = dict(w_in=w_in, s5_lam_re=s5_lam_re, s5_lam_im=s5_lam_im, s5_log_dt=s5_log_dt,
             s5_b_re=s5_b_re, s5_b_im=s5_b_im, s5_c_re=s5_c_re, s5_c_im=s5_c_im, s5_d=s5_d,
             s5_glu_w=s5_glu_w, s5_glu_b=s5_glu_b, fn_w=fn_w, gdn_conv=gdn_conv,
             gdn_a_log=gdn_a_log, gdn_dt_bias=gdn_dt_bias, gdn_norm=gdn_norm, sg_ln_g=sg_ln_g,
             sg_ln_b=sg_ln_b, sg_w=sg_w, sg_b=sg_b, w_branch=w_branch, w_out=w_out,
             ffn_w1=ffn_w1, ffn_w2=ffn_w2, norm1=norm1, norm2=norm2)
    nb, l_lat, d = x.shape
    l_ctx = ctx.shape[1]
    depth = ada_w.shape[0]
    ttot = l_lat + l_ctx
    assert nb % SUBLANES == 0 and nb % PAIR == 0 and nb % KIN_GROUP == 0
    assert d == N_BRANCH * W and l_lat % l_ctx == 0
    assert l_lat % ROW_TILE == 0 and l_ctx % ROW_TILE == 0 and ROW_TILE % SG_CHUNK == 0
    n_lat_tiles = l_lat // ROW_TILE
    n_tiles = ttot // ROW_TILE
    sg_rows = max(r for r in range(SG_CHUNK, 7 * SG_CHUNK, SG_CHUNK) if ttot % r == 0)

    if l_ctx == ROW_TILE:
        xs = (x.astype(F32), ctx.astype(F32))
    else:
        xs = (jnp.concatenate([x, ctx], axis=1).astype(F32),)

    n_cond = nb + max(PAIR, KIN_GROUP)
    ada_rows = -(-n_cond // SUBLANES) * SUBLANES
    cc = jnp.zeros((ada_rows, d), F32).at[:nb].set(c.astype(F32)).at[nb:n_cond].set(c_ctx.astype(F32))
    mods = _ada(cc, ada_w.astype(F32), ada_b.astype(F32))[:, :n_cond].reshape(depth, n_cond, 1, 6 * d)

    cos_l, sin_l = _dft_tables(l_lat)
    cos_c, sin_c = _dft_tables(l_ctx)
    sc_l = 1.0 / math.sqrt(l_lat * FN_GROUP)
    sc_c = 1.0 / math.sqrt(l_ctx * FN_GROUP)
    cos_l, nsin_l = jnp.asarray(cos_l * sc_l, BF16), jnp.asarray(-sin_l * sc_l, BF16)
    cos_c, nsin_c = jnp.asarray(cos_c * sc_c, BF16), jnp.asarray(-sin_c * sc_c, BF16)
    nf = norm_f.astype(F32).reshape(1, d)

    lane_head = np.arange(W) // GDN_HEAD_DIM
    seg = (lane_head[:, None] == lane_head[None, :]).astype(np.float32)
    seg_ones = jnp.asarray(seg, BF16)
    seg_avg = jnp.asarray(seg / GDN_HEAD_DIM, BF16)
    expand = []
    for dr in range(2):
        e = np.zeros((LANES, 2 * W), np.float32)
        for h in range(GDN_HEADS):
            e[dr * GDN_HEADS + h, :W] = lane_head == h
            e[2 * GDN_HEADS + dr * GDN_HEADS + h, W:] = lane_head == h
        expand.append(jnp.asarray(e, BF16))
    gdn_masks = [_gdn_masks(False), _gdn_masks(True)]
    bd = jnp.asarray(seg, F32)
    bd16 = jnp.asarray(np.concatenate([seg, seg], axis=1), BF16)
    src = (np.arange(nb)[None, :] * SUBLANES + np.arange(SUBLANES)[:, None]).reshape(-1)
    perm_np = np.zeros((nb * SUBLANES, nb * SUBLANES), np.float32)
    perm_np[np.arange(nb * SUBLANES), src] = 1.0
    perm, permt = jnp.asarray(perm_np, BF16), jnp.asarray(perm_np.T, BF16)

    for l in range(depth):
        k = _layer_consts(l, p)
        mod = mods[l]
        last = l == depth - 1
        u5, kvq_p, gb, z, afn, sg, h = _kin(xs, mod, k['norm1'], k['ws_in'],
                                            (k['cw'], seg_ones, k['alog'], k['dtb']), n_lat_tiles, n_tiles)
        yf, yb = _s5(u5, perm, permt, [t[0] for t in k['s5']], [t[1] for t in k['s5']],
                     [t[2] for t in k['s5']], l_lat // S5_STEPS)
        yfn = _fnet(afn, cos_l, nsin_l, cos_c, nsin_c)
        ysg = _sgu(sg, k['sg_w'], k['sg_bias'], k['sg_lng'], k['sg_lnb'], seg_avg, sg_rows)
        of, ob = _gdn_chunk(kvq_p, gb, expand, gdn_masks, bd, bd16, l_lat, 16)
        acts = (h, yf, yb, u5, of, ob, z, yfn, ysg)
        consts = k['merge'] + (seg_avg, k['w_gate'], k['wbr_half'], k['wout'])
        x3 = _merge(xs, mod, acts, consts, n_lat_tiles, n_lat_tiles if last else n_tiles)
        xs = (_ffn(x3, mod, k['norm2'], k['w1'], k['w2'], nf, n_lat_tiles, final=last),)

    return xs[0].astype(x.dtype)
```

```python
import functools
import math

import numpy as np
import jax
import jax.numpy as jnp
from jax import lax
from jax.experimental import pallas as pl
from jax.experimental.pallas import tpu as pltpu

F32 = jnp.float32
BF16 = jnp.bfloat16
HIGHEST = lax.Precision.HIGHEST

EPS = 1e-6
W = 256
N_BRANCH = 4
S5_GROUP = 16
S5_GROUPS = W // S5_GROUP
S5_STATE = 64
N_S5 = S5_GROUPS * S5_STATE
FN_GROUPS = 4
FN_GROUP = W // FN_GROUPS
GDN_HEADS = 4
GDN_HEAD_DIM = W // GDN_HEADS
GDN_CHUNK = 64
GDN_BASE = 8
GDN_GROUP = 16
SG_GROUPS = 4
SG_GROUP = W // SG_GROUPS
SG_CHUNK = 128
LANES = 128
SUBLANES = 8
ROW_TILE = 256
PAIR = 2
KIN_GROUP = 4
S5_STEPS = 32
S5_LANE_GROUP = 256
VMEM_LIMIT = 56 * 1024 * 1024


def _cparams(*sem):
    return pltpu.CompilerParams(dimension_semantics=sem, vmem_limit_bytes=VMEM_LIMIT)


def _resident(shape):
    nd = len(shape)
    return pl.BlockSpec(shape, lambda *_: (0,) * nd, pipeline_mode=pl.Buffered(1))


def _dot(a, b):
    return jnp.dot(a, b, preferred_element_type=F32)


def _split(x, parts):
    out = []
    for _ in range(parts - 1):
        hi = x.astype(BF16)
        out.append(hi)
        x = x - hi.astype(F32)
    out.append(x.astype(BF16))
    return out


def _dot_sel(x, sel, parts):
    acc = None
    for piece in _split(x, parts):
        t = _dot(piece, sel)
        acc = t if acc is None else acc + t
    return acc


def _gelu(x):
    return 0.5 * x * (1.0 + jnp.tanh(math.sqrt(2.0 / math.pi) * (x + 0.044715 * (x * x * x))))


def _sigmoid(x):
    return 0.5 * jnp.tanh(0.5 * x) + 0.5


def _silu(x):
    return x * _sigmoid(x)


def _modulated_norm(x, gain, scale, shift):
    y = x * lax.rsqrt(jnp.mean(x * x, axis=-1, keepdims=True) + EPS) * gain
    return y * (1.0 + scale) + shift


def _ada_kernel(c_ref, w_ref, b_ref, o_ref):
    c = c_ref[...]
    o_ref[0] = jnp.dot(_silu(c), w_ref[0], preferred_element_type=F32, precision=HIGHEST) + b_ref[0]


def _ada(cc, ada_w, ada_b):
    depth, d, n = ada_w.shape
    rows = cc.shape[0]
    return pl.pallas_call(
        _ada_kernel,
        grid=(depth, n // d),
        in_specs=[pl.BlockSpec((rows, d), lambda l, j: (0, 0)),
                  pl.BlockSpec((1, d, d), lambda l, j: (l, 0, j)),
                  pl.BlockSpec((1, 1, d), lambda l, j: (l, 0, j))],
        out_specs=pl.BlockSpec((1, rows, d), lambda l, j: (l, 0, j)),
        out_shape=jax.ShapeDtypeStruct((depth, rows, n), F32),
        compiler_params=_cparams("parallel", "parallel"),
        name="ada",
    )(cc, ada_w, ada_b.reshape(depth, 1, n))


def _tile_spec(n, group=PAIR):
    return pl.BlockSpec((group, ROW_TILE, n), lambda b, j: (b, j, 0))


def _mod_spec(nb, d, k, n_lat_tiles, group=PAIR):
    return pl.BlockSpec((group, 1, d), lambda b, j: (jnp.where(j < n_lat_tiles, b, nb // group), 0, k))


def _gdn_features(kvq, before, after, cw_ref, ones_ref):
    rows, n = kvq.shape
    row = lax.broadcasted_iota(jnp.int32, (rows, n), 0)
    xp = jnp.where(row == 0, before, pltpu.roll(kvq, 1, 0))
    xn = jnp.where(row == rows - 1, after, pltpu.roll(kvq, rows - 1, 0))
    y = _silu(xp * cw_ref[0:1, :] + kvq * cw_ref[1:2, :] + xn * cw_ref[2:3, :])
    k = y[:, :W]
    q = y[:, 2 * W:]
    k = k * lax.rsqrt(_dot_sel(k * k, ones_ref[...], 2) + EPS)
    q = q * lax.rsqrt(_dot_sel(q * q, ones_ref[...], 2) + EPS) * GDN_HEAD_DIM ** -0.5
    return k, y[:, W:2 * W], q


def _gdn_gates(ab, alog_ref, dtb_ref):
    z = ab + dtb_ref[...]
    softplus = jnp.maximum(z, 0.0) + jnp.log(1.0 + jnp.exp(-jnp.abs(z)))
    g = -jnp.exp(alog_ref[...]) * softplus
    lane = lax.broadcasted_iota(jnp.int32, ab.shape, 1)
    return jnp.where(lane < 2 * GDN_HEADS, g, _sigmoid(ab))


def _stream_specs(xs, d, group, n_lat_tiles):
    if len(xs) == 1:
        return [_tile_spec(d, group)]
    return [pl.BlockSpec((group, ROW_TILE, d), lambda b, j: (b, jnp.minimum(j, n_lat_tiles - 1), 0)),
            pl.BlockSpec((group, ROW_TILE, d), lambda b, j: (b, jnp.maximum(j - n_lat_tiles, 0), 0))]


def _stream_tile(x_ref, c_ref, s, n_lat_tiles):
    if c_ref is None:
        return x_ref[s]
    return jnp.where(pl.program_id(1) >= n_lat_tiles, c_ref[s], x_ref[s])


def _kin_kernel(*refs, n_lat_tiles, n_tiles, split):
    x_ref, c_ref = (refs[0], refs[1]) if split else (refs[0], None)
    (xp_ref, xn_ref, sh_ref, sc_ref, g_ref, w5, wkvq, wab, wz, wfn, wsg, cw_ref, ones_ref, alog_ref,
     dtb_ref, o5, okvq, ogb, oz, ofn, osg, oh) = refs[2 if split else 1:]
    j = pl.program_id(1)
    subs = range(x_ref.shape[0])
    halo = xp_ref.shape[1]
    hb = [_modulated_norm(_stream_tile(x_ref, c_ref, s, n_lat_tiles), g_ref[...], sc_ref[s],
                          sh_ref[s]).astype(BF16) for s in subs]
    edge = [jnp.concatenate([xp_ref[s], xn_ref[s]], axis=0) for s in subs]
    eb = [_modulated_norm(edge[s], g_ref[...], sc_ref[s], sh_ref[s]).astype(BF16) for s in subs]
    has_prev = jnp.logical_and(j != 0, j != n_lat_tiles).astype(F32)
    has_next = jnp.logical_and(j != n_lat_tiles - 1, j != n_tiles - 1).astype(F32)
    rows = x_ref.shape[1]
    kvq = [_dot(jnp.concatenate([hb[s], eb[s]], axis=0), wkvq[...]) for s in subs]
    ab = [_dot(hb[s], wab[...]) for s in subs]
    for s in subs:
        k, v, q = _gdn_features(kvq[s][:rows], kvq[s][rows + halo - 1:rows + halo, :] * has_prev,
                                kvq[s][rows + halo:rows + halo + 1, :] * has_next, cw_ref, ones_ref)
        okvq[s, :, :W] = k.astype(okvq.dtype)
        okvq[s, :, W:2 * W] = v.astype(okvq.dtype)
        okvq[s, :, 2 * W:] = q.astype(okvq.dtype)
        ogb[s] = _gdn_gates(ab[s], alog_ref, dtb_ref)
    for s in subs:
        oh[s] = hb[s]
    for w_ref, o_ref in ((w5, o5), (wz, oz), (wfn, ofn), (wsg, osg)):
        for s in subs:
            o_ref[s] = _dot(hb[s], w_ref[...]).astype(o_ref.dtype)


def _kin(xs, mod, gain, ws, gdn_consts, n_lat_tiles, n_tiles):
    nb, _, d = xs[0].shape
    ttot = n_tiles * ROW_TILE
    widths = [W, 3 * W, LANES, W, 2 * W, 2 * W, d]
    dts = [F32, BF16, F32, BF16, BF16, BF16, BF16]
    per = ROW_TILE // SUBLANES
    last = xs[0].shape[1] // SUBLANES - 1
    grp = KIN_GROUP
    return pl.pallas_call(
        functools.partial(_kin_kernel, n_lat_tiles=n_lat_tiles, n_tiles=n_tiles, split=len(xs) == 2),
        grid=(nb // grp, n_tiles),
        in_specs=_stream_specs(xs, d, grp, n_lat_tiles)
                 + [pl.BlockSpec((grp, SUBLANES, d), lambda b, j: (b, jnp.clip(j * per - 1, 0, last), 0)),
                    pl.BlockSpec((grp, SUBLANES, d), lambda b, j: (b, jnp.clip((j + 1) * per, 0, last), 0)),
                    _mod_spec(nb, d, 0, n_lat_tiles, grp), _mod_spec(nb, d, 1, n_lat_tiles, grp),
                    _resident((1, d))] + [_resident(w.shape) for w in ws]
                 + [_resident(t.shape) for t in gdn_consts],
        out_specs=[_tile_spec(n, grp) for n in widths],
        out_shape=[jax.ShapeDtypeStruct((nb, ttot, n), dt) for n, dt in zip(widths, dts)],
        compiler_params=_cparams("parallel", "parallel"),
        name="kin",
    )(*xs, xs[0], xs[0], mod, mod, gain, *ws, *gdn_consts)


def _s5_kernel(uf_ref, ub_ref, perm_ref, permt_ref, bmf, bmb, cmf, cmb, af_ref, ab_ref,
               yf_ref, yb_ref, st_ref, bu_ref, s_ref):
    i = pl.program_id(0)
    nb, tt, _ = uf_ref.shape
    sub = SUBLANES
    n_grp, _, gw = bmf.shape
    half = gw // 2

    @pl.when(i == 0)
    def _():
        st_ref[...] = jnp.zeros_like(st_ref)

    dirs = ((uf_ref, bmf, cmf, af_ref, yf_ref, range(tt)),
            (ub_ref, bmb, cmb, ab_ref, yb_ref, range(tt - 1, -1, -1)))

    def regroup(dr):
        u_ref = dirs[dr][0]
        parts = []
        for k in range(tt // sub):
            blk = u_ref[:, k * sub:(k + 1) * sub, :].reshape(nb * sub, W).astype(BF16)
            parts.append(_dot(perm_ref[...], blk).astype(BF16))
        return jnp.concatenate(parts, axis=0)

    u_tb = [regroup(0), regroup(1)]
    y_tb = [None, None]

    def project(dr, g):
        bu_ref[dr, g] = _dot(u_tb[dr], dirs[dr][1][g])

    def recur(dr, g):
        a_ref, order = dirs[dr][3], dirs[dr][5]
        for c in range(half // LANES):
            re = slice(c * LANES, (c + 1) * LANES)
            im = slice(half + c * LANES, half + (c + 1) * LANES)
            a_re = a_ref[g, :, re]
            a_im = a_ref[g, :, im]
            s_re = st_ref[dr, g, :, re]
            s_im = st_ref[dr, g, :, im]
            for t in order:
                r = slice(t * nb, (t + 1) * nb)
                n_re = a_re * s_re - a_im * s_im + bu_ref[dr, g, r, re]
                n_im = a_re * s_im + a_im * s_re + bu_ref[dr, g, r, im]
                s_re, s_im = n_re, n_im
                s_ref[dr, g, r, re] = s_re.astype(BF16)
                s_ref[dr, g, r, im] = s_im.astype(BF16)
            st_ref[dr, g, :, re] = s_re
            st_ref[dr, g, :, im] = s_im

    def read_out(dr, g):
        t = _dot(s_ref[dr, g], dirs[dr][2][g])
        y_tb[dr] = t if y_tb[dr] is None else y_tb[dr] + t

    for dr in range(2):
        project(dr, 0)
    for g in range(n_grp):
        for dr in range(2):
            if g + 1 < n_grp:
                project(dr, g + 1)
            recur(dr, g)
        for dr in range(2):
            read_out(dr, g)
    for dr in range(2):
        y_ref = dirs[dr][4]
        for k in range(tt // sub):
            y_bt = _dot(permt_ref[...], y_tb[dr][k * sub * nb:(k + 1) * sub * nb].astype(BF16))
            y_ref[:, k * sub:(k + 1) * sub, :] = y_bt.reshape(nb, sub, W)


def _s5(u5, perm, permt, bmats, cmats, avecs, n_lat_tiles):
    nb, ttot, _ = u5.shape
    tt = S5_STEPS
    n_tiles = ttot // tt
    fwd = lambda i: (0, (i + n_lat_tiles) % n_tiles, 0)
    bwd = lambda i: (0, n_tiles - 1 - i, 0)
    rows = tt * nb
    n_grp, _, gw = bmats[0].shape
    return pl.pallas_call(
        _s5_kernel,
        grid=(n_tiles,),
        in_specs=[pl.BlockSpec((nb, tt, W), fwd), pl.BlockSpec((nb, tt, W), bwd),
                  _resident(perm.shape), _resident(permt.shape),
                  _resident(bmats[0].shape), _resident(bmats[1].shape),
                  _resident(cmats[0].shape), _resident(cmats[1].shape),
                  _resident(avecs[0].shape), _resident(avecs[1].shape)],
        out_specs=[pl.BlockSpec((nb, tt, W), fwd), pl.BlockSpec((nb, tt, W), bwd)],
        out_shape=[jax.ShapeDtypeStruct((nb, ttot, W), F32)] * 2,
        scratch_shapes=[pltpu.VMEM((2, n_grp, nb, gw), F32),
                        pltpu.VMEM((2, n_grp, rows, gw), F32),
                        pltpu.VMEM((2, n_grp, rows, gw), BF16)],
        compiler_params=_cparams("arbitrary"),
        name="s5",
    )(u5, u5, perm, permt, bmats[0], bmats[1], cmats[0], cmats[1], avecs[0], avecs[1])


def _fnet_kernel(a_ref, cosl_ref, nsinl_ref, cosc_ref, nsinc_ref, y_ref):
    l_lat = cosl_ref.shape[0]
    for rows, cos_ref, nsin_ref in ((slice(0, l_lat), cosl_ref, nsinl_ref),
                                    (slice(l_lat, y_ref.shape[0]), cosc_ref, nsinc_ref)):
        y = _dot(cos_ref[...], a_ref[rows, :W]) + _dot(nsin_ref[...], a_ref[rows, W:])
        y_ref[rows, :] = y.astype(y_ref.dtype)


def _fnet(afn, cos_l, nsin_l, cos_c, nsin_c):
    nb, ttot, _ = afn.shape
    return pl.pallas_call(
        _fnet_kernel,
        grid=(nb,),
        in_specs=[pl.BlockSpec((None, ttot, 2 * W), lambda b: (b, 0, 0)),
                  _resident(cos_l.shape), _resident(nsin_l.shape),
                  _resident(cos_c.shape), _resident(nsin_c.shape)],
        out_specs=pl.BlockSpec((None, ttot, W), lambda b: (b, 0, 0)),
        out_shape=jax.ShapeDtypeStruct((nb, ttot, W), BF16),
        compiler_params=_cparams("parallel"),
        name="fnet",
    )(afn, cos_l, nsin_l, cos_c, nsin_c)


def _sgu_kernel(uv_ref, w_ref, bias_ref, lng_ref, lnb_ref, avg_ref, y_ref):
    lane_group = lax.broadcasted_iota(jnp.int32, (SG_CHUNK, W), 1) // SG_GROUP
    rows = [slice(n * SG_CHUNK, (n + 1) * SG_CHUNK) for n in range(uv_ref.shape[0] // SG_CHUNK)]
    v = [_gelu(uv_ref[r, W:].astype(F32)) for r in rows]
    dv = [t - _dot_sel(t, avg_ref[...], 2) for t in v]
    var = [_dot_sel(t * t, avg_ref[...], 2) for t in dv]
    vn = [(d_ * lax.rsqrt(s_ + EPS) * lng_ref[...] + lnb_ref[...]).astype(BF16) for d_, s_ in zip(dv, var)]
    sv = [_dot(w_ref[0], t) for t in vn]
    for g in range(1, SG_GROUPS):
        sv = [jnp.where(lane_group == g, _dot(w_ref[g], t), s_) for t, s_ in zip(vn, sv)]
    for r, s_ in zip(rows, sv):
        y_ref[r, :] = (_gelu(uv_ref[r, :W].astype(F32)) * (s_ + bias_ref[...])).astype(y_ref.dtype)


def _sgu(sg, w, bias, lng, lnb, avg, rows):
    nb, ttot, _ = sg.shape
    return pl.pallas_call(
        _sgu_kernel,
        grid=(nb, ttot // rows),
        in_specs=[pl.BlockSpec((None, rows, 2 * W), lambda b, n: (b, n, 0)),
                  _resident(w.shape), _resident(bias.shape), _resident(lng.shape),
                  _resident(lnb.shape), _resident(avg.shape)],
        out_specs=pl.BlockSpec((None, rows, W), lambda b, n: (b, n, 0)),
        out_shape=jax.ShapeDtypeStruct((nb, ttot, W), BF16),
        compiler_params=_cparams("parallel", "parallel"),
        name="sgu",
    )(sg, w, bias, lng, lnb, avg)


def _gdn_masks(backward):
    c, heads = GDN_CHUNK, GDN_HEADS
    i = np.arange(c)[:, None]
    j = (np.arange(W) % c)[None, :]
    incl = (i <= j) if backward else (i >= j)
    strict = (i < j) if backward else (i > j)
    j64 = np.arange(c)[None, :]
    incl64 = (i <= j64) if backward else (i >= j64)
    same = lambda n: (i // n) == (j // n)
    levels = [same(GDN_BASE)]
    n = GDN_BASE
    while n < c:
        levels.append(same(2 * n) & ~same(n))
        n *= 2
    f = lambda m: jnp.asarray(m, F32)
    same_head = (np.arange(heads * c)[:, None] // c) == (np.arange(W)[None, :] // c)
    z_masks = np.stack([np.tile(m, (heads, 1)) & same_head for m in levels[1:]])
    return (f(np.stack([incl, strict, i == j])), jnp.asarray(incl64, BF16), f(np.stack(levels)),
            jnp.asarray(z_masks, BF16))


def _gdn_chains(chains, bd_ref, bd16_ref):
    c, heads = GDN_CHUNK, GDN_HEADS
    each = lambda f, *cols: [f(*args) for args in zip(*cols)]

    def expand_heads(t):
        t = t.astype(BF16)
        return jnp.concatenate([t] * heads, axis=0) * bd16_ref[:, :t.shape[1]]

    kvq, gb, expand, m64, incl64, lvls, zmasks, s_refs = zip(*chains)
    kvq = [t.astype(F32) for t in kvq]
    k = [t[:, :W] for t in kvq]
    v = [t[:, W:2 * W] for t in kvq]
    q = [t[:, 2 * W:] for t in kvq]
    def expand_gates(g, e):
        hi, lo = _split(g, 2)
        t = _dot(jnp.concatenate([hi, lo], axis=0), e[...])
        return t[:c] + t[c:]

    ge = each(expand_gates, gb, expand)
    g_l = [t[:, :W] for t in ge]
    beta_l = [t[:, W:] for t in ge]
    incl = [m[0] for m in m64]
    strict = [m[1] for m in m64]
    diag = [m[2] for m in m64]

    def cumulative(g, i64):
        acc = None
        for piece in _split(g, 2):
            t = _dot(i64[...], piece)
            acc = t if acc is None else acc + t
        return acc

    gc = each(cumulative, g_l, incl64)
    dif = each(lambda g, dg: g - jnp.sum(g * dg, axis=0, keepdims=True), gc, diag)
    g_tot = [jnp.sum(t, axis=0, keepdims=True) for t in g_l]
    kb = each(lambda a_, b_: a_ * b_, k, beta_l)
    k_st = [expand_heads(t) for t in k]
    kq = each(lambda a_, b_, st: lax.dot_general(
        jnp.concatenate([a_, b_], axis=0).astype(BF16), st, (((1,), (1,)), ((), ())),
        preferred_element_type=F32), kb, q, k_st)
    rel = each(lambda d_, i: jnp.exp(d_ * i) * i, dif, incl)
    a = each(lambda t, r, st: t[:c] * r * st, kq, rel, strict)
    qk = each(lambda t, r: t[c:] * r, kq, rel)
    pw = each(lambda t, lv: t * lv[0], a, lvls)
    t_inv = each(lambda dg, t: dg - t, diag, pw)
    pw_bd = [expand_heads(t) for t in pw]
    for _ in range(int(math.log2(GDN_BASE)) - 1):
        pw = each(lambda t, bd_: _dot(t.astype(BF16), bd_), pw, pw_bd)
        pw_bd = [expand_heads(t) for t in pw]
        t_inv = each(lambda t, bd_: t + _dot(t.astype(BF16), bd_), t_inv, pw_bd)
    a_tiled = [jnp.concatenate([t.astype(BF16)] * heads, axis=0) for t in a]
    for lvl in range(1, lvls[0].shape[0]):
        z_bd = each(lambda t, zm: t * zm[lvl - 1], a_tiled, zmasks)
        tz = each(lambda t, z_: _dot(t.astype(BF16), z_), t_inv, z_bd)
        t_inv = each(lambda t, tz_: t - _dot(tz_.astype(BF16), expand_heads(t)), t_inv, tz)
    e_gc = [jnp.exp(t) for t in gc]
    s = [r[...] for r in s_refs]
    sb = [t.astype(BF16) for t in s]
    ws = each(lambda kb_, e, q_, sb_: _dot(jnp.concatenate([kb_ * e, q_ * e], axis=0).astype(BF16), sb_),
              kb, e_gc, q, sb)
    rhs = each(lambda v_, b_, t: expand_heads(v_ * b_ - t[:c]), v, beta_l, ws)
    u = each(lambda t, r: _dot(t.astype(BF16), r), t_inv, rhs)
    o = each(lambda t, qk_, u_: t[c:] + _dot(qk_.astype(BF16), expand_heads(u_)), ws, qk, u)
    k_dec = each(lambda k_, gt, g: k_ * jnp.exp(gt - g), k, g_tot, gc)
    upd = each(lambda kd, u_: lax.dot_general(kd.astype(BF16), u_.astype(BF16), (((0,), (0,)), ((), ())),
                                              preferred_element_type=F32), k_dec, u)
    for r, s_, gt, up in zip(s_refs, s, g_tot, upd):
        r[...] = s_ * jnp.exp(gt) + up * bd_ref[...]
    return o


def _gdn_chunk_kernel(kvqf_ref, kvqb_ref, gbf_ref, gbb_ref, ef_ref, eb_ref, m64f_ref, m64b_ref,
                      i64f_ref, i64b_ref, lvlf_ref, lvlb_ref, zmf_ref, zmb_ref, bd_ref, bd16_ref,
                      of_ref, ob_ref, s_ref):
    @pl.when(pl.program_id(1) == 0)
    def _():
        s_ref[...] = jnp.zeros_like(s_ref)

    chains = []
    for bi in range(kvqf_ref.shape[0]):
        chains.append((kvqf_ref[bi], gbf_ref[bi], ef_ref, m64f_ref, i64f_ref, lvlf_ref, zmf_ref,
                       s_ref.at[bi, 0]))
        chains.append((kvqb_ref[bi], gbb_ref[bi], eb_ref, m64b_ref, i64b_ref, lvlb_ref, zmb_ref,
                       s_ref.at[bi, 1]))
    outs = _gdn_chains(chains, bd_ref, bd16_ref)
    for bi in range(kvqf_ref.shape[0]):
        of_ref[bi] = outs[2 * bi].astype(of_ref.dtype)
        ob_ref[bi] = outs[2 * bi + 1].astype(ob_ref.dtype)


def _gdn_chunk(kvq, gb, expand, masks, bd, bd16, l_lat, per_step):
    c = GDN_CHUNK
    nb, ttot, _ = kvq.shape
    n_lat = l_lat // c
    n_chunks = ttot // c
    fwd = lambda b, i: (b, (i + n_lat) % n_chunks, 0)
    bwd = lambda b, i: (b, n_chunks - 1 - i, 0)
    consts = [expand[0], expand[1], masks[0][0], masks[1][0], masks[0][1], masks[1][1],
              masks[0][2], masks[1][2], masks[0][3], masks[1][3], bd, bd16]
    return pl.pallas_call(
        _gdn_chunk_kernel,
        grid=(nb // per_step, n_chunks),
        in_specs=[pl.BlockSpec((per_step, c, 3 * W), fwd), pl.BlockSpec((per_step, c, 3 * W), bwd),
                  pl.BlockSpec((per_step, c, LANES), fwd), pl.BlockSpec((per_step, c, LANES), bwd)]
                 + [_resident(t.shape) for t in consts],
        out_specs=[pl.BlockSpec((per_step, c, W), fwd), pl.BlockSpec((per_step, c, W), bwd)],
        out_shape=[jax.ShapeDtypeStruct((nb, ttot, W), BF16)] * 2,
        scratch_shapes=[pltpu.VMEM((per_step, 2, W, W), F32)],
        compiler_params=_cparams("parallel", "arbitrary"),
        name="gdn_chunk",
    )(kvq, kvq, gb, gb, *consts)


def _merge_kernel(*refs, n_lat_tiles, split):
    x_ref, c_ref = (refs[0], refs[1]) if split else (refs[0], None)
    (h_ref, g1_ref, yf_ref, yb_ref, u5_ref, dskip_ref, gluw_ref, glub_ref, of_ref, ob_ref, z_ref,
     gain_ref, avg_ref, yfn_ref, ysg_ref, wgate_ref, wbr_ref, wout_ref, o_ref) = refs[2 if split else 1:]
    d = x_ref.shape[2]
    subs = range(x_ref.shape[0])
    acc = [None for _ in subs]

    def add_branch(j, ys):
        for s in subs:
            th = jnp.tanh(_dot(h_ref[s], wgate_ref[:, j * d:(j + 1) * d]))
            b_half = _dot(ys[s].astype(BF16), wbr_ref[j])
            t = th * b_half + b_half
            acc[s] = t if acc[s] is None else acc[s] + t

    add_branch(1, [yfn_ref[s] for s in subs])
    add_branch(3, [ysg_ref[s] for s in subs])
    y5 = [_gelu(yf_ref[s] + yb_ref[s] + dskip_ref[...] * u5_ref[s]) for s in subs]
    glu = [_dot(t.astype(BF16), gluw_ref[...]) for t in y5]
    o = [of_ref[s].astype(F32) + ob_ref[s].astype(F32) for s in subs]
    ms = [_dot_sel(t * t, avg_ref[...], 2) for t in o]
    y5 = [t * _sigmoid(g_ + glub_ref[...]) for t, g_ in zip(y5, glu)]
    yg = [o[s] * lax.rsqrt(ms[s] + EPS) * gain_ref[...] * _silu(z_ref[s].astype(F32)) for s in subs]
    add_branch(0, y5)
    add_branch(2, yg)
    out = [_dot(t.astype(BF16), wout_ref[...]) for t in acc]
    for s in subs:
        o_ref[s] = _stream_tile(x_ref, c_ref, s, n_lat_tiles) + g1_ref[s] * out[s]


def _merge(xs, mod, acts, consts, n_lat_tiles, n_tiles):
    nb, _, d = xs[0].shape
    h, yf, yb, u5, of, ob, z, yfn, ysg = acts
    dskip, gluw, glub, gain, avg, wgate, wbr, wout = consts
    return pl.pallas_call(
        functools.partial(_merge_kernel, n_lat_tiles=n_lat_tiles, split=len(xs) == 2),
        grid=(nb // PAIR, n_tiles),
        in_specs=_stream_specs(xs, d, PAIR, n_lat_tiles)
                 + [_tile_spec(d), _mod_spec(nb, d, 2, n_lat_tiles),
                  _tile_spec(W), _tile_spec(W), _tile_spec(W),
                  _resident(dskip.shape), _resident(gluw.shape), _resident(glub.shape),
                  _tile_spec(W), _tile_spec(W), _tile_spec(W),
                  _resident(gain.shape), _resident(avg.shape),
                  _tile_spec(W), _tile_spec(W), _resident(wgate.shape),
                  _resident(wbr.shape), _resident(wout.shape)],
        out_specs=_tile_spec(d),
        out_shape=jax.ShapeDtypeStruct((nb, n_tiles * ROW_TILE, d), F32),
        compiler_params=_cparams("parallel", "parallel"),
        name="merge",
    )(*xs, h, mod, yf, yb, u5, dskip, gluw, glub, of, ob, z, gain, avg, yfn, ysg, wgate, wbr, wout)


def _ffn_kernel(x_ref, sh_ref, sc_ref, g2_ref, gain_ref, w1_ref, w2_ref, nf_ref, o_ref, *, final):
    dff = w2_ref.shape[0]
    subs = range(x_ref.shape[0])
    h = [_modulated_norm(x_ref[s], gain_ref[...], sc_ref[s], sh_ref[s]).astype(BF16) for s in subs]
    t = [_dot(h_, w1_ref[...]) for h_ in h]
    act = [(_silu(t_[:, :dff]) * t_[:, dff:]).astype(BF16) for t_ in t]
    out = [_dot(a_, w2_ref[...]) for a_ in act]
    for s in subs:
        y = x_ref[s] + g2_ref[s] * out[s]
        if final:
            y = y * lax.rsqrt(jnp.mean(y * y, axis=-1, keepdims=True) + EPS) * nf_ref[...]
        o_ref[s] = y


def _ffn(x3, mod, gain, w1, w2, norm_f, n_lat_tiles, final):
    nb, ttot, d = x3.shape
    return pl.pallas_call(
        functools.partial(_ffn_kernel, final=final),
        grid=(nb // PAIR, ttot // ROW_TILE),
        in_specs=[_tile_spec(d), _mod_spec(nb, d, 3, n_lat_tiles), _mod_spec(nb, d, 4, n_lat_tiles),
                  _mod_spec(nb, d, 5, n_lat_tiles),
                  _resident(gain.shape), _resident(w1.shape), _resident(w2.shape),
                  _resident(norm_f.shape)],
        out_specs=_tile_spec(d),
        out_shape=jax.ShapeDtypeStruct(x3.shape, F32),
        compiler_params=_cparams("parallel", "parallel"),
        name="ffn",
    )(x3, mod, mod, mod, gain, w1, w2, norm_f)


def _dft_tables(n):
    idx = np.arange(n, dtype=np.int64)
    ang = 2.0 * np.pi * ((idx[:, None] * idx[None, :]) % n).astype(np.float64) / n
    return np.cos(ang), np.sin(ang)


def _s5_tables(lam_re, lam_im, log_dt, b_re, b_im, c_re, c_im):
    g, p, cg = S5_GROUPS, S5_STATE, S5_GROUP
    lam = lax.complex(lam_re.astype(F32), lam_im.astype(F32))
    a_bar = jnp.exp(lam * jnp.exp(log_dt.astype(F32))[:, None])
    b_bar = ((a_bar - 1.0) / lam)[..., None] * lax.complex(b_re.astype(F32), b_im.astype(F32))
    eye = jnp.eye(g, dtype=F32)
    bm_re = jnp.einsum('gpc,gh->gchp', b_bar.real, eye).reshape(g * cg, g * p)
    bm_im = jnp.einsum('gpc,gh->gchp', b_bar.imag, eye).reshape(g * cg, g * p)
    cm_re = jnp.einsum('gcp,gh->gphc', c_re.astype(F32), eye).reshape(g * p, g * cg)
    cm_im = jnp.einsum('gcp,gh->gphc', c_im.astype(F32), eye).reshape(g * p, g * cg)
    ng, h = N_S5 // S5_LANE_GROUP, S5_LANE_GROUP
    bmat = jnp.concatenate([bm_re.reshape(W, ng, h), bm_im.reshape(W, ng, h)], axis=2)
    bmat = jnp.transpose(bmat, (1, 0, 2)).astype(BF16)
    cmat = jnp.concatenate([cm_re.reshape(ng, h, W), -cm_im.reshape(ng, h, W)], axis=1).astype(BF16)
    avec = jnp.concatenate([a_bar.real.reshape(ng, 1, h), a_bar.imag.reshape(ng, 1, h)], axis=2)
    return bmat, cmat, avec


def _layer_consts(l, p):
    d = p['w_in'].shape[1]
    w_in = p['w_in'][l]
    sizes = (W, W, W, 2 * GDN_HEADS, 2 * GDN_HEADS, W, W, W, W, W, N_BRANCH * d)
    offs = np.concatenate([[0], np.cumsum(sizes)])
    col = lambda j: w_in[:, offs[j]:offs[j + 1]]
    u5, k, v, a, bt, q, z, ufn, usg, vsg, gate = [col(j) for j in range(len(sizes))]
    cc, sc = _dft_tables(FN_GROUP)
    fw = p['fn_w'][l].astype(F32)
    wc = jnp.einsum('cd,gde->gce', jnp.asarray(cc, F32), fw, precision=HIGHEST)
    ws = jnp.einsum('cd,gde->gce', jnp.asarray(sc, F32), fw, precision=HIGHEST)
    ufn_g = ufn.reshape(d, FN_GROUPS, FN_GROUP)
    fold_c = jnp.einsum('kgc,gce->kge', ufn_g, wc, precision=HIGHEST).reshape(d, W)
    fold_s = jnp.einsum('kgc,gce->kge', ufn_g, ws, precision=HIGHEST).reshape(d, W)
    w_ab = jnp.concatenate([a, bt, jnp.zeros((d, LANES - 4 * GDN_HEADS), F32)], axis=1)
    ws_in = [u5, jnp.concatenate([k, v, q], axis=1), w_ab, z,
             jnp.concatenate([fold_c, fold_s], axis=1), jnp.concatenate([usg, vsg], axis=1)]
    ws_in = [w.astype(BF16) for w in ws_in]
    w_gate = (0.5 * gate).astype(BF16)

    s5 = [_s5_tables(p['s5_lam_re'][l, dr], p['s5_lam_im'][l, dr], p['s5_log_dt'][l, dr],
                     p['s5_b_re'][l, dr], p['s5_b_im'][l, dr], p['s5_c_re'][l, dr],
                     p['s5_c_im'][l, dr]) for dr in range(2)]

    conv = p['gdn_conv'][l].astype(F32)
    cw = jnp.transpose(conv, (2, 0, 1)).reshape(conv.shape[2], 3 * W)
    pad = jnp.zeros((1, LANES - 2 * GDN_HEADS), F32)
    alog = jnp.concatenate([p['gdn_a_log'][l].astype(F32).reshape(1, -1), pad], axis=1)
    dtb = jnp.concatenate([p['gdn_dt_bias'][l].astype(F32).reshape(1, -1), pad], axis=1)

    sg_bias = jnp.repeat(p['sg_b'][l].astype(F32).T, SG_GROUP, axis=1)
    merge_consts = (p['s5_d'][l].astype(F32).reshape(1, W), p['s5_glu_w'][l].astype(BF16),
                    p['s5_glu_b'][l].astype(F32).reshape(1, W),
                    jnp.tile(p['gdn_norm'][l].astype(F32), GDN_HEADS).reshape(1, W))
    return dict(
        ws_in=ws_in, w_gate=w_gate, s5=s5, cw=cw, alog=alog, dtb=dtb,
        sg_w=p['sg_w'][l].astype(BF16), sg_bias=sg_bias,
        sg_lng=p['sg_ln_g'][l].astype(F32).reshape(1, W), sg_lnb=p['sg_ln_b'][l].astype(F32).reshape(1, W),
        merge=merge_consts, wbr_half=(0.5 * p['w_branch'][l]).astype(BF16),
        wout=p['w_out'][l].astype(BF16),
        w1=p['ffn_w1'][l].astype(BF16), w2=p['ffn_w2'][l].astype(BF16),
        norm1=p['norm1'][l].astype(F32).reshape(1, d), norm2=p['norm2'][l].astype(F32).reshape(1, d))


def kernel(x, c, ctx, c_ctx, ada_w, ada_b, norm1, norm2, w_in, s5_lam_re, s5_lam_im, s5_log_dt, s5_b_re, s5_b_im, s5_c_re, s5_c_im, s5_d, s5_glu_w, s5_glu_b, fn_w, gdn_conv, gdn_a_log, gdn_dt_bias, gdn_norm, sg_ln_g, sg_ln_b, sg_w, sg_b, w_branch, w_out, ffn_w1, ffn_w2, norm_f):
    p = dict(w_in=w_in, s5_lam_re=s5_lam_re, s5_lam_im=s5_lam_im, s5_log_dt=s5_log_dt,
             s5_b_re=s5_b_re, s5_b_im=s5_b_im, s5_c_re=s5_c_re, s5_c_im=s5_c_im, s5_d=s5_d,
             s5_glu_w=s5_glu_w, s5_glu_b=s5_glu_b, fn_w=fn_w, gdn_conv=gdn_conv,
             gdn_a_log=gdn_a_log, gdn_dt_bias=gdn_dt_bias, gdn_norm=gdn_norm, sg_ln_g=sg_ln_g,
             sg_ln_b=sg_ln_b, sg_w=sg_w, sg_b=sg_b, w_branch=w_branch, w_out=w_out,
             ffn_w1=ffn_w1, ffn_w2=ffn_w2, norm1=norm1, norm2=norm2)
    nb, l_lat, d = x.shape
    l_ctx = ctx.shape[1]
    depth = ada_w.shape[0]
    ttot = l_lat + l_ctx
    assert nb % SUBLANES == 0 and nb % PAIR == 0 and nb % KIN_GROUP == 0
    assert d == N_BRANCH * W and l_lat % l_ctx == 0
    assert l_lat % ROW_TILE == 0 and l_ctx % ROW_TILE == 0 and ROW_TILE % SG_CHUNK == 0
    n_lat_tiles = l_lat // ROW_TILE
    n_tiles = ttot // ROW_TILE
    sg_rows = max(r for r in range(SG_CHUNK, 7 * SG_CHUNK, SG_CHUNK) if ttot % r == 0)
    gdn_group = math.gcd(nb, GDN_GROUP)

    if l_ctx == ROW_TILE:
        xs = (x.astype(F32), ctx.astype(F32))
    else:
        xs = (jnp.concatenate([x, ctx], axis=1).astype(F32),)

    n_cond = nb + max(PAIR, KIN_GROUP)
    ada_rows = -(-n_cond // SUBLANES) * SUBLANES
    cc = jnp.zeros((ada_rows, d), F32).at[:nb].set(c.astype(F32)).at[nb:n_cond].set(c_ctx.astype(F32))
    mods = _ada(cc, ada_w.astype(F32), ada_b.astype(F32))[:, :n_cond].reshape(depth, n_cond, 1, 6 * d)

    cos_l, sin_l = _dft_tables(l_lat)
    cos_c, sin_c = _dft_tables(l_ctx)
    sc_l = 1.0 / math.sqrt(l_lat * FN_GROUP)
    sc_c = 1.0 / math.sqrt(l_ctx * FN_GROUP)
    cos_l, nsin_l = jnp.asarray(cos_l * sc_l, BF16), jnp.asarray(-sin_l * sc_l, BF16)
    cos_c, nsin_c = jnp.asarray(cos_c * sc_c, BF16), jnp.asarray(-sin_c * sc_c, BF16)
    nf = norm_f.astype(F32).reshape(1, d)

    lane_head = np.arange(W) // GDN_HEAD_DIM
    seg = (lane_head[:, None] == lane_head[None, :]).astype(np.float32)
    seg_ones = jnp.asarray(seg, BF16)
    seg_avg = jnp.asarray(seg / GDN_HEAD_DIM, BF16)
    expand = []
    for dr in range(2):
        e = np.zeros((LANES, 2 * W), np.float32)
        for h in range(GDN_HEADS):
            e[dr * GDN_HEADS + h, :W] = lane_head == h
            e[2 * GDN_HEADS + dr * GDN_HEADS + h, W:] = lane_head == h
        expand.append(jnp.asarray(e, BF16))
    gdn_masks = [_gdn_masks(False), _gdn_masks(True)]
    bd = jnp.asarray(seg, F32)
    bd16 = jnp.asarray(np.concatenate([seg, seg], axis=1), BF16)
    src = (np.arange(nb)[None, :] * SUBLANES + np.arange(SUBLANES)[:, None]).reshape(-1)
    perm_np = np.zeros((nb * SUBLANES, nb * SUBLANES), np.float32)
    perm_np[np.arange(nb * SUBLANES), src] = 1.0
    perm, permt = jnp.asarray(perm_np, BF16), jnp.asarray(perm_np.T, BF16)

    for l in range(depth):
        k = _layer_consts(l, p)
        mod = mods[l]
        last = l == depth - 1
        u5, kvq_p, gb, z, afn, sg, h = _kin(xs, mod, k['norm1'], k['ws_in'],
                                            (k['cw'], seg_ones, k['alog'], k['dtb']), n_lat_tiles, n_tiles)
        yf, yb = _s5(u5, perm, permt, [t[0] for t in k['s5']], [t[1] for t in k['s5']],
                     [t[2] for t in k['s5']], l_lat // S5_STEPS)
        yfn = _fnet(afn, cos_l, nsin_l, cos_c, nsin_c)
        ysg = _sgu(sg, k['sg_w'], k['sg_bias'], k['sg_lng'], k['sg_lnb'], seg_avg, sg_rows)
        of, ob = _gdn_chunk(kvq_p, gb, expand, gdn_masks, bd, bd16, l_lat, gdn_group)
        acts = (h, yf, yb, u5, of, ob, z, yfn, ysg)
        consts = k['merge'] + (seg_avg, k['w_gate'], k['wbr_half'], k['wout'])
        x3 = _merge(xs, mod, acts, consts, n_lat_tiles, n_lat_tiles if last else n_tiles)
        xs = (_ffn(x3, mod, k['norm2'], k['w1'], k['w2'], nf, n_lat_tiles, final=last),)

    return xs[0].astype(x.dtype)
```

```python
import functools
import math

import numpy as np
import jax
import jax.numpy as jnp
from jax import lax
from jax.experimental import pallas as pl
from jax.experimental.pallas import tpu as pltpu

F32 = jnp.float32
BF16 = jnp.bfloat16
HIGHEST = lax.Precision.HIGHEST

EPS = 1e-6
W = 256
N_BRANCH = 4
S5_GROUP = 16
S5_GROUPS = W // S5_GROUP
S5_STATE = 64
N_S5 = S5_GROUPS * S5_STATE
FN_GROUPS = 4
FN_GROUP = W // FN_GROUPS
GDN_HEADS = 4
GDN_HEAD_DIM = W // GDN_HEADS
GDN_CHUNK = 64
GDN_BASE = 8
GDN_GROUP = 16
SG_GROUPS = 4
SG_GROUP = W // SG_GROUPS
SG_CHUNK = 128
LANES = 128
SUBLANES = 8
ROW_TILE = 256
PAIR = 2
KIN_GROUP = 4
S5_STEPS = 32
S5_LANE_GROUP = 256
VMEM_LIMIT = 56 * 1024 * 1024


def _cparams(*sem):
    return pltpu.CompilerParams(dimension_semantics=sem, vmem_limit_bytes=VMEM_LIMIT)


def _resident(shape):
    nd = len(shape)
    return pl.BlockSpec(shape, lambda *_: (0,) * nd, pipeline_mode=pl.Buffered(1))


def _dot(a, b):
    return jnp.dot(a, b, preferred_element_type=F32)


def _split(x, parts):
    out = []
    for _ in range(parts - 1):
        hi = x.astype(BF16)
        out.append(hi)
        x = x - hi.astype(F32)
    out.append(x.astype(BF16))
    return out


def _dot_sel(x, sel, parts):
    acc = None
    for piece in _split(x, parts):
        t = _dot(piece, sel)
        acc = t if acc is None else acc + t
    return acc


def _gelu(x):
    return 0.5 * x * (1.0 + jnp.tanh(math.sqrt(2.0 / math.pi) * (x + 0.044715 * (x * x * x))))


def _sigmoid(x):
    return 0.5 * jnp.tanh(0.5 * x) + 0.5


def _silu(x):
    return x * _sigmoid(x)


def _modulated_norm(x, gain, scale, shift):
    y = x * lax.rsqrt(jnp.mean(x * x, axis=-1, keepdims=True) + EPS) * gain
    return y * (1.0 + scale) + shift


def _ada_kernel(c_ref, w_ref, b_ref, o_ref):
    c = c_ref[...]
    o_ref[0] = jnp.dot(_silu(c), w_ref[0], preferred_element_type=F32, precision=HIGHEST) + b_ref[0]


def _ada(cc, ada_w, ada_b):
    depth, d, n = ada_w.shape
    rows = cc.shape[0]
    return pl.pallas_call(
        _ada_kernel,
        grid=(depth, n // d),
        in_specs=[pl.BlockSpec((rows, d), lambda l, j: (0, 0)),
                  pl.BlockSpec((1, d, d), lambda l, j: (l, 0, j)),
                  pl.BlockSpec((1, 1, d), lambda l, j: (l, 0, j))],
        out_specs=pl.BlockSpec((1, rows, d), lambda l, j: (l, 0, j)),
        out_shape=jax.ShapeDtypeStruct((depth, rows, n), F32),
        compiler_params=_cparams("parallel", "parallel"),
        name="ada",
    )(cc, ada_w, ada_b.reshape(depth, 1, n))


def _tile_spec(n, group=PAIR):
    return pl.BlockSpec((group, ROW_TILE, n), lambda b, j: (b, j, 0))


def _mod_spec(nb, d, k, n_lat_tiles, group=PAIR):
    return pl.BlockSpec((group, 1, d), lambda b, j: (jnp.where(j < n_lat_tiles, b, nb // group), 0, k))


def _gdn_features(kvq, before, after, cw_ref, ones_ref):
    rows, n = kvq.shape
    row = lax.broadcasted_iota(jnp.int32, (rows, n), 0)
    xp = jnp.where(row == 0, before, pltpu.roll(kvq, 1, 0))
    xn = jnp.where(row == rows - 1, after, pltpu.roll(kvq, rows - 1, 0))
    y = _silu(xp * cw_ref[0:1, :] + kvq * cw_ref[1:2, :] + xn * cw_ref[2:3, :])
    k = y[:, :W]
    q = y[:, 2 * W:]
    k = k * lax.rsqrt(_dot_sel(k * k, ones_ref[...], 2) + EPS)
    q = q * lax.rsqrt(_dot_sel(q * q, ones_ref[...], 2) + EPS) * GDN_HEAD_DIM ** -0.5
    return k, y[:, W:2 * W], q


def _gdn_gates(ab, alog_ref, dtb_ref):
    z = ab + dtb_ref[...]
    softplus = jnp.maximum(z, 0.0) + jnp.log(1.0 + jnp.exp(-jnp.abs(z)))
    g = -jnp.exp(alog_ref[...]) * softplus
    lane = lax.broadcasted_iota(jnp.int32, ab.shape, 1)
    return jnp.where(lane < 2 * GDN_HEADS, g, _sigmoid(ab))


def _stream_specs(xs, d, group, n_lat_tiles):
    if len(xs) == 1:
        return [_tile_spec(d, group)]
    return [pl.BlockSpec((group, ROW_TILE, d), lambda b, j: (b, jnp.minimum(j, n_lat_tiles - 1), 0)),
            pl.BlockSpec((group, ROW_TILE, d), lambda b, j: (b, jnp.maximum(j - n_lat_tiles, 0), 0))]


def _stream_tile(x_ref, c_ref, s, n_lat_tiles):
    if c_ref is None:
        return x_ref[s]
    return jnp.where(pl.program_id(1) >= n_lat_tiles, c_ref[s], x_ref[s])


def _kin_kernel(*refs, n_lat_tiles, n_tiles, split):
    x_ref, c_ref = (refs[0], refs[1]) if split else (refs[0], None)
    (xp_ref, xn_ref, sh_ref, sc_ref, g_ref, w5, wkvq, wab, wz, wfn, wsg, cw_ref, ones_ref, alog_ref,
     dtb_ref, o5, okvq, ogb, oz, ofn, osg, oh) = refs[2 if split else 1:]
    j = pl.program_id(1)
    subs = range(x_ref.shape[0])
    halo = xp_ref.shape[1]
    hb = [_modulated_norm(_stream_tile(x_ref, c_ref, s, n_lat_tiles), g_ref[...], sc_ref[s],
                          sh_ref[s]).astype(BF16) for s in subs]
    edge = [jnp.concatenate([xp_ref[s], xn_ref[s]], axis=0) for s in subs]
    eb = [_modulated_norm(edge[s], g_ref[...], sc_ref[s], sh_ref[s]).astype(BF16) for s in subs]
    has_prev = jnp.logical_and(j != 0, j != n_lat_tiles).astype(F32)
    has_next = jnp.logical_and(j != n_lat_tiles - 1, j != n_tiles - 1).astype(F32)
    for s in subs:
        oh[s] = hb[s]
    for w_ref, o_ref in ((w5, o5), (wz, oz), (wfn, ofn), (wsg, osg)):
        for s in subs:
            o_ref[s] = _dot(hb[s], w_ref[...]).astype(o_ref.dtype)
    rows = x_ref.shape[1]
    kvq = [_dot(jnp.concatenate([hb[s], eb[s]], axis=0), wkvq[...]) for s in subs]
    for s in subs:
        k, v, q = _gdn_features(kvq[s][:rows], kvq[s][rows + halo - 1:rows + halo, :] * has_prev,
                                kvq[s][rows + halo:rows + halo + 1, :] * has_next, cw_ref, ones_ref)
        okvq[s, :, :W] = k.astype(okvq.dtype)
        okvq[s, :, W:2 * W] = v.astype(okvq.dtype)
        okvq[s, :, 2 * W:] = q.astype(okvq.dtype)
    for s in subs:
        ogb[s] = _gdn_gates(_dot(hb[s], wab[...]), alog_ref, dtb_ref)


def _kin(xs, mod, gain, ws, gdn_consts, n_lat_tiles, n_tiles):
    nb, _, d = xs[0].shape
    ttot = n_tiles * ROW_TILE
    widths = [W, 3 * W, LANES, W, 2 * W, 2 * W, d]
    dts = [F32, BF16, F32, BF16, BF16, BF16, BF16]
    per = ROW_TILE // SUBLANES
    last = xs[0].shape[1] // SUBLANES - 1
    grp = KIN_GROUP
    return pl.pallas_call(
        functools.partial(_kin_kernel, n_lat_tiles=n_lat_tiles, n_tiles=n_tiles, split=len(xs) == 2),
        grid=(nb // grp, n_tiles),
        in_specs=_stream_specs(xs, d, grp, n_lat_tiles)
                 + [pl.BlockSpec((grp, SUBLANES, d), lambda b, j: (b, jnp.clip(j * per - 1, 0, last), 0)),
                    pl.BlockSpec((grp, SUBLANES, d), lambda b, j: (b, jnp.clip((j + 1) * per, 0, last), 0)),
                    _mod_spec(nb, d, 0, n_lat_tiles, grp), _mod_spec(nb, d, 1, n_lat_tiles, grp),
                    _resident((1, d))] + [_resident(w.shape) for w in ws]
                 + [_resident(t.shape) for t in gdn_consts],
        out_specs=[_tile_spec(n, grp) for n in widths],
        out_shape=[jax.ShapeDtypeStruct((nb, ttot, n), dt) for n, dt in zip(widths, dts)],
        compiler_params=_cparams("parallel", "parallel"),
        name="kin",
    )(*xs, xs[0], xs[0], mod, mod, gain, *ws, *gdn_consts)


def _s5_kernel(uf_ref, ub_ref, perm_ref, permt_ref, bmf, bmb, cmf, cmb, af_ref, ab_ref,
               yf_ref, yb_ref, st_ref, bu_ref, s_ref):
    i = pl.program_id(0)
    nb, tt, _ = uf_ref.shape
    sub = SUBLANES
    n_grp, _, gw = bmf.shape
    half = gw // 2

    @pl.when(i == 0)
    def _():
        st_ref[...] = jnp.zeros_like(st_ref)

    dirs = ((uf_ref, bmf, cmf, af_ref, yf_ref, range(tt)),
            (ub_ref, bmb, cmb, ab_ref, yb_ref, range(tt - 1, -1, -1)))

    def regroup(dr):
        u_ref = dirs[dr][0]
        parts = []
        for k in range(tt // sub):
            blk = u_ref[:, k * sub:(k + 1) * sub, :].reshape(nb * sub, W).astype(BF16)
            parts.append(_dot(perm_ref[...], blk).astype(BF16))
        return jnp.concatenate(parts, axis=0)

    u_tb = [regroup(0), regroup(1)]
    y_tb = [None, None]

    def project(dr, g):
        bu_ref[dr, g] = _dot(u_tb[dr], dirs[dr][1][g])

    def recur(dr, g):
        a_ref, order = dirs[dr][3], dirs[dr][5]
        for c in range(half // LANES):
            re = slice(c * LANES, (c + 1) * LANES)
            im = slice(half + c * LANES, half + (c + 1) * LANES)
            a_re = a_ref[g, :, re]
            a_im = a_ref[g, :, im]
            s_re = st_ref[dr, g, :, re]
            s_im = st_ref[dr, g, :, im]
            for t in order:
                r = slice(t * nb, (t + 1) * nb)
                n_re = a_re * s_re - a_im * s_im + bu_ref[dr, g, r, re]
                n_im = a_re * s_im + a_im * s_re + bu_ref[dr, g, r, im]
                s_re, s_im = n_re, n_im
                s_ref[dr, g, r, re] = s_re.astype(BF16)
                s_ref[dr, g, r, im] = s_im.astype(BF16)
            st_ref[dr, g, :, re] = s_re
            st_ref[dr, g, :, im] = s_im

    def read_out(dr, g):
        t = _dot(s_ref[dr, g], dirs[dr][2][g])
        y_tb[dr] = t if y_tb[dr] is None else y_tb[dr] + t

    for dr in range(2):
        project(dr, 0)
    for g in range(n_grp):
        for dr in range(2):
            if g + 1 < n_grp:
                project(dr, g + 1)
            recur(dr, g)
        for dr in range(2):
            read_out(dr, g)
    for dr in range(2):
        y_ref = dirs[dr][4]
        for k in range(tt // sub):
            y_bt = _dot(permt_ref[...], y_tb[dr][k * sub * nb:(k + 1) * sub * nb].astype(BF16))
            y_ref[:, k * sub:(k + 1) * sub, :] = y_bt.reshape(nb, sub, W)


def _s5(u5, perm, permt, bmats, cmats, avecs, n_lat_tiles):
    nb, ttot, _ = u5.shape
    tt = S5_STEPS
    n_tiles = ttot // tt
    fwd = lambda i: (0, (i + n_lat_tiles) % n_tiles, 0)
    bwd = lambda i: (0, n_tiles - 1 - i, 0)
    rows = tt * nb
    n_grp, _, gw = bmats[0].shape
    return pl.pallas_call(
        _s5_kernel,
        grid=(n_tiles,),
        in_specs=[pl.BlockSpec((nb, tt, W), fwd), pl.BlockSpec((nb, tt, W), bwd),
                  _resident(perm.shape), _resident(permt.shape),
                  _resident(bmats[0].shape), _resident(bmats[1].shape),
                  _resident(cmats[0].shape), _resident(cmats[1].shape),
                  _resident(avecs[0].shape), _resident(avecs[1].shape)],
        out_specs=[pl.BlockSpec((nb, tt, W), fwd), pl.BlockSpec((nb, tt, W), bwd)],
        out_shape=[jax.ShapeDtypeStruct((nb, ttot, W), F32)] * 2,
        scratch_shapes=[pltpu.VMEM((2, n_grp, nb, gw), F32),
                        pltpu.VMEM((2, n_grp, rows, gw), F32),
                        pltpu.VMEM((2, n_grp, rows, gw), BF16)],
        compiler_params=_cparams("arbitrary"),
        name="s5",
    )(u5, u5, perm, permt, bmats[0], bmats[1], cmats[0], cmats[1], avecs[0], avecs[1])


def _fnet_kernel(a_ref, cosl_ref, nsinl_ref, cosc_ref, nsinc_ref, y_ref):
    l_lat = cosl_ref.shape[0]
    for rows, cos_ref, nsin_ref in ((slice(0, l_lat), cosl_ref, nsinl_ref),
                                    (slice(l_lat, y_ref.shape[0]), cosc_ref, nsinc_ref)):
        y = _dot(cos_ref[...], a_ref[rows, :W]) + _dot(nsin_ref[...], a_ref[rows, W:])
        y_ref[rows, :] = y.astype(y_ref.dtype)


def _fnet(afn, cos_l, nsin_l, cos_c, nsin_c):
    nb, ttot, _ = afn.shape
    return pl.pallas_call(
        _fnet_kernel,
        grid=(nb,),
        in_specs=[pl.BlockSpec((None, ttot, 2 * W), lambda b: (b, 0, 0)),
                  _resident(cos_l.shape), _resident(nsin_l.shape),
                  _resident(cos_c.shape), _resident(nsin_c.shape)],
        out_specs=pl.BlockSpec((None, ttot, W), lambda b: (b, 0, 0)),
        out_shape=jax.ShapeDtypeStruct((nb, ttot, W), BF16),
        compiler_params=_cparams("parallel"),
        name="fnet",
    )(afn, cos_l, nsin_l, cos_c, nsin_c)


def _sgu_kernel(uv_ref, w_ref, bias_ref, lng_ref, lnb_ref, avg_ref, y_ref):
    lane_group = lax.broadcasted_iota(jnp.int32, (SG_CHUNK, W), 1) // SG_GROUP
    rows = [slice(n * SG_CHUNK, (n + 1) * SG_CHUNK) for n in range(uv_ref.shape[0] // SG_CHUNK)]
    v = [_gelu(uv_ref[r, W:].astype(F32)) for r in rows]
    dv = [t - _dot_sel(t, avg_ref[...], 2) for t in v]
    var = [_dot_sel(t * t, avg_ref[...], 2) for t in dv]
    vn = [(d_ * lax.rsqrt(s_ + EPS) * lng_ref[...] + lnb_ref[...]).astype(BF16) for d_, s_ in zip(dv, var)]
    sv = [_dot(w_ref[0], t) for t in vn]
    for g in range(1, SG_GROUPS):
        sv = [jnp.where(lane_group == g, _dot(w_ref[g], t), s_) for t, s_ in zip(vn, sv)]
    for r, s_ in zip(rows, sv):
        y_ref[r, :] = (_gelu(uv_ref[r, :W].astype(F32)) * (s_ + bias_ref[...])).astype(y_ref.dtype)


def _sgu(sg, w, bias, lng, lnb, avg, rows):
    nb, ttot, _ = sg.shape
    return pl.pallas_call(
        _sgu_kernel,
        grid=(nb, ttot // rows),
        in_specs=[pl.BlockSpec((None, rows, 2 * W), lambda b, n: (b, n, 0)),
                  _resident(w.shape), _resident(bias.shape), _resident(lng.shape),
                  _resident(lnb.shape), _resident(avg.shape)],
        out_specs=pl.BlockSpec((None, rows, W), lambda b, n: (b, n, 0)),
        out_shape=jax.ShapeDtypeStruct((nb, ttot, W), BF16),
        compiler_params=_cparams("parallel", "parallel"),
        name="sgu",
    )(sg, w, bias, lng, lnb, avg)


def _gdn_masks(backward):
    c, heads = GDN_CHUNK, GDN_HEADS
    i = np.arange(c)[:, None]
    j = (np.arange(W) % c)[None, :]
    incl = (i <= j) if backward else (i >= j)
    strict = (i < j) if backward else (i > j)
    j64 = np.arange(c)[None, :]
    incl64 = (i <= j64) if backward else (i >= j64)
    same = lambda n: (i // n) == (j // n)
    levels = [same(GDN_BASE)]
    n = GDN_BASE
    while n < c:
        levels.append(same(2 * n) & ~same(n))
        n *= 2
    f = lambda m: jnp.asarray(m, F32)
    same_head = (np.arange(heads * c)[:, None] // c) == (np.arange(W)[None, :] // c)
    z_masks = np.stack([np.tile(m, (heads, 1)) & same_head for m in levels[1:]])
    return (f(np.stack([incl, strict, i == j])), jnp.asarray(incl64, BF16), f(np.stack(levels)),
            jnp.asarray(z_masks, BF16))


def _gdn_chains(chains, bd_ref, bd16_ref):
    c, heads = GDN_CHUNK, GDN_HEADS
    each = lambda f, *cols: [f(*args) for args in zip(*cols)]

    def expand_heads(t):
        t = t.astype(BF16)
        return jnp.concatenate([t] * heads, axis=0) * bd16_ref[:, :t.shape[1]]

    kvq, gb, expand, m64, incl64, lvls, zmasks, s_refs = zip(*chains)
    kvq = [t.astype(F32) for t in kvq]
    k = [t[:, :W] for t in kvq]
    v = [t[:, W:2 * W] for t in kvq]
    q = [t[:, 2 * W:] for t in kvq]
    def expand_gates(g, e):
        hi, lo = _split(g, 2)
        t = _dot(jnp.concatenate([hi, lo], axis=0), e[...])
        return t[:c] + t[c:]

    ge = each(expand_gates, gb, expand)
    g_l = [t[:, :W] for t in ge]
    beta_l = [t[:, W:] for t in ge]
    incl = [m[0] for m in m64]
    strict = [m[1] for m in m64]
    diag = [m[2] for m in m64]

    def cumulative(g, i64):
        acc = None
        for piece in _split(g, 2):
            t = _dot(i64[...], piece)
            acc = t if acc is None else acc + t
        return acc

    gc = each(cumulative, g_l, incl64)
    dif = each(lambda g, dg: g - jnp.sum(g * dg, axis=0, keepdims=True), gc, diag)
    g_tot = [jnp.sum(t, axis=0, keepdims=True) for t in g_l]
    kb = each(lambda a_, b_: a_ * b_, k, beta_l)
    k_st = [expand_heads(t) for t in k]
    kq = each(lambda a_, b_, st: lax.dot_general(
        jnp.concatenate([a_, b_], axis=0).astype(BF16), st, (((1,), (1,)), ((), ())),
        preferred_element_type=F32), kb, q, k_st)
    rel = each(lambda d_, i: jnp.exp(d_ * i) * i, dif, incl)
    a = each(lambda t, r, st: t[:c] * r * st, kq, rel, strict)
    qk = each(lambda t, r: t[c:] * r, kq, rel)
    pw = each(lambda t, lv: t * lv[0], a, lvls)
    t_inv = each(lambda dg, t: dg - t, diag, pw)
    pw_bd = [expand_heads(t) for t in pw]
    for _ in range(int(math.log2(GDN_BASE)) - 1):
        pw = each(lambda t, bd_: _dot(t.astype(BF16), bd_), pw, pw_bd)
        pw_bd = [expand_heads(t) for t in pw]
        t_inv = each(lambda t, bd_: t + _dot(t.astype(BF16), bd_), t_inv, pw_bd)
    a_tiled = [jnp.concatenate([t.astype(BF16)] * heads, axis=0) for t in a]
    for lvl in range(1, lvls[0].shape[0]):
        z_bd = each(lambda t, zm: t * zm[lvl - 1], a_tiled, zmasks)
        tz = each(lambda t, z_: _dot(t.astype(BF16), z_), t_inv, z_bd)
        t_inv = each(lambda t, tz_: t - _dot(tz_.astype(BF16), expand_heads(t)), t_inv, tz)
    e_gc = [jnp.exp(t) for t in gc]
    s = [r[...] for r in s_refs]
    sb = [t.astype(BF16) for t in s]
    ws = each(lambda kb_, e, q_, sb_: _dot(jnp.concatenate([kb_ * e, q_ * e], axis=0).astype(BF16), sb_),
              kb, e_gc, q, sb)
    rhs = each(lambda v_, b_, t: expand_heads(v_ * b_ - t[:c]), v, beta_l, ws)
    u = each(lambda t, r: _dot(t.astype(BF16), r), t_inv, rhs)
    o = each(lambda t, qk_, u_: t[c:] + _dot(qk_.astype(BF16), expand_heads(u_)), ws, qk, u)
    k_dec = each(lambda k_, gt, g: k_ * jnp.exp(gt - g), k, g_tot, gc)
    upd = each(lambda kd, u_: lax.dot_general(kd.astype(BF16), u_.astype(BF16), (((0,), (0,)), ((), ())),
                                              preferred_element_type=F32), k_dec, u)
    for r, s_, gt, up in zip(s_refs, s, g_tot, upd):
        r[...] = s_ * jnp.exp(gt) + up * bd_ref[...]
    return o


def _gdn_chunk_kernel(kvqf_ref, kvqb_ref, gbf_ref, gbb_ref, ef_ref, eb_ref, m64f_ref, m64b_ref,
                      i64f_ref, i64b_ref, lvlf_ref, lvlb_ref, zmf_ref, zmb_ref, bd_ref, bd16_ref,
                      of_ref, ob_ref, s_ref):
    @pl.when(pl.program_id(1) == 0)
    def _():
        s_ref[...] = jnp.zeros_like(s_ref)

    chains = []
    for bi in range(kvqf_ref.shape[0]):
        chains.append((kvqf_ref[bi], gbf_ref[bi], ef_ref, m64f_ref, i64f_ref, lvlf_ref, zmf_ref,
                       s_ref.at[bi, 0]))
        chains.append((kvqb_ref[bi], gbb_ref[bi], eb_ref, m64b_ref, i64b_ref, lvlb_ref, zmb_ref,
                       s_ref.at[bi, 1]))
    outs = _gdn_chains(chains, bd_ref, bd16_ref)
    for bi in range(kvqf_ref.shape[0]):
        of_ref[bi] = outs[2 * bi].astype(of_ref.dtype)
        ob_ref[bi] = outs[2 * bi + 1].astype(ob_ref.dtype)


def _gdn_chunk(kvq, gb, expand, masks, bd, bd16, l_lat, per_step):
    c = GDN_CHUNK
    nb, ttot, _ = kvq.shape
    n_lat = l_lat // c
    n_chunks = ttot // c
    fwd = lambda b, i: (b, (i + n_lat) % n_chunks, 0)
    bwd = lambda b, i: (b, n_chunks - 1 - i, 0)
    consts = [expand[0], expand[1], masks[0][0], masks[1][0], masks[0][1], masks[1][1],
              masks[0][2], masks[1][2], masks[0][3], masks[1][3], bd, bd16]
    return pl.pallas_call(
        _gdn_chunk_kernel,
        grid=(nb // per_step, n_chunks),
        in_specs=[pl.BlockSpec((per_step, c, 3 * W), fwd), pl.BlockSpec((per_step, c, 3 * W), bwd),
                  pl.BlockSpec((per_step, c, LANES), fwd), pl.BlockSpec((per_step, c, LANES), bwd)]
                 + [_resident(t.shape) for t in consts],
        out_specs=[pl.BlockSpec((per_step, c, W), fwd), pl.BlockSpec((per_step, c, W), bwd)],
        out_shape=[jax.ShapeDtypeStruct((nb, ttot, W), BF16)] * 2,
        scratch_shapes=[pltpu.VMEM((per_step, 2, W, W), F32)],
        compiler_params=_cparams("parallel", "arbitrary"),
        name="gdn_chunk",
    )(kvq, kvq, gb, gb, *consts)


def _merge_kernel(*refs, n_lat_tiles, split):
    x_ref, c_ref = (refs[0], refs[1]) if split else (refs[0], None)
    (h_ref, g1_ref, yf_ref, yb_ref, u5_ref, dskip_ref, gluw_ref, glub_ref, of_ref, ob_ref, z_ref,
     gain_ref, avg_ref, yfn_ref, ysg_ref, wgate_ref, wbr_ref, wout_ref, o_ref) = refs[2 if split else 1:]
    d = x_ref.shape[2]
    subs = range(x_ref.shape[0])
    y5 = [_gelu(yf_ref[s] + yb_ref[s] + dskip_ref[...] * u5_ref[s]) for s in subs]
    glu = [_dot(t.astype(BF16), gluw_ref[...]) for t in y5]
    o = [of_ref[s].astype(F32) + ob_ref[s].astype(F32) for s in subs]
    ms = [_dot_sel(t * t, avg_ref[...], 2) for t in o]
    y5 = [t * _sigmoid(g_ + glub_ref[...]) for t, g_ in zip(y5, glu)]
    yg = [o[s] * lax.rsqrt(ms[s] + EPS) * gain_ref[...] * _silu(z_ref[s].astype(F32)) for s in subs]
    ys = [(y5[s], yfn_ref[s], yg[s], ysg_ref[s]) for s in subs]
    acc = [None for _ in subs]
    for j in range(N_BRANCH):
        for s in subs:
            th = jnp.tanh(_dot(h_ref[s], wgate_ref[:, j * d:(j + 1) * d]))
            b_half = _dot(ys[s][j].astype(BF16), wbr_ref[j])
            t = th * b_half + b_half
            acc[s] = t if acc[s] is None else acc[s] + t
    out = [_dot(t.astype(BF16), wout_ref[...]) for t in acc]
    for s in subs:
        o_ref[s] = _stream_tile(x_ref, c_ref, s, n_lat_tiles) + g1_ref[s] * out[s]


def _merge(xs, mod, acts, consts, n_lat_tiles, n_tiles):
    nb, _, d = xs[0].shape
    h, yf, yb, u5, of, ob, z, yfn, ysg = acts
    dskip, gluw, glub, gain, avg, wgate, wbr, wout = consts
    return pl.pallas_call(
        functools.partial(_merge_kernel, n_lat_tiles=n_lat_tiles, split=len(xs) == 2),
        grid=(nb // PAIR, n_tiles),
        in_specs=_stream_specs(xs, d, PAIR, n_lat_tiles)
                 + [_tile_spec(d), _mod_spec(nb, d, 2, n_lat_tiles),
                  _tile_spec(W), _tile_spec(W), _tile_spec(W),
                  _resident(dskip.shape), _resident(gluw.shape), _resident(glub.shape),
                  _tile_spec(W), _tile_spec(W), _tile_spec(W),
                  _resident(gain.shape), _resident(avg.shape),
                  _tile_spec(W), _tile_spec(W), _resident(wgate.shape),
                  _resident(wbr.shape), _resident(wout.shape)],
        out_specs=_tile_spec(d),
        out_shape=jax.ShapeDtypeStruct((nb, n_tiles * ROW_TILE, d), F32),
        compiler_params=_cparams("parallel", "parallel"),
        name="merge",
    )(*xs, h, mod, yf, yb, u5, dskip, gluw, glub, of, ob, z, gain, avg, yfn, ysg, wgate, wbr, wout)


def _ffn_kernel(x_ref, sh_ref, sc_ref, g2_ref, gain_ref, w1_ref, w2_ref, nf_ref, o_ref, *, final):
    dff = w2_ref.shape[0]
    subs = range(x_ref.shape[0])
    h = [_modulated_norm(x_ref[s], gain_ref[...], sc_ref[s], sh_ref[s]).astype(BF16) for s in subs]
    t = [_dot(h_, w1_ref[...]) for h_ in h]
    act = [(_silu(t_[:, :dff]) * t_[:, dff:]).astype(BF16) for t_ in t]
    out = [_dot(a_, w2_ref[...]) for a_ in act]
    for s in subs:
        y = x_ref[s] + g2_ref[s] * out[s]
        if final:
            y = y * lax.rsqrt(jnp.mean(y * y, axis=-1, keepdims=True) + EPS) * nf_ref[...]
        o_ref[s] = y


def _ffn(x3, mod, gain, w1, w2, norm_f, n_lat_tiles, final):
    nb, ttot, d = x3.shape
    return pl.pallas_call(
        functools.partial(_ffn_kernel, final=final),
        grid=(nb // PAIR, ttot // ROW_TILE),
        in_specs=[_tile_spec(d), _mod_spec(nb, d, 3, n_lat_tiles), _mod_spec(nb, d, 4, n_lat_tiles),
                  _mod_spec(nb, d, 5, n_lat_tiles),
                  _resident(gain.shape), _resident(w1.shape), _resident(w2.shape),
                  _resident(norm_f.shape)],
        out_specs=_tile_spec(d),
        out_shape=jax.ShapeDtypeStruct(x3.shape, F32),
        compiler_params=_cparams("parallel", "parallel"),
        name="ffn",
    )(x3, mod, mod, mod, gain, w1, w2, norm_f)


def _dft_tables(n):
    idx = np.arange(n, dtype=np.int64)
    ang = 2.0 * np.pi * ((idx[:, None] * idx[None, :]) % n).astype(np.float64) / n
    return np.cos(ang), np.sin(ang)


def _s5_tables(lam_re, lam_im, log_dt, b_re, b_im, c_re, c_im):
    g, p, cg = S5_GROUPS, S5_STATE, S5_GROUP
    lam = lax.complex(lam_re.astype(F32), lam_im.astype(F32))
    a_bar = jnp.exp(lam * jnp.exp(log_dt.astype(F32))[:, None])
    b_bar = ((a_bar - 1.0) / lam)[..., None] * lax.complex(b_re.astype(F32), b_im.astype(F32))
    eye = jnp.eye(g, dtype=F32)
    bm_re = jnp.einsum('gpc,gh->gchp', b_bar.real, eye).reshape(g * cg, g * p)
    bm_im = jnp.einsum('gpc,gh->gchp', b_bar.imag, eye).reshape(g * cg, g * p)
    cm_re = jnp.einsum('gcp,gh->gphc', c_re.astype(F32), eye).reshape(g * p, g * cg)
    cm_im = jnp.einsum('gcp,gh->gphc', c_im.astype(F32), eye).reshape(g * p, g * cg)
    ng, h = N_S5 // S5_LANE_GROUP, S5_LANE_GROUP
    bmat = jnp.concatenate([bm_re.reshape(W, ng, h), bm_im.reshape(W, ng, h)], axis=2)
    bmat = jnp.transpose(bmat, (1, 0, 2)).astype(BF16)
    cmat = jnp.concatenate([cm_re.reshape(ng, h, W), -cm_im.reshape(ng, h, W)], axis=1).astype(BF16)
    avec = jnp.concatenate([a_bar.real.reshape(ng, 1, h), a_bar.imag.reshape(ng, 1, h)], axis=2)
    return bmat, cmat, avec


def _layer_consts(l, p):
    d = p['w_in'].shape[1]
    w_in = p['w_in'][l]
    sizes = (W, W, W, 2 * GDN_HEADS, 2 * GDN_HEADS, W, W, W, W, W, N_BRANCH * d)
    offs = np.concatenate([[0], np.cumsum(sizes)])
    col = lambda j: w_in[:, offs[j]:offs[j + 1]]
    u5, k, v, a, bt, q, z, ufn, usg, vsg, gate = [col(j) for j in range(len(sizes))]
    cc, sc = _dft_tables(FN_GROUP)
    fw = p['fn_w'][l].astype(F32)
    wc = jnp.einsum('cd,gde->gce', jnp.asarray(cc, F32), fw, precision=HIGHEST)
    ws = jnp.einsum('cd,gde->gce', jnp.asarray(sc, F32), fw, precision=HIGHEST)
    ufn_g = ufn.reshape(d, FN_GROUPS, FN_GROUP)
    fold_c = jnp.einsum('kgc,gce->kge', ufn_g, wc, precision=HIGHEST).reshape(d, W)
    fold_s = jnp.einsum('kgc,gce->kge', ufn_g, ws, precision=HIGHEST).reshape(d, W)
    w_ab = jnp.concatenate([a, bt, jnp.zeros((d, LANES - 4 * GDN_HEADS), F32)], axis=1)
    ws_in = [u5, jnp.concatenate([k, v, q], axis=1), w_ab, z,
             jnp.concatenate([fold_c, fold_s], axis=1), jnp.concatenate([usg, vsg], axis=1)]
    ws_in = [w.astype(BF16) for w in ws_in]
    w_gate = (0.5 * gate).astype(BF16)

    s5 = [_s5_tables(p['s5_lam_re'][l, dr], p['s5_lam_im'][l, dr], p['s5_log_dt'][l, dr],
                     p['s5_b_re'][l, dr], p['s5_b_im'][l, dr], p['s5_c_re'][l, dr],
                     p['s5_c_im'][l, dr]) for dr in range(2)]

    conv = p['gdn_conv'][l].astype(F32)
    cw = jnp.transpose(conv, (2, 0, 1)).reshape(conv.shape[2], 3 * W)
    pad = jnp.zeros((1, LANES - 2 * GDN_HEADS), F32)
    alog = jnp.concatenate([p['gdn_a_log'][l].astype(F32).reshape(1, -1), pad], axis=1)
    dtb = jnp.concatenate([p['gdn_dt_bias'][l].astype(F32).reshape(1, -1), pad], axis=1)

    sg_bias = jnp.repeat(p['sg_b'][l].astype(F32).T, SG_GROUP, axis=1)
    merge_consts = (p['s5_d'][l].astype(F32).reshape(1, W), p['s5_glu_w'][l].astype(BF16),
                    p['s5_glu_b'][l].astype(F32).reshape(1, W),
                    jnp.tile(p['gdn_norm'][l].astype(F32), GDN_HEADS).reshape(1, W))
    return dict(
        ws_in=ws_in, w_gate=w_gate, s5=s5, cw=cw, alog=alog, dtb=dtb,
        sg_w=p['sg_w'][l].astype(BF16), sg_bias=sg_bias,
        sg_lng=p['sg_ln_g'][l].astype(F32).reshape(1, W), sg_lnb=p['sg_ln_b'][l].astype(F32).reshape(1, W),
        merge=merge_consts, wbr_half=(0.5 * p['w_branch'][l]).astype(BF16),
        wout=p['w_out'][l].astype(BF16),
        w1=p['ffn_w1'][l].astype(BF16), w2=p['ffn_w2'][l].astype(BF16),
        norm1=p['norm1'][l].astype(F32).reshape(1, d), norm2=p['norm2'][l].astype(F32).reshape(1, d))


def kernel(x, c, ctx, c_ctx, ada_w, ada_b, norm1, norm2, w_in, s5_lam_re, s5_lam_im, s5_log_dt, s5_b_re, s5_b_im, s5_c_re, s5_c_im, s5_d, s5_glu_w, s5_glu_b, fn_w, gdn_conv, gdn_a_log, gdn_dt_bias, gdn_norm, sg_ln_g, sg_ln_b, sg_w, sg_b, w_branch, w_out, ffn_w1, ffn_w2, norm_f):
    p = dict(w_in=w_in, s5_lam_re=s5_lam_re, s5_lam_im=s5_lam_im, s5_log_dt=s5_log_dt,
             s5_b_re=s5_b_re, s5_b_im=s5_b_im, s5_c_re=s5_c_re, s5_c_im=s5_c_im, s5_d=s5_d,
             s5_glu_w=s5_glu_w, s5_glu_b=s5_glu_b, fn_w=fn_w, gdn_conv=gdn_conv,
             gdn_a_log=gdn_a_log, gdn_dt_bias=gdn_dt_bias, gdn_norm=gdn_norm, sg_ln_g=sg_ln_g,
             sg_ln_b=sg_ln_b, sg_w=sg_w, sg_b=sg_b, w_branch=w_branch, w_out=w_out,
             ffn_w1=ffn_w1, ffn_w2=ffn_w2, norm1=norm1, norm2=norm2)
    nb, l_lat, d = x.shape
    l_ctx = ctx.shape[1]
    depth = ada_w.shape[0]
    ttot = l_lat + l_ctx
    assert nb % SUBLANES == 0 and nb % PAIR == 0 and nb % KIN_GROUP == 0
    assert d == N_BRANCH * W and l_lat % l_ctx == 0
    assert l_lat % ROW_TILE == 0 and l_ctx % ROW_TILE == 0 and ROW_TILE % SG_CHUNK == 0
    n_lat_tiles = l_lat // ROW_TILE
    n_tiles = ttot // ROW_TILE
    sg_rows = max(r for r in range(SG_CHUNK, 7 * SG_CHUNK, SG_CHUNK) if ttot % r == 0)
    gdn_group = math.gcd(nb, GDN_GROUP)

    if l_ctx == ROW_TILE:
        xs = (x.astype(F32), ctx.astype(F32))
    else:
        xs = (jnp.concatenate([x, ctx], axis=1).astype(F32),)

    n_cond = nb + max(PAIR, KIN_GROUP)
    ada_rows = -(-n_cond // SUBLANES) * SUBLANES
    cc = jnp.zeros((ada_rows, d), F32).at[:nb].set(c.astype(F32)).at[nb:n_cond].set(c_ctx.astype(F32))
    mods = _ada(cc, ada_w.astype(F32), ada_b.astype(F32))[:, :n_cond].reshape(depth, n_cond, 1, 6 * d)

    cos_l, sin_l = _dft_tables(l_lat)
    cos_c, sin_c = _dft_tables(l_ctx)
    sc_l = 1.0 / math.sqrt(l_lat * FN_GROUP)
    sc_c = 1.0 / math.sqrt(l_ctx * FN_GROUP)
    cos_l, nsin_l = jnp.asarray(cos_l * sc_l, BF16), jnp.asarray(-sin_l * sc_l, BF16)
    cos_c, nsin_c = jnp.asarray(cos_c * sc_c, BF16), jnp.asarray(-sin_c * sc_c, BF16)
    nf = norm_f.astype(F32).reshape(1, d)

    lane_head = np.arange(W) // GDN_HEAD_DIM
    seg = (lane_head[:, None] == lane_head[None, :]).astype(np.float32)
    seg_ones = jnp.asarray(seg, BF16)
    seg_avg = jnp.asarray(seg / GDN_HEAD_DIM, BF16)
    expand = []
    for dr in range(2):
        e = np.zeros((LANES, 2 * W), np.float32)
        for h in range(GDN_HEADS):
            e[dr * GDN_HEADS + h, :W] = lane_head == h
            e[2 * GDN_HEADS + dr * GDN_HEADS + h, W:] = lane_head == h
        expand.append(jnp.asarray(e, BF16))
    gdn_masks = [_gdn_masks(False), _gdn_masks(True)]
    bd = jnp.asarray(seg, F32)
    bd16 = jnp.asarray(np.concatenate([seg, seg], axis=1), BF16)
    src = (np.arange(nb)[None, :] * SUBLANES + np.arange(SUBLANES)[:, None]).reshape(-1)
    perm_np = np.zeros((nb * SUBLANES, nb * SUBLANES), np.float32)
    perm_np[np.arange(nb * SUBLANES), src] = 1.0
    perm, permt = jnp.asarray(perm_np, BF16), jnp.asarray(perm_np.T, BF16)

    for l in range(depth):
        k = _layer_consts(l, p)
        mod = mods[l]
        last = l == depth - 1
        u5, kvq_p, gb, z, afn, sg, h = _kin(xs, mod, k['norm1'], k['ws_in'],
                                            (k['cw'], seg_ones, k['alog'], k['dtb']), n_lat_tiles, n_tiles)
        yf, yb = _s5(u5, perm, permt, [t[0] for t in k['s5']], [t[1] for t in k['s5']],
                     [t[2] for t in k['s5']], l_lat // S5_STEPS)
        yfn = _fnet(afn, cos_l, nsin_l, cos_c, nsin_c)
        ysg = _sgu(sg, k['sg_w'], k['sg_bias'], k['sg_lng'], k['sg_lnb'], seg_avg, sg_rows)
        of, ob = _gdn_chunk(kvq_p, gb, expand, gdn_masks, bd, bd16, l_lat, gdn_group)
        acts = (h, yf, yb, u5, of, ob, z, yfn, ysg)
        consts = k['merge'] + (seg_avg, k['w_gate'], k['wbr_half'], k['wout'])
        x3 = _merge(xs, mod, acts, consts, n_lat_tiles, n_lat_tiles if last else n_tiles)
        xs = (_ffn(x3, mod, k['norm2'], k['w1'], k['w2'], nf, n_lat_tiles, final=last),)

    return xs[0].astype(x.dtype)
```

```python
import functools
import math

import numpy as np
import jax
import jax.numpy as jnp
from jax import lax
from jax.experimental import pallas as pl
from jax.experimental.pallas import tpu as pltpu

F32 = jnp.float32
BF16 = jnp.bfloat16
HIGHEST = lax.Precision.HIGHEST

EPS = 1e-6
W = 256
N_BRANCH = 4
S5_GROUP = 16
S5_GROUPS = W // S5_GROUP
S5_STATE = 64
N_S5 = S5_GROUPS * S5_STATE
FN_GROUPS = 4
FN_GROUP = W // FN_GROUPS
GDN_HEADS = 4
GDN_HEAD_DIM = W // GDN_HEADS
GDN_CHUNK = 64
GDN_BASE = 8
GDN_GROUP = 16
SG_GROUPS = 4
SG_GROUP = W // SG_GROUPS
SG_CHUNK = 128
LANES = 128
SUBLANES = 8
ROW_TILE = 256
PAIR = 2
KIN_GROUP = 4
S5_STEPS = 32
S5_LANE_GROUP = 256
VMEM_LIMIT = 56 * 1024 * 1024


def _cparams(*sem):
    return pltpu.CompilerParams(dimension_semantics=sem, vmem_limit_bytes=VMEM_LIMIT)


def _resident(shape):
    nd = len(shape)
    return pl.BlockSpec(shape, lambda *_: (0,) * nd, pipeline_mode=pl.Buffered(1))


def _dot(a, b):
    return jnp.dot(a, b, preferred_element_type=F32)


def _split(x, parts):
    out = []
    for _ in range(parts - 1):
        hi = x.astype(BF16)
        out.append(hi)
        x = x - hi.astype(F32)
    out.append(x.astype(BF16))
    return out


def _dot_sel(x, sel, parts):
    acc = None
    for piece in _split(x, parts):
        t = _dot(piece, sel)
        acc = t if acc is None else acc + t
    return acc


def _gelu(x):
    return 0.5 * x * (1.0 + jnp.tanh(math.sqrt(2.0 / math.pi) * (x + 0.044715 * (x * x * x))))


def _sigmoid(x):
    return 0.5 * jnp.tanh(0.5 * x) + 0.5


def _silu(x):
    return x * _sigmoid(x)


def _modulated_norm(x, gain, scale, shift):
    y = x * lax.rsqrt(jnp.mean(x * x, axis=-1, keepdims=True) + EPS) * gain
    return y * (1.0 + scale) + shift


def _ada_kernel(c_ref, w_ref, b_ref, o_ref):
    c = c_ref[...]
    o_ref[0] = jnp.dot(_silu(c), w_ref[0], preferred_element_type=F32, precision=HIGHEST) + b_ref[0]


def _ada(cc, ada_w, ada_b):
    depth, d, n = ada_w.shape
    rows = cc.shape[0]
    return pl.pallas_call(
        _ada_kernel,
        grid=(depth, n // d),
        in_specs=[pl.BlockSpec((rows, d), lambda l, j: (0, 0)),
                  pl.BlockSpec((1, d, d), lambda l, j: (l, 0, j)),
                  pl.BlockSpec((1, 1, d), lambda l, j: (l, 0, j))],
        out_specs=pl.BlockSpec((1, rows, d), lambda l, j: (l, 0, j)),
        out_shape=jax.ShapeDtypeStruct((depth, rows, n), F32),
        compiler_params=_cparams("parallel", "parallel"),
        name="ada",
    )(cc, ada_w, ada_b.reshape(depth, 1, n))


def _tile_spec(n, group=PAIR):
    return pl.BlockSpec((group, ROW_TILE, n), lambda b, j: (b, j, 0))


def _mod_spec(nb, d, k, n_lat_tiles, group=PAIR):
    return pl.BlockSpec((group, 1, d), lambda b, j: (jnp.where(j < n_lat_tiles, b, nb // group), 0, k))


def _gdn_features(kvq, before, after, cw_ref, ones_ref):
    rows, n = kvq.shape
    row = lax.broadcasted_iota(jnp.int32, (rows, n), 0)
    xp = jnp.where(row == 0, before, pltpu.roll(kvq, 1, 0))
    xn = jnp.where(row == rows - 1, after, pltpu.roll(kvq, rows - 1, 0))
    y = _silu(xp * cw_ref[0:1, :] + kvq * cw_ref[1:2, :] + xn * cw_ref[2:3, :])
    k = y[:, :W]
    q = y[:, 2 * W:]
    k = k * lax.rsqrt(_dot_sel(k * k, ones_ref[...], 1) + EPS)
    q = q * lax.rsqrt(_dot_sel(q * q, ones_ref[...], 1) + EPS) * GDN_HEAD_DIM ** -0.5
    return k, y[:, W:2 * W], q


def _gdn_gates(ab, alog_ref, dtb_ref):
    z = ab + dtb_ref[...]
    softplus = jnp.maximum(z, 0.0) + jnp.log(1.0 + jnp.exp(-jnp.abs(z)))
    g = -jnp.exp(alog_ref[...]) * softplus
    lane = lax.broadcasted_iota(jnp.int32, ab.shape, 1)
    return jnp.where(lane < 2 * GDN_HEADS, g, _sigmoid(ab))


def _stream_specs(xs, d, group, n_lat_tiles):
    if len(xs) == 1:
        return [_tile_spec(d, group)]
    return [pl.BlockSpec((group, ROW_TILE, d), lambda b, j: (b, jnp.minimum(j, n_lat_tiles - 1), 0)),
            pl.BlockSpec((group, ROW_TILE, d), lambda b, j: (b, jnp.maximum(j - n_lat_tiles, 0), 0))]


def _stream_tile(x_ref, c_ref, s, n_lat_tiles):
    if c_ref is None:
        return x_ref[s]
    return jnp.where(pl.program_id(1) >= n_lat_tiles, c_ref[s], x_ref[s])


def _kin_kernel(*refs, n_lat_tiles, n_tiles, split):
    x_ref, c_ref = (refs[0], refs[1]) if split else (refs[0], None)
    (xp_ref, xn_ref, sh_ref, sc_ref, g_ref, w5, wkvq, wab, wz, wfn, wsg, cw_ref, ones_ref, alog_ref,
     dtb_ref, o5, okvq, ogb, oz, ofn, osg, oh) = refs[2 if split else 1:]
    j = pl.program_id(1)
    subs = range(x_ref.shape[0])
    halo = xp_ref.shape[1]
    hb = [_modulated_norm(_stream_tile(x_ref, c_ref, s, n_lat_tiles), g_ref[...], sc_ref[s],
                          sh_ref[s]).astype(BF16) for s in subs]
    edge = [jnp.concatenate([xp_ref[s], xn_ref[s]], axis=0) for s in subs]
    eb = [_modulated_norm(edge[s], g_ref[...], sc_ref[s], sh_ref[s]).astype(BF16) for s in subs]
    has_prev = jnp.logical_and(j != 0, j != n_lat_tiles).astype(F32)
    has_next = jnp.logical_and(j != n_lat_tiles - 1, j != n_tiles - 1).astype(F32)
    for s in subs:
        oh[s] = hb[s]
    for w_ref, o_ref in ((w5, o5), (wz, oz), (wfn, ofn), (wsg, osg)):
        for s in subs:
            o_ref[s] = _dot(hb[s], w_ref[...]).astype(o_ref.dtype)
    rows = x_ref.shape[1]
    kvq = [_dot(jnp.concatenate([hb[s], eb[s]], axis=0), wkvq[...]) for s in subs]
    for s in subs:
        k, v, q = _gdn_features(kvq[s][:rows], kvq[s][rows + halo - 1:rows + halo, :] * has_prev,
                                kvq[s][rows + halo:rows + halo + 1, :] * has_next, cw_ref, ones_ref)
        okvq[s, :, :W] = k.astype(okvq.dtype)
        okvq[s, :, W:2 * W] = v.astype(okvq.dtype)
        okvq[s, :, 2 * W:] = q.astype(okvq.dtype)
    for s in subs:
        ogb[s] = _gdn_gates(_dot(hb[s], wab[...]), alog_ref, dtb_ref)


def _kin(xs, mod, gain, ws, gdn_consts, n_lat_tiles, n_tiles):
    nb, _, d = xs[0].shape
    ttot = n_tiles * ROW_TILE
    widths = [W, 3 * W, LANES, W, 2 * W, 2 * W, d]
    dts = [F32, BF16, F32, BF16, BF16, BF16, BF16]
    per = ROW_TILE // SUBLANES
    last = xs[0].shape[1] // SUBLANES - 1
    grp = KIN_GROUP
    return pl.pallas_call(
        functools.partial(_kin_kernel, n_lat_tiles=n_lat_tiles, n_tiles=n_tiles, split=len(xs) == 2),
        grid=(nb // grp, n_tiles),
        in_specs=_stream_specs(xs, d, grp, n_lat_tiles)
                 + [pl.BlockSpec((grp, SUBLANES, d), lambda b, j: (b, jnp.clip(j * per - 1, 0, last), 0)),
                    pl.BlockSpec((grp, SUBLANES, d), lambda b, j: (b, jnp.clip((j + 1) * per, 0, last), 0)),
                    _mod_spec(nb, d, 0, n_lat_tiles, grp), _mod_spec(nb, d, 1, n_lat_tiles, grp),
                    _resident((1, d))] + [_resident(w.shape) for w in ws]
                 + [_resident(t.shape) for t in gdn_consts],
        out_specs=[_tile_spec(n, grp) for n in widths],
        out_shape=[jax.ShapeDtypeStruct((nb, ttot, n), dt) for n, dt in zip(widths, dts)],
        compiler_params=_cparams("parallel", "parallel"),
        name="kin",
    )(*xs, xs[0], xs[0], mod, mod, gain, *ws, *gdn_consts)


def _s5_kernel(uf_ref, ub_ref, perm_ref, permt_ref, bmf, bmb, cmf, cmb, af_ref, ab_ref,
               yf_ref, yb_ref, st_ref, bu_ref, s_ref):
    i = pl.program_id(0)
    nb, tt, _ = uf_ref.shape
    sub = SUBLANES
    n_grp, _, gw = bmf.shape
    half = gw // 2

    @pl.when(i == 0)
    def _():
        st_ref[...] = jnp.zeros_like(st_ref)

    dirs = ((uf_ref, bmf, cmf, af_ref, yf_ref, range(tt)),
            (ub_ref, bmb, cmb, ab_ref, yb_ref, range(tt - 1, -1, -1)))

    def regroup(dr):
        u_ref = dirs[dr][0]
        parts = []
        for k in range(tt // sub):
            blk = u_ref[:, k * sub:(k + 1) * sub, :].reshape(nb * sub, W).astype(BF16)
            parts.append(_dot(perm_ref[...], blk).astype(BF16))
        return jnp.concatenate(parts, axis=0)

    u_tb = [regroup(0), regroup(1)]
    y_tb = [None, None]

    def project(dr, g):
        bu_ref[dr, g] = _dot(u_tb[dr], dirs[dr][1][g])

    def recur(dr, g):
        a_ref, order = dirs[dr][3], dirs[dr][5]
        for c in range(half // LANES):
            re = slice(c * LANES, (c + 1) * LANES)
            im = slice(half + c * LANES, half + (c + 1) * LANES)
            a_re = a_ref[g, :, re]
            a_im = a_ref[g, :, im]
            s_re = st_ref[dr, g, :, re]
            s_im = st_ref[dr, g, :, im]
            for t in order:
                r = slice(t * nb, (t + 1) * nb)
                n_re = a_re * s_re - a_im * s_im + bu_ref[dr, g, r, re]
                n_im = a_re * s_im + a_im * s_re + bu_ref[dr, g, r, im]
                s_re, s_im = n_re, n_im
                s_ref[dr, g, r, re] = s_re.astype(BF16)
                s_ref[dr, g, r, im] = s_im.astype(BF16)
            st_ref[dr, g, :, re] = s_re
            st_ref[dr, g, :, im] = s_im

    def read_out(dr, g):
        t = _dot(s_ref[dr, g], dirs[dr][2][g])
        y_tb[dr] = t if y_tb[dr] is None else y_tb[dr] + t

    for dr in range(2):
        project(dr, 0)
    for g in range(n_grp):
        for dr in range(2):
            if g + 1 < n_grp:
                project(dr, g + 1)
            recur(dr, g)
        for dr in range(2):
            read_out(dr, g)
    for dr in range(2):
        y_ref = dirs[dr][4]
        for k in range(tt // sub):
            y_bt = _dot(permt_ref[...], y_tb[dr][k * sub * nb:(k + 1) * sub * nb].astype(BF16))
            y_ref[:, k * sub:(k + 1) * sub, :] = y_bt.reshape(nb, sub, W)


def _s5(u5, perm, permt, bmats, cmats, avecs, n_lat_tiles):
    nb, ttot, _ = u5.shape
    tt = S5_STEPS
    n_tiles = ttot // tt
    fwd = lambda i: (0, (i + n_lat_tiles) % n_tiles, 0)
    bwd = lambda i: (0, n_tiles - 1 - i, 0)
    rows = tt * nb
    n_grp, _, gw = bmats[0].shape
    return pl.pallas_call(
        _s5_kernel,
        grid=(n_tiles,),
        in_specs=[pl.BlockSpec((nb, tt, W), fwd), pl.BlockSpec((nb, tt, W), bwd),
                  _resident(perm.shape), _resident(permt.shape),
                  _resident(bmats[0].shape), _resident(bmats[1].shape),
                  _resident(cmats[0].shape), _resident(cmats[1].shape),
                  _resident(avecs[0].shape), _resident(avecs[1].shape)],
        out_specs=[pl.BlockSpec((nb, tt, W), fwd), pl.BlockSpec((nb, tt, W), bwd)],
        out_shape=[jax.ShapeDtypeStruct((nb, ttot, W), F32)] * 2,
        scratch_shapes=[pltpu.VMEM((2, n_grp, nb, gw), F32),
                        pltpu.VMEM((2, n_grp, rows, gw), F32),
                        pltpu.VMEM((2, n_grp, rows, gw), BF16)],
        compiler_params=_cparams("arbitrary"),
        name="s5",
    )(u5, u5, perm, permt, bmats[0], bmats[1], cmats[0], cmats[1], avecs[0], avecs[1])


def _fnet_kernel(a_ref, cosl_ref, nsinl_ref, cosc_ref, nsinc_ref, y_ref):
    l_lat = cosl_ref.shape[0]
    for rows, cos_ref, nsin_ref in ((slice(0, l_lat), cosl_ref, nsinl_ref),
                                    (slice(l_lat, y_ref.shape[0]), cosc_ref, nsinc_ref)):
        y = _dot(cos_ref[...], a_ref[rows, :W]) + _dot(nsin_ref[...], a_ref[rows, W:])
        y_ref[rows, :] = y.astype(y_ref.dtype)


def _fnet(afn, cos_l, nsin_l, cos_c, nsin_c):
    nb, ttot, _ = afn.shape
    return pl.pallas_call(
        _fnet_kernel,
        grid=(nb,),
        in_specs=[pl.BlockSpec((None, ttot, 2 * W), lambda b: (b, 0, 0)),
                  _resident(cos_l.shape), _resident(nsin_l.shape),
                  _resident(cos_c.shape), _resident(nsin_c.shape)],
        out_specs=pl.BlockSpec((None, ttot, W), lambda b: (b, 0, 0)),
        out_shape=jax.ShapeDtypeStruct((nb, ttot, W), BF16),
        compiler_params=_cparams("parallel"),
        name="fnet",
    )(afn, cos_l, nsin_l, cos_c, nsin_c)


def _sgu_kernel(uv_ref, w_ref, bias_ref, lng_ref, lnb_ref, avg_ref, y_ref):
    lane_group = lax.broadcasted_iota(jnp.int32, (SG_CHUNK, W), 1) // SG_GROUP
    rows = [slice(n * SG_CHUNK, (n + 1) * SG_CHUNK) for n in range(uv_ref.shape[0] // SG_CHUNK)]
    v = [_gelu(uv_ref[r, W:].astype(F32)) for r in rows]
    dv = [t - _dot_sel(t, avg_ref[...], 2) for t in v]
    var = [_dot_sel(t * t, avg_ref[...], 1) for t in dv]
    vn = [(d_ * lax.rsqrt(s_ + EPS) * lng_ref[...] + lnb_ref[...]).astype(BF16) for d_, s_ in zip(dv, var)]
    sv = [_dot(w_ref[0], t) for t in vn]
    for g in range(1, SG_GROUPS):
        sv = [jnp.where(lane_group == g, _dot(w_ref[g], t), s_) for t, s_ in zip(vn, sv)]
    for r, s_ in zip(rows, sv):
        y_ref[r, :] = (_gelu(uv_ref[r, :W].astype(F32)) * (s_ + bias_ref[...])).astype(y_ref.dtype)


def _sgu(sg, w, bias, lng, lnb, avg, rows):
    nb, ttot, _ = sg.shape
    return pl.pallas_call(
        _sgu_kernel,
        grid=(nb, ttot // rows),
        in_specs=[pl.BlockSpec((None, rows, 2 * W), lambda b, n: (b, n, 0)),
                  _resident(w.shape), _resident(bias.shape), _resident(lng.shape),
                  _resident(lnb.shape), _resident(avg.shape)],
        out_specs=pl.BlockSpec((None, rows, W), lambda b, n: (b, n, 0)),
        out_shape=jax.ShapeDtypeStruct((nb, ttot, W), BF16),
        compiler_params=_cparams("parallel", "parallel"),
        name="sgu",
    )(sg, w, bias, lng, lnb, avg)


def _gdn_masks(backward):
    c, heads = GDN_CHUNK, GDN_HEADS
    i = np.arange(c)[:, None]
    j = (np.arange(W) % c)[None, :]
    incl = (i <= j) if backward else (i >= j)
    strict = (i < j) if backward else (i > j)
    j64 = np.arange(c)[None, :]
    incl64 = (i <= j64) if backward else (i >= j64)
    same = lambda n: (i // n) == (j // n)
    levels = [same(GDN_BASE)]
    n = GDN_BASE
    while n < c:
        levels.append(same(2 * n) & ~same(n))
        n *= 2
    f = lambda m: jnp.asarray(m, F32)
    same_head = (np.arange(heads * c)[:, None] // c) == (np.arange(W)[None, :] // c)
    z_masks = np.stack([np.tile(m, (heads, 1)) & same_head for m in levels[1:]])
    return (f(np.stack([incl, strict, i == j])), jnp.asarray(incl64, BF16), f(np.stack(levels)),
            jnp.asarray(z_masks, BF16))


def _gdn_chains(chains, bd_ref, bd16_ref):
    c, heads = GDN_CHUNK, GDN_HEADS
    each = lambda f, *cols: [f(*args) for args in zip(*cols)]

    def expand_heads(t):
        t = t.astype(BF16)
        return jnp.concatenate([t] * heads, axis=0) * bd16_ref[:, :t.shape[1]]

    kvq, gb, expand, m64, incl64, lvls, zmasks, s_refs = zip(*chains)
    kvq = [t.astype(F32) for t in kvq]
    k = [t[:, :W] for t in kvq]
    v = [t[:, W:2 * W] for t in kvq]
    q = [t[:, 2 * W:] for t in kvq]
    def expand_gates(g, e):
        hi, lo = _split(g, 2)
        t = _dot(jnp.concatenate([hi, lo], axis=0), e[...])
        return t[:c] + t[c:]

    ge = each(expand_gates, gb, expand)
    g_l = [t[:, :W] for t in ge]
    beta_l = [t[:, W:] for t in ge]
    incl = [m[0] for m in m64]
    strict = [m[1] for m in m64]
    diag = [m[2] for m in m64]

    def cumulative(g, i64):
        acc = None
        for piece in _split(g, 2):
            t = _dot(i64[...], piece)
            acc = t if acc is None else acc + t
        return acc

    gc = each(cumulative, g_l, incl64)
    dif = each(lambda g, dg: g - jnp.sum(g * dg, axis=0, keepdims=True), gc, diag)
    g_tot = [jnp.sum(t, axis=0, keepdims=True) for t in g_l]
    kb = each(lambda a_, b_: a_ * b_, k, beta_l)
    k_st = [expand_heads(t) for t in k]
    kq = each(lambda a_, b_, st: lax.dot_general(
        jnp.concatenate([a_, b_], axis=0).astype(BF16), st, (((1,), (1,)), ((), ())),
        preferred_element_type=F32), kb, q, k_st)
    rel = each(lambda d_, i: jnp.exp(d_ * i) * i, dif, incl)
    a = each(lambda t, r, st: t[:c] * r * st, kq, rel, strict)
    qk = each(lambda t, r: t[c:] * r, kq, rel)
    pw = each(lambda t, lv: t * lv[0], a, lvls)
    t_inv = each(lambda dg, t: dg - t, diag, pw)
    pw_bd = [expand_heads(t) for t in pw]
    for _ in range(int(math.log2(GDN_BASE)) - 1):
        pw = each(lambda t, bd_: _dot(t.astype(BF16), bd_), pw, pw_bd)
        pw_bd = [expand_heads(t) for t in pw]
        t_inv = each(lambda t, bd_: t + _dot(t.astype(BF16), bd_), t_inv, pw_bd)
    a_tiled = [jnp.concatenate([t.astype(BF16)] * heads, axis=0) for t in a]
    for lvl in range(1, lvls[0].shape[0]):
        z_bd = each(lambda t, zm: t * zm[lvl - 1], a_tiled, zmasks)
        tz = each(lambda t, z_: _dot(t.astype(BF16), z_), t_inv, z_bd)
        t_inv = each(lambda t, tz_: t - _dot(tz_.astype(BF16), expand_heads(t)), t_inv, tz)
    e_gc = [jnp.exp(t) for t in gc]
    s = [r[...] for r in s_refs]
    sb = [t.astype(BF16) for t in s]
    ws = each(lambda kb_, e, q_, sb_: _dot(jnp.concatenate([kb_ * e, q_ * e], axis=0).astype(BF16), sb_),
              kb, e_gc, q, sb)
    rhs = each(lambda v_, b_, t: expand_heads(v_ * b_ - t[:c]), v, beta_l, ws)
    u = each(lambda t, r: _dot(t.astype(BF16), r), t_inv, rhs)
    o = each(lambda t, qk_, u_: t[c:] + _dot(qk_.astype(BF16), expand_heads(u_)), ws, qk, u)
    k_dec = each(lambda k_, gt, g: k_ * jnp.exp(gt - g), k, g_tot, gc)
    upd = each(lambda kd, u_: lax.dot_general(kd.astype(BF16), u_.astype(BF16), (((0,), (0,)), ((), ())),
                                              preferred_element_type=F32), k_dec, u)
    for r, s_, gt, up in zip(s_refs, s, g_tot, upd):
        r[...] = s_ * jnp.exp(gt) + up * bd_ref[...]
    return o


def _gdn_chunk_kernel(kvqf_ref, kvqb_ref, gbf_ref, gbb_ref, ef_ref, eb_ref, m64f_ref, m64b_ref,
                      i64f_ref, i64b_ref, lvlf_ref, lvlb_ref, zmf_ref, zmb_ref, bd_ref, bd16_ref,
                      of_ref, ob_ref, s_ref):
    @pl.when(pl.program_id(1) == 0)
    def _():
        s_ref[...] = jnp.zeros_like(s_ref)

    chains = []
    for bi in range(kvqf_ref.shape[0]):
        chains.append((kvqf_ref[bi], gbf_ref[bi], ef_ref, m64f_ref, i64f_ref, lvlf_ref, zmf_ref,
                       s_ref.at[bi, 0]))
        chains.append((kvqb_ref[bi], gbb_ref[bi], eb_ref, m64b_ref, i64b_ref, lvlb_ref, zmb_ref,
                       s_ref.at[bi, 1]))
    outs = _gdn_chains(chains, bd_ref, bd16_ref)
    for bi in range(kvqf_ref.shape[0]):
        of_ref[bi] = outs[2 * bi].astype(of_ref.dtype)
        ob_ref[bi] = outs[2 * bi + 1].astype(ob_ref.dtype)


def _gdn_chunk(kvq, gb, expand, masks, bd, bd16, l_lat, per_step):
    c = GDN_CHUNK
    nb, ttot, _ = kvq.shape
    n_lat = l_lat // c
    n_chunks = ttot // c
    fwd = lambda b, i: (b, (i + n_lat) % n_chunks, 0)
    bwd = lambda b, i: (b, n_chunks - 1 - i, 0)
    consts = [expand[0], expand[1], masks[0][0], masks[1][0], masks[0][1], masks[1][1],
              masks[0][2], masks[1][2], masks[0][3], masks[1][3], bd, bd16]
    return pl.pallas_call(
        _gdn_chunk_kernel,
        grid=(nb // per_step, n_chunks),
        in_specs=[pl.BlockSpec((per_step, c, 3 * W), fwd), pl.BlockSpec((per_step, c, 3 * W), bwd),
                  pl.BlockSpec((per_step, c, LANES), fwd), pl.BlockSpec((per_step, c, LANES), bwd)]
                 + [_resident(t.shape) for t in consts],
        out_specs=[pl.BlockSpec((per_step, c, W), fwd), pl.BlockSpec((per_step, c, W), bwd)],
        out_shape=[jax.ShapeDtypeStruct((nb, ttot, W), BF16)] * 2,
        scratch_shapes=[pltpu.VMEM((per_step, 2, W, W), F32)],
        compiler_params=_cparams("parallel", "arbitrary"),
        name="gdn_chunk",
    )(kvq, kvq, gb, gb, *consts)


def _merge_kernel(*refs, n_lat_tiles, split):
    x_ref, c_ref = (refs[0], refs[1]) if split else (refs[0], None)
    (h_ref, g1_ref, yf_ref, yb_ref, u5_ref, dskip_ref, gluw_ref, glub_ref, of_ref, ob_ref, z_ref,
     gain_ref, avg_ref, yfn_ref, ysg_ref, wgate_ref, wbr_ref, wout_ref, o_ref) = refs[2 if split else 1:]
    d = x_ref.shape[2]
    subs = range(x_ref.shape[0])
    y5 = [_gelu(yf_ref[s] + yb_ref[s] + dskip_ref[...] * u5_ref[s]) for s in subs]
    glu = [_dot(t.astype(BF16), gluw_ref[...]) for t in y5]
    o = [of_ref[s].astype(F32) + ob_ref[s].astype(F32) for s in subs]
    ms = [_dot_sel(t * t, avg_ref[...], 1) for t in o]
    y5 = [t * _sigmoid(g_ + glub_ref[...]) for t, g_ in zip(y5, glu)]
    yg = [o[s] * lax.rsqrt(ms[s] + EPS) * gain_ref[...] * _silu(z_ref[s].astype(F32)) for s in subs]
    ys = [(y5[s], yfn_ref[s], yg[s], ysg_ref[s]) for s in subs]
    acc = [None for _ in subs]
    for j in range(N_BRANCH):
        for s in subs:
            th = jnp.tanh(_dot(h_ref[s], wgate_ref[:, j * d:(j + 1) * d]))
            b_half = _dot(ys[s][j].astype(BF16), wbr_ref[j])
            t = th * b_half + b_half
            acc[s] = t if acc[s] is None else acc[s] + t
    out = [_dot(t.astype(BF16), wout_ref[...]) for t in acc]
    for s in subs:
        o_ref[s] = _stream_tile(x_ref, c_ref, s, n_lat_tiles) + g1_ref[s] * out[s]


def _merge(xs, mod, acts, consts, n_lat_tiles, n_tiles):
    nb, _, d = xs[0].shape
    h, yf, yb, u5, of, ob, z, yfn, ysg = acts
    dskip, gluw, glub, gain, avg, wgate, wbr, wout = consts
    return pl.pallas_call(
        functools.partial(_merge_kernel, n_lat_tiles=n_lat_tiles, split=len(xs) == 2),
        grid=(nb // PAIR, n_tiles),
        in_specs=_stream_specs(xs, d, PAIR, n_lat_tiles)
                 + [_tile_spec(d), _mod_spec(nb, d, 2, n_lat_tiles),
                  _tile_spec(W), _tile_spec(W), _tile_spec(W),
                  _resident(dskip.shape), _resident(gluw.shape), _resident(glub.shape),
                  _tile_spec(W), _tile_spec(W), _tile_spec(W),
                  _resident(gain.shape), _resident(avg.shape),
                  _tile_spec(W), _tile_spec(W), _resident(wgate.shape),
                  _resident(wbr.shape), _resident(wout.shape)],
        out_specs=_tile_spec(d),
        out_shape=jax.ShapeDtypeStruct((nb, n_tiles * ROW_TILE, d), F32),
        compiler_params=_cparams("parallel", "parallel"),
        name="merge",
    )(*xs, h, mod, yf, yb, u5, dskip, gluw, glub, of, ob, z, gain, avg, yfn, ysg, wgate, wbr, wout)


def _ffn_kernel(x_ref, sh_ref, sc_ref, g2_ref, gain_ref, w1_ref, w2_ref, nf_ref, o_ref, *, final):
    dff = w2_ref.shape[0]
    subs = range(x_ref.shape[0])
    h = [_modulated_norm(x_ref[s], gain_ref[...], sc_ref[s], sh_ref[s]).astype(BF16) for s in subs]
    t = [_dot(h_, w1_ref[...]) for h_ in h]
    act = [(_silu(t_[:, :dff]) * t_[:, dff:]).astype(BF16) for t_ in t]
    out = [_dot(a_, w2_ref[...]) for a_ in act]
    for s in subs:
        y = x_ref[s] + g2_ref[s] * out[s]
        if final:
            y = y * lax.rsqrt(jnp.mean(y * y, axis=-1, keepdims=True) + EPS) * nf_ref[...]
        o_ref[s] = y


def _ffn(x3, mod, gain, w1, w2, norm_f, n_lat_tiles, final):
    nb, ttot, d = x3.shape
    return pl.pallas_call(
        functools.partial(_ffn_kernel, final=final),
        grid=(nb // PAIR, ttot // ROW_TILE),
        in_specs=[_tile_spec(d), _mod_spec(nb, d, 3, n_lat_tiles), _mod_spec(nb, d, 4, n_lat_tiles),
                  _mod_spec(nb, d, 5, n_lat_tiles),
                  _resident(gain.shape), _resident(w1.shape), _resident(w2.shape),
                  _resident(norm_f.shape)],
        out_specs=_tile_spec(d),
        out_shape=jax.ShapeDtypeStruct(x3.shape, F32),
        compiler_params=_cparams("parallel", "parallel"),
        name="ffn",
    )(x3, mod, mod, mod, gain, w1, w2, norm_f)


def _dft_tables(n):
    idx = np.arange(n, dtype=np.int64)
    ang = 2.0 * np.pi * ((idx[:, None] * idx[None, :]) % n).astype(np.float64) / n
    return np.cos(ang), np.sin(ang)


def _s5_tables(lam_re, lam_im, log_dt, b_re, b_im, c_re, c_im):
    g, p, cg = S5_GROUPS, S5_STATE, S5_GROUP
    lam = lax.complex(lam_re.astype(F32), lam_im.astype(F32))
    a_bar = jnp.exp(lam * jnp.exp(log_dt.astype(F32))[:, None])
    b_bar = ((a_bar - 1.0) / lam)[..., None] * lax.complex(b_re.astype(F32), b_im.astype(F32))
    eye = jnp.eye(g, dtype=F32)
    bm_re = jnp.einsum('gpc,gh->gchp', b_bar.real, eye).reshape(g * cg, g * p)
    bm_im = jnp.einsum('gpc,gh->gchp', b_bar.imag, eye).reshape(g * cg, g * p)
    cm_re = jnp.einsum('gcp,gh->gphc', c_re.astype(F32), eye).reshape(g * p, g * cg)
    cm_im = jnp.einsum('gcp,gh->gphc', c_im.astype(F32), eye).reshape(g * p, g * cg)
    ng, h = N_S5 // S5_LANE_GROUP, S5_LANE_GROUP
    bmat = jnp.concatenate([bm_re.reshape(W, ng, h), bm_im.reshape(W, ng, h)], axis=2)
    bmat = jnp.transpose(bmat, (1, 0, 2)).astype(BF16)
    cmat = jnp.concatenate([cm_re.reshape(ng, h, W), -cm_im.reshape(ng, h, W)], axis=1).astype(BF16)
    avec = jnp.concatenate([a_bar.real.reshape(ng, 1, h), a_bar.imag.reshape(ng, 1, h)], axis=2)
    return bmat, cmat, avec


def _layer_consts(l, p):
    d = p['w_in'].shape[1]
    w_in = p['w_in'][l]
    sizes = (W, W, W, 2 * GDN_HEADS, 2 * GDN_HEADS, W, W, W, W, W, N_BRANCH * d)
    offs = np.concatenate([[0], np.cumsum(sizes)])
    col = lambda j: w_in[:, offs[j]:offs[j + 1]]
    u5, k, v, a, bt, q, z, ufn, usg, vsg, gate = [col(j) for j in range(len(sizes))]
    cc, sc = _dft_tables(FN_GROUP)
    fw = p['fn_w'][l].astype(F32)
    wc = jnp.einsum('cd,gde->gce', jnp.asarray(cc, F32), fw, precision=HIGHEST)
    ws = jnp.einsum('cd,gde->gce', jnp.asarray(sc, F32), fw, precision=HIGHEST)
    ufn_g = ufn.reshape(d, FN_GROUPS, FN_GROUP)
    fold_c = jnp.einsum('kgc,gce->kge', ufn_g, wc, precision=HIGHEST).reshape(d, W)
    fold_s = jnp.einsum('kgc,gce->kge', ufn_g, ws, precision=HIGHEST).reshape(d, W)
    w_ab = jnp.concatenate([a, bt, jnp.zeros((d, LANES - 4 * GDN_HEADS), F32)], axis=1)
    ws_in = [u5, jnp.concatenate([k, v, q], axis=1), w_ab, z,
             jnp.concatenate([fold_c, fold_s], axis=1), jnp.concatenate([usg, vsg], axis=1)]
    ws_in = [w.astype(BF16) for w in ws_in]
    w_gate = (0.5 * gate).astype(BF16)

    s5 = [_s5_tables(p['s5_lam_re'][l, dr], p['s5_lam_im'][l, dr], p['s5_log_dt'][l, dr],
                     p['s5_b_re'][l, dr], p['s5_b_im'][l, dr], p['s5_c_re'][l, dr],
                     p['s5_c_im'][l, dr]) for dr in range(2)]

    conv = p['gdn_conv'][l].astype(F32)
    cw = jnp.transpose(conv, (2, 0, 1)).reshape(conv.shape[2], 3 * W)
    pad = jnp.zeros((1, LANES - 2 * GDN_HEADS), F32)
    alog = jnp.concatenate([p['gdn_a_log'][l].astype(F32).reshape(1, -1), pad], axis=1)
    dtb = jnp.concatenate([p['gdn_dt_bias'][l].astype(F32).reshape(1, -1), pad], axis=1)

    sg_bias = jnp.repeat(p['sg_b'][l].astype(F32).T, SG_GROUP, axis=1)
    merge_consts = (p['s5_d'][l].astype(F32).reshape(1, W), p['s5_glu_w'][l].astype(BF16),
                    p['s5_glu_b'][l].astype(F32).reshape(1, W),
                    jnp.tile(p['gdn_norm'][l].astype(F32), GDN_HEADS).reshape(1, W))
    return dict(
        ws_in=ws_in, w_gate=w_gate, s5=s5, cw=cw, alog=alog, dtb=dtb,
        sg_w=p['sg_w'][l].astype(BF16), sg_bias=sg_bias,
        sg_lng=p['sg_ln_g'][l].astype(F32).reshape(1, W), sg_lnb=p['sg_ln_b'][l].astype(F32).reshape(1, W),
        merge=merge_consts, wbr_half=(0.5 * p['w_branch'][l]).astype(BF16),
        wout=p['w_out'][l].astype(BF16),
        w1=p['ffn_w1'][l].astype(BF16), w2=p['ffn_w2'][l].astype(BF16),
        norm1=p['norm1'][l].astype(F32).reshape(1, d), norm2=p['norm2'][l].astype(F32).reshape(1, d))


def kernel(x, c, ctx, c_ctx, ada_w, ada_b, norm1, norm2, w_in, s5_lam_re, s5_lam_im, s5_log_dt, s5_b_re, s5_b_im, s5_c_re, s5_c_im, s5_d, s5_glu_w, s5_glu_b, fn_w, gdn_conv, gdn_a_log, gdn_dt_bias, gdn_norm, sg_ln_g, sg_ln_b, sg_w, sg_b, w_branch, w_out, ffn_w1, ffn_w2, norm_f):
    p = dict(w_in=w_in, s5_lam_re=s5_lam_re, s5_lam_im=s5_lam_im, s5_log_dt=s5_log_dt,
             s5_b_re=s5_b_re, s5_b_im=s5_b_im, s5_c_re=s5_c_re, s5_c_im=s5_c_im, s5_d=s5_d,
             s5_glu_w=s5_glu_w, s5_glu_b=s5_glu_b, fn_w=fn_w, gdn_conv=gdn_conv,
             gdn_a_log=gdn_a_log, gdn_dt_bias=gdn_dt_bias, gdn_norm=gdn_norm, sg_ln_g=sg_ln_g,
             sg_ln_b=sg_ln_b, sg_w=sg_w, sg_b=sg_b, w_branch=w_branch, w_out=w_out,
             ffn_w1=ffn_w1, ffn_w2=ffn_w2, norm1=norm1, norm2=norm2)
    nb, l_lat, d = x.shape
    l_ctx = ctx.shape[1]
    depth = ada_w.shape[0]
    ttot = l_lat + l_ctx
    assert nb % SUBLANES == 0 and nb % PAIR == 0 and nb % KIN_GROUP == 0
    assert d == N_BRANCH * W and l_lat % l_ctx == 0
    assert l_lat % ROW_TILE == 0 and l_ctx % ROW_TILE == 0 and ROW_TILE % SG_CHUNK == 0
    n_lat_tiles = l_lat // ROW_TILE
    n_tiles = ttot // ROW_TILE
    sg_rows = max(r for r in range(SG_CHUNK, 7 * SG_CHUNK, SG_CHUNK) if ttot % r == 0)
    gdn_group = math.gcd(nb, GDN_GROUP)

    if l_ctx == ROW_TILE:
        xs = (x.astype(F32), ctx.astype(F32))
    else:
        xs = (jnp.concatenate([x, ctx], axis=1).astype(F32),)

    n_cond = nb + max(PAIR, KIN_GROUP)
    ada_rows = -(-n_cond // SUBLANES) * SUBLANES
    cc = jnp.zeros((ada_rows, d), F32).at[:nb].set(c.astype(F32)).at[nb:n_cond].set(c_ctx.astype(F32))
    mods = _ada(cc, ada_w.astype(F32), ada_b.astype(F32))[:, :n_cond].reshape(depth, n_cond, 1, 6 * d)

    cos_l, sin_l = _dft_tables(l_lat)
    cos_c, sin_c = _dft_tables(l_ctx)
    sc_l = 1.0 / math.sqrt(l_lat * FN_GROUP)
    sc_c = 1.0 / math.sqrt(l_ctx * FN_GROUP)
    cos_l, nsin_l = jnp.asarray(cos_l * sc_l, BF16), jnp.asarray(-sin_l * sc_l, BF16)
    cos_c, nsin_c = jnp.asarray(cos_c * sc_c, BF16), jnp.asarray(-sin_c * sc_c, BF16)
    nf = norm_f.astype(F32).reshape(1, d)

    lane_head = np.arange(W) // GDN_HEAD_DIM
    seg = (lane_head[:, None] == lane_head[None, :]).astype(np.float32)
    seg_ones = jnp.asarray(seg, BF16)
    seg_avg = jnp.asarray(seg / GDN_HEAD_DIM, BF16)
    expand = []
    for dr in range(2):
        e = np.zeros((LANES, 2 * W), np.float32)
        for h in range(GDN_HEADS):
            e[dr * GDN_HEADS + h, :W] = lane_head == h
            e[2 * GDN_HEADS + dr * GDN_HEADS + h, W:] = lane_head == h
        expand.append(jnp.asarray(e, BF16))
    gdn_masks = [_gdn_masks(False), _gdn_masks(True)]
    bd = jnp.asarray(seg, F32)
    bd16 = jnp.asarray(np.concatenate([seg, seg], axis=1), BF16)
    src = (np.arange(nb)[None, :] * SUBLANES + np.arange(SUBLANES)[:, None]).reshape(-1)
    perm_np = np.zeros((nb * SUBLANES, nb * SUBLANES), np.float32)
    perm_np[np.arange(nb * SUBLANES), src] = 1.0
    perm, permt = jnp.asarray(perm_np, BF16), jnp.asarray(perm_np.T, BF16)

    for l in range(depth):
        k = _layer_consts(l, p)
        mod = mods[l]
        last = l == depth - 1
        u5, kvq_p, gb, z, afn, sg, h = _kin(xs, mod, k['norm1'], k['ws_in'],
                                            (k['cw'], seg_ones, k['alog'], k['dtb']), n_lat_tiles, n_tiles)
        yf, yb = _s5(u5, perm, permt, [t[0] for t in k['s5']], [t[1] for t in k['s5']],
                     [t[2] for t in k['s5']], l_lat // S5_STEPS)
        yfn = _fnet(afn, cos_l, nsin_l, cos_c, nsin_c)
        ysg = _sgu(sg, k['sg_w'], k['sg_bias'], k['sg_lng'], k['sg_lnb'], seg_avg, sg_rows)
        of, ob = _gdn_chunk(kvq_p, gb, expand, gdn_masks, bd, bd16, l_lat, gdn_group)
        acts = (h, yf, yb, u5, of, ob, z, yfn, ysg)
        consts = k['merge'] + (seg_avg, k['w_gate'], k['wbr_half'], k['wout'])
        x3 = _merge(xs, mod, acts, consts, n_lat_tiles, n_lat_tiles if last else n_tiles)
        xs = (_ffn(x3, mod, k['norm2'], k['w1'], k['w2'], nf, n_lat_tiles, final=last),)

    return xs[0].astype(x.dtype)
```
